```python
import math
import jax, jax.numpy as jnp
from jax import lax
import numpy as np

D_MODEL = 1024
BATCH = 32
SEQ = 2048
DEPTH = 2
DEC_BATCH = 4
DEC_SEQ = 8192
PAST_LEN = 128

N_EVEN = (DEPTH + 1) // 2
N_ODD = DEPTH // 2
FNET_GROUPS = 4
FNET_GROUP_DIM = D_MODEL // 16
FNET_WIDTH = FNET_GROUPS * FNET_GROUP_DIM
ATT_HEAD_DIM = 64
ATT_HEADS = (D_MODEL - FNET_WIDTH) // ATT_HEAD_DIM
ATT_WIDTH = ATT_HEADS * ATT_HEAD_DIM
DILATED_PATTERNS = ((128, 1), (512, 4), (2048, 16))
EVEN_IN = FNET_WIDTH + 3 * ATT_WIDTH
RET_HEADS = 4
RET_DK = D_MODEL // RET_HEADS
RET_DV = 2 * RET_DK
RET_CHUNK = 128
RET_QK_WIDTH = RET_HEADS * RET_DK
RET_V_WIDTH = RET_HEADS * RET_DV
ODD_IN = 2 * RET_QK_WIDTH + 2 * RET_V_WIDTH
D_FF = 2816
N_EXPERTS = 8
TOP_K = 2
D_FF_EXPERT = 3584
EPS = 1e-6
NEG = -1e30

kernel_name = "hybrid_fnet_dilated_retention_encoder"


def _rmsnorm(x, w):
    xf = x.astype(jnp.float32)
    y = xf * lax.rsqrt(jnp.mean(xf * xf, -1, keepdims=True) + EPS)
    return (y * w.astype(jnp.float32)).astype(x.dtype)


def _alibi_slopes(n):
    return jnp.exp2(-8.0 * (jnp.arange(n, dtype=jnp.float32) + 1.0) / n)


def _fourier_mix(u):
    b, s, _ = u.shape
    g = u.astype(jnp.float32).reshape(b, s, FNET_GROUPS, FNET_GROUP_DIM)
    f = jnp.fft.fft2(g, axes=(1, 3), norm="ortho")
    return jnp.real(f).reshape(b, s, FNET_WIDTH).astype(u.dtype)


def _dilated_pattern(q, k, v, slopes, window, dilation):
    b, s, h, d = q.shape
    r = dilation
    nw = window // (2 * r)
    blk = nw
    L = s // r
    nb = -(-L // blk)
    Lp = nb * blk

    def to_sub(t):
        return t.reshape(b, L, r, h, d).transpose(0, 2, 1, 3, 4)

    qs = jnp.pad(to_sub(q), ((0, 0), (0, 0), (0, Lp - L), (0, 0), (0, 0))).reshape(b, r, nb, blk, h, d)
    pad_kv = ((0, 0), (0, 0), (blk, Lp - L + blk), (0, 0), (0, 0))
    kp = jnp.pad(to_sub(k), pad_kv)
    vp = jnp.pad(to_sub(v), pad_kv)

    def windows(t):
        return jnp.concatenate(
            [t[:, :, o * blk:o * blk + Lp].reshape(b, r, nb, blk, h, d) for o in range(3)], axis=3)

    kw = windows(kp)
    vw = windows(vp)
    qi = jnp.arange(blk)[:, None]
    kj = jnp.arange(3 * blk)[None, :]
    dist = jnp.abs(kj - blk - qi)
    lk = (jnp.arange(nb)[:, None] - 1) * blk + jnp.arange(3 * blk)[None, :]
    valid = (dist <= nw)[None] & ((lk >= 0) & (lk < L))[:, None, :]
    bias = -(slopes[None, :, None] * (r * dist).astype(jnp.float32)[:, None, :])
    sc = jnp.einsum('brnqhd,brnkhd->brnqhk', qs, kw) * (d ** -0.5) + bias
    sc = jnp.where(valid[None, None, :, :, None, :], sc, NEG)
    m = jnp.max(sc, -1)
    e = jnp.exp(sc - m[..., None])
    den = jnp.sum(e, -1)
    num = jnp.einsum('brnqhk,brnkhd->brnqhd', e, vw)

    def from_sub(t):
        t = t.reshape((b, r, Lp) + t.shape[4:])[:, :, :L]
        return jnp.swapaxes(t, 1, 2).reshape((b, s) + t.shape[3:])

    return from_sub(m), from_sub(den), from_sub(num)


def _dilated_attention(q, k, v, qn_w, kn_w):
    qf = _rmsnorm(q, qn_w).astype(jnp.float32)
    kf = _rmsnorm(k, kn_w).astype(jnp.float32)
    vf = v.astype(jnp.float32)
    slopes = _alibi_slopes(ATT_HEADS)
    parts = [_dilated_pattern(qf, kf, vf, slopes, w, r) for (w, r) in DILATED_PATTERNS]
    m_all = jnp.max(jnp.stack([p[0] for p in parts]), 0)
    scales = [jnp.exp(p[0] - m_all) for p in parts]
    den = sum(a * p[1] for a, p in zip(scales, parts))
    num = sum(a[..., None] * p[2] for a, p in zip(scales, parts))
    return (num / den[..., None]).astype(v.dtype)


def _even_mixer(h, w_in, w_out, qn_w, kn_w):
    b, s, _ = h.shape
    z = h @ w_in
    u, q, k, v = jnp.split(z, [FNET_WIDTH, FNET_WIDTH + ATT_WIDTH, FNET_WIDTH + 2 * ATT_WIDTH], axis=-1)
    f = _fourier_mix(u)
    shp = (b, s, ATT_HEADS, ATT_HEAD_DIM)
    a = _dilated_attention(q.reshape(shp), k.reshape(shp), v.reshape(shp), qn_w, kn_w).reshape(b, s, ATT_WIDTH)
    return jnp.concatenate([f, a.astype(f.dtype)], -1) @ w_out


def _retention_dir(q, k, v, log_gamma, strict):
    b, h, s, dk = q.shape
    dv = v.shape[-1]
    c = RET_CHUNK
    n = s // c
    qc = q.reshape(b, h, n, c, dk)
    kc = k.reshape(b, h, n, c, dk)
    vc = v.reshape(b, h, n, c, dv)
    pos = jnp.arange(c, dtype=jnp.float32)
    rel = pos[:, None] - pos[None, :]
    keep = (rel > 0) if strict else (rel >= 0)
    dmask = jnp.where(keep[None], jnp.exp(log_gamma[:, None, None] * jnp.maximum(rel, 0.0)[None]), 0.0)
    intra = jnp.einsum('bhnij,bhnje->bhnie',
                       jnp.einsum('bhnid,bhnjd->bhnij', qc, kc) * dmask[None, :, None], vc)
    xi = jnp.exp(log_gamma[:, None] * (pos + 1.0)[None])
    zeta = jnp.exp(log_gamma[:, None] * (c - 1.0 - pos)[None])
    chunk_decay = jnp.exp(log_gamma * c)

    def step(state, inp):
        q_n, k_n, v_n = inp
        cross = jnp.einsum('bhid,bhde->bhie', q_n * xi[None, :, :, None], state)
        state = state * chunk_decay[None, :, None, None] + jnp.einsum(
            'bhjd,bhje->bhde', k_n * zeta[None, :, :, None], v_n)
        return state, cross

    init = jnp.zeros((b, h, dk, dv), jnp.float32)
    _, cross = lax.scan(step, init, (jnp.moveaxis(qc, 2, 0), jnp.moveaxis(kc, 2, 0), jnp.moveaxis(vc, 2, 0)))
    out = intra + jnp.moveaxis(cross, 0, 2)
    return out.reshape(b, h, s, dv)


def _odd_mixer(h, w_in, w_out, logg_f, logg_b, gn_w):
    b, s, _ = h.shape
    z = h @ w_in
    q, k, v, g = jnp.split(z, [RET_QK_WIDTH, 2 * RET_QK_WIDTH, 2 * RET_QK_WIDTH + RET_V_WIDTH], axis=-1)

    def heads(t, d):
        return t.astype(jnp.float32).reshape(b, s, RET_HEADS, d).transpose(0, 2, 1, 3)

    qh = heads(q, RET_DK)
    kh = heads(k, RET_DK) * (RET_DK ** -0.5)
    vh = heads(v, RET_DV)
    fwd = _retention_dir(qh, kh, vh, logg_f.astype(jnp.float32), False)
    bwd = jnp.flip(_retention_dir(jnp.flip(qh, 2), jnp.flip(kh, 2), jnp.flip(vh, 2),
                                  logg_b.astype(jnp.float32), True), 2)
    y = (fwd + bwd).transpose(0, 2, 1, 3)
    y = y * lax.rsqrt(jnp.mean(y * y, -1, keepdims=True) + EPS)
    y = y.reshape(b, s, RET_V_WIDTH) * gn_w.astype(jnp.float32)
    return (jax.nn.silu(g.astype(jnp.float32)) * y).astype(h.dtype) @ w_out


def _swiglu(h, w_gate, w_up, w_down):
    return (jax.nn.silu(h @ w_gate) * (h @ w_up)) @ w_down


def _moe(h, w_router, w1, w3, w2):
    b, s, d = h.shape
    t = h.reshape(b * s, d)
    logits = (t @ w_router).astype(jnp.float32)
    top_val, top_idx = lax.top_k(logits, TOP_K)
    top_w = jax.nn.softmax(top_val, -1)
    gates = jnp.sum(jax.nn.one_hot(top_idx, N_EXPERTS, dtype=jnp.float32) * top_w[..., None], 1)
    out = jnp.zeros((b * s, d), jnp.float32)
    for e in range(N_EXPERTS):
        he = jax.nn.silu(t @ w1[e]) * (t @ w3[e])
        out = out + gates[:, e:e + 1] * (he @ w2[e]).astype(jnp.float32)
    return out.astype(h.dtype).reshape(b, s, d)


def _trunk(x, ln_mix_e, w_in_e, w_out_e, qn_e, kn_e, ln_ffn_e, w_gate_e, w_up_e, w_down_e,
           ln_mix_o, w_in_o, w_out_o, logdecay_fwd, logdecay_bwd, ret_gn, ln_ffn_o,
           w_router, moe_w1, moe_w3, moe_w2):
    for layer in range(DEPTH):
        i = layer // 2
        if layer % 2 == 0:
            x = x + _even_mixer(_rmsnorm(x, ln_mix_e[i]), w_in_e[i], w_out_e[i], qn_e[i], kn_e[i])
            x = x + _swiglu(_rmsnorm(x, ln_ffn_e[i]), w_gate_e[i], w_up_e[i], w_down_e[i])
        else:
            x = x + _odd_mixer(_rmsnorm(x, ln_mix_o[i]), w_in_o[i], w_out_o[i],
                               logdecay_fwd[i], logdecay_bwd[i], ret_gn[i])
            x = x + _moe(_rmsnorm(x, ln_ffn_o[i]), w_router[i], moe_w1[i], moe_w3[i], moe_w2[i])
    return x


def setup_inputs(seed: int = 0) -> dict:
    key = jax.random.key(seed)
    ks = jax.random.split(key, 24)
    f32 = jnp.float32

    def nrm(k, shape, fan_in):
        return jax.random.normal(k, shape, f32) * (fan_in ** -0.5)

    def gain(k, shape):
        return 1.0 + 0.02 * jax.random.normal(k, shape, f32)

    base_decay = jnp.log(1.0 - jnp.exp2(-5.0 - jnp.arange(RET_HEADS, dtype=f32)))
    return {
        "x_prompt": jax.random.normal(ks[0], (BATCH, SEQ, D_MODEL), f32),
        "x_sample": jax.random.normal(ks[1], (DEC_BATCH, DEC_SEQ, D_MODEL), f32),
        "ln_mix_e": gain(ks[2], (N_EVEN, D_MODEL)),
        "w_in_e": nrm(ks[3], (N_EVEN, D_MODEL, EVEN_IN), D_MODEL),
        "w_out_e": nrm(ks[4], (N_EVEN, D_MODEL, D_MODEL), D_MODEL),
        "qn_e": gain(ks[5], (N_EVEN, ATT_HEAD_DIM)),
        "kn_e": gain(ks[6], (N_EVEN, ATT_HEAD_DIM)),
        "ln_ffn_e": gain(ks[7], (N_EVEN, D_MODEL)),
        "w_gate_e": nrm(ks[8], (N_EVEN, D_MODEL, D_FF), D_MODEL),
        "w_up_e": nrm(ks[9], (N_EVEN, D_MODEL, D_FF), D_MODEL),
        "w_down_e": nrm(ks[10], (N_EVEN, D_FF, D_MODEL), D_FF),
        "ln_mix_o": gain(ks[11], (N_ODD, D_MODEL)),
        "w_in_o": nrm(ks[12], (N_ODD, D_MODEL, ODD_IN), D_MODEL),
        "w_out_o": nrm(ks[13], (N_ODD, RET_V_WIDTH, D_MODEL), RET_V_WIDTH),
        "logdecay_fwd": base_decay[None] * (1.0 + 0.05 * jax.random.normal(ks[14], (N_ODD, RET_HEADS), f32)),
        "logdecay_bwd": base_decay[None] * (1.0 + 0.05 * jax.random.normal(ks[15], (N_ODD, RET_HEADS), f32)),
        "ret_gn": gain(ks[16], (N_ODD, RET_V_WIDTH)),
        "ln_ffn_o": gain(ks[17], (N_ODD, D_MODEL)),
        "w_router": nrm(ks[18], (N_ODD, D_MODEL, N_EXPERTS), D_MODEL),
        "moe_w1": nrm(ks[19], (N_ODD, N_EXPERTS, D_MODEL, D_FF_EXPERT), D_MODEL),
        "moe_w3": nrm(ks[20], (N_ODD, N_EXPERTS, D_MODEL, D_FF_EXPERT), D_MODEL),
        "moe_w2": nrm(ks[21], (N_ODD, N_EXPERTS, D_FF_EXPERT, D_MODEL), D_FF_EXPERT),
    }


def reference(x_prompt, x_sample, ln_mix_e, w_in_e, w_out_e, qn_e, kn_e, ln_ffn_e, w_gate_e, w_up_e,
              w_down_e, ln_mix_o, w_in_o, w_out_o, logdecay_fwd, logdecay_bwd, ret_gn, ln_ffn_o,
              w_router, moe_w1, moe_w3, moe_w2):
    y_prompt = _trunk(x_prompt, ln_mix_e, w_in_e, w_out_e, qn_e, kn_e, ln_ffn_e, w_gate_e, w_up_e, w_down_e,
                      ln_mix_o, w_in_o, w_out_o, logdecay_fwd, logdecay_bwd, ret_gn, ln_ffn_o,
                      w_router, moe_w1, moe_w3, moe_w2)
    y_sample = _trunk(x_sample, ln_mix_e, w_in_e, w_out_e, qn_e, kn_e, ln_ffn_e, w_gate_e, w_up_e, w_down_e,
                      ln_mix_o, w_in_o, w_out_o, logdecay_fwd, logdecay_bwd, ret_gn, ln_ffn_o,
                      w_router, moe_w1, moe_w3, moe_w2)
    return (y_prompt, y_sample)
```

```python
import functools
import math

import jax
import jax.numpy as jnp
from jax import lax
from jax.experimental import pallas as pl
from jax.experimental.pallas import tpu as pltpu

D_MODEL = 1024
FNET_GROUPS = 4
FNET_GROUP_DIM = 64
FNET_WIDTH = 256
ATT_HEAD_DIM = 64
ATT_HEADS = 12
ATT_WIDTH = 768
ATT_HALF_WINDOW = 64
DILATIONS = (1, 4, 16)
EVEN_IN = FNET_WIDTH + 3 * ATT_WIDTH
RET_HEADS = 4
RET_DK = 256
RET_DV = 512
RET_QK_WIDTH = 1024
RET_V_WIDTH = 2048
ODD_IN = 6144
D_FF = 2816
N_EXPERTS = 8
D_FF_EXPERT = 3584
EPS = 1e-6
NEG = -1e30

LANES = 128
VMEM_LIMIT_BYTES = 56 * 2**20

F32 = jnp.float32
BF16 = jnp.bfloat16


def _params(semantics):
    return pltpu.CompilerParams(dimension_semantics=semantics, vmem_limit_bytes=VMEM_LIMIT_BYTES)


def _rms(x, w):
    return x * lax.rsqrt(jnp.mean(x * x, axis=-1, keepdims=True) + EPS) * w


def _silu(x):
    return x / (1.0 + jnp.exp(-x))


def _even_in_kernel(x_ref, ln_ref, w_ref, cd_ref, hs_ref, qn_ref, kn_ref, y_ref, qkv_ref):
    h = _rms(x_ref[0], ln_ref[...]).astype(BF16)
    u = jnp.dot(h, w_ref[:, 0:FNET_WIDTH], preferred_element_type=F32).astype(BF16)
    yy = jnp.dot(u, cd_ref[...], preferred_element_type=F32)
    y_ref[0] = yy[:, :FNET_WIDTH].astype(BF16)
    y_ref[1] = yy[:, FNET_WIDTH:].astype(BF16)
    for part, n_ref in ((0, qn_ref), (1, kn_ref)):
        c0 = FNET_WIDTH + part * ATT_WIDTH
        z = jnp.dot(h, w_ref[:, c0:c0 + ATT_WIDTH], preferred_element_type=F32)
        for j in range(ATT_WIDTH // LANES):
            zj = z[:, j * LANES:(j + 1) * LANES]
            ss = jnp.dot((zj * zj).astype(BF16), hs_ref[...], preferred_element_type=F32)
            zn = zj * lax.rsqrt(ss * (1.0 / ATT_HEAD_DIM) + EPS) * n_ref[...]
            qkv_ref[0, :, part * ATT_WIDTH + j * LANES:part * ATT_WIDTH + (j + 1) * LANES] = zn.astype(BF16)
    c0 = FNET_WIDTH + 2 * ATT_WIDTH
    v = jnp.dot(h, w_ref[:, c0:c0 + ATT_WIDTH], preferred_element_type=F32)
    qkv_ref[0, :, 2 * ATT_WIDTH:3 * ATT_WIDTH] = v.astype(BF16)


def _even_in(x, ln, w_in, cd, hs, qn, kn, tm=512):
    b, s, d = x.shape
    full = lambda shp: pl.BlockSpec(shp, lambda bi, i: (0,) * len(shp))
    return pl.pallas_call(
        _even_in_kernel,
        grid=(b, s // tm),
        in_specs=[
            pl.BlockSpec((1, tm, d), lambda bi, i: (bi, i, 0)),
            full((1, d)), full((d, EVEN_IN)), full((FNET_WIDTH, 2 * FNET_WIDTH)), full((LANES, LANES)),
            full((1, LANES)), full((1, LANES)),
        ],
        out_specs=[
            pl.BlockSpec((2, tm, FNET_WIDTH), lambda bi, i: (0, i, bi)),
            pl.BlockSpec((1, tm, 3 * ATT_WIDTH), lambda bi, i: (bi, i, 0)),
        ],
        out_shape=[
            jax.ShapeDtypeStruct((2, s, b * FNET_WIDTH), BF16),
            jax.ShapeDtypeStruct((b, s, 3 * ATT_WIDTH), BF16),
        ],
        compiler_params=_params(("parallel", "parallel")),
        name="even_in_proj",
    )(x, ln, w_in, cd, hs, qn, kn)


def _matmul_kernel(a_ref, b_ref, o_ref, acc_ref):
    @pl.when(pl.program_id(2) == 0)
    def _():
        acc_ref[...] = jnp.zeros_like(acc_ref)

    acc_ref[...] += jnp.dot(a_ref[...], b_ref[...], preferred_element_type=F32)

    @pl.when(pl.program_id(2) == pl.num_programs(2) - 1)
    def _():
        o_ref[...] = acc_ref[...].astype(o_ref.dtype)


def _matmul(a, b, tm=1024, tn=2048, tk=1024):
    m, k = a.shape
    _, n = b.shape
    tm, tn, tk = min(tm, m), min(tn, n), min(tk, k)
    return pl.pallas_call(
        _matmul_kernel,
        grid=(m // tm, n // tn, k // tk),
        in_specs=[pl.BlockSpec((tm, tk), lambda i, j, kk: (i, kk)),
                  pl.BlockSpec((tk, tn), lambda i, j, kk: (kk, j))],
        out_specs=pl.BlockSpec((tm, tn), lambda i, j, kk: (i, j)),
        out_shape=jax.ShapeDtypeStruct((m, n), BF16),
        scratch_shapes=[pltpu.VMEM((tm, tn), F32)],
        compiler_params=_params(("parallel", "parallel", "arbitrary")),
        name="seq_dft_matmul",
    )(a, b)


ATT_QBLK = 128
ATT_CONV_ROWS = 512


def _attn_kernel(slopes_ref, q_ref, k_ref, v_ref, o_ref, qf, kf, vf, m_s, l_s, a_s, *, seq):
    pair = pl.program_id(1)

    def to_f32(c, carry):
        r0 = pl.multiple_of(c * ATT_CONV_ROWS, ATT_CONV_ROWS)
        rows = pl.ds(r0, ATT_CONV_ROWS)
        qf[rows, :] = q_ref[0, rows, :].astype(F32)
        kf[rows, :] = k_ref[0, rows, :].astype(F32)
        vf[rows, :] = v_ref[0, rows, :].astype(F32)
        return carry

    lax.fori_loop(0, seq // ATT_CONV_ROWS, to_f32, 0)

    lane = lax.broadcasted_iota(jnp.int32, (1, LANES), 1)
    first_head = lane < ATT_HEAD_DIM
    slope0 = slopes_ref[2 * pair]
    slope1 = slopes_ref[2 * pair + 1]

    for pattern, r in enumerate(DILATIONS):
        sub_len = seq // r
        n_keys = min(2 * ATT_QBLK, sub_len)
        blocks_per_class = sub_len // ATT_QBLK

        def block(t, carry, r=r, sub_len=sub_len, n_keys=n_keys, blocks_per_class=blocks_per_class,
                  pattern=pattern):
            rho = t // blocks_per_class
            l0 = (t % blocks_per_class) * ATT_QBLK
            ks = jnp.clip(l0 - ATT_HALF_WINDOW, 0, sub_len - n_keys)
            if r == 1:
                q_rows = pl.ds(pl.multiple_of(l0, ATT_QBLK), ATT_QBLK)
                k_rows = pl.ds(pl.multiple_of(ks, 8), n_keys)
            else:
                q_rows = pl.ds(rho + r * l0, ATT_QBLK, stride=r)
                k_rows = pl.ds(rho + r * ks, n_keys, stride=r)
            qb = qf[q_rows, :]
            kb = kf[k_rows, :].astype(BF16)
            vb = vf[k_rows, :].astype(BF16)
            qi = lax.broadcasted_iota(jnp.int32, (ATT_QBLK, n_keys), 0) + l0
            kj = lax.broadcasted_iota(jnp.int32, (ATT_QBLK, n_keys), 1) + ks
            dist = jnp.abs(kj - qi)
            valid = dist <= ATT_HALF_WINDOW
            distance = (dist * r).astype(F32)

            stats = []
            for head_mask, slope in ((first_head, slope0), (jnp.logical_not(first_head), slope1)):
                qm = jnp.where(head_mask, qb, 0.0).astype(BF16)
                sc = lax.dot_general(qm, kb, (((1,), (1,)), ((), ())), preferred_element_type=F32)
                sc = jnp.where(valid, sc - slope * distance, NEG)
                m = jnp.max(sc, axis=-1, keepdims=True)
                p = jnp.exp(sc - m)
                l = jnp.sum(p, axis=-1, keepdims=True)
                pv = jnp.dot(p.astype(BF16), vb, preferred_element_type=F32)
                stats.append((m, l, pv))
            (m0, l0_, pv0), (m1, l1_, pv1) = stats
            m_new = jnp.where(first_head, m0, m1)
            l_new = jnp.where(first_head, l0_, l1_)
            a_new = jnp.where(first_head, pv0, pv1)
            if pattern == 0:
                m_s[q_rows, :] = m_new
                l_s[q_rows, :] = l_new
                a_s[q_rows, :] = a_new
            else:
                m_old = m_s[q_rows, :]
                m_tot = jnp.maximum(m_old, m_new)
                w_old = jnp.exp(m_old - m_tot)
                w_new = jnp.exp(m_new - m_tot)
                m_s[q_rows, :] = m_tot
                l_s[q_rows, :] = w_old * l_s[q_rows, :] + w_new * l_new
                a_s[q_rows, :] = w_old * a_s[q_rows, :] + w_new * a_new
            return carry

        lax.fori_loop(0, seq // ATT_QBLK, block, 0)

    def finish(c, carry):
        r0 = pl.multiple_of(c * ATT_CONV_ROWS, ATT_CONV_ROWS)
        rows = pl.ds(r0, ATT_CONV_ROWS)
        o_ref[0, rows, :] = (a_s[rows, :] / l_s[rows, :]).astype(BF16)
        return carry

    lax.fori_loop(0, seq // ATT_CONV_ROWS, finish, 0)


def _attention(qkv, slopes):
    b, s, _ = qkv.shape
    n_pairs = ATT_WIDTH // LANES
    blk = lambda off: pl.BlockSpec((1, s, LANES), lambda bi, j, off=off: (bi, 0, off + j))
    return pl.pallas_call(
        functools.partial(_attn_kernel, seq=s),
        grid=(b, n_pairs),
        in_specs=[pl.BlockSpec(memory_space=pltpu.SMEM), blk(0), blk(n_pairs), blk(2 * n_pairs)],
        out_specs=pl.BlockSpec((1, s, LANES), lambda bi, j: (bi, 0, j)),
        out_shape=jax.ShapeDtypeStruct((b, s, ATT_WIDTH), BF16),
        scratch_shapes=[pltpu.VMEM((s, LANES), F32)] * 6,
        compiler_params=_params(("parallel", "parallel")),
        name="dilated_attention",
    )(slopes, qkv, qkv, qkv)


def _even_out_kernel(x_ref, f_ref, a_ref, w_ref, o_ref):
    acc = x_ref[0]
    acc = acc + jnp.dot(f_ref[...], w_ref[0:FNET_WIDTH, :], preferred_element_type=F32)
    acc = acc + jnp.dot(a_ref[0], w_ref[FNET_WIDTH:, :], preferred_element_type=F32)
    o_ref[0] = acc


def _even_out(x, fmix, att, w_out, tm=512):
    b, s, d = x.shape
    return pl.pallas_call(
        _even_out_kernel,
        grid=(b, s // tm),
        in_specs=[
            pl.BlockSpec((1, tm, d), lambda bi, i: (bi, i, 0)),
            pl.BlockSpec((tm, FNET_WIDTH), lambda bi, i: (i, bi)),
            pl.BlockSpec((1, tm, ATT_WIDTH), lambda bi, i: (bi, i, 0)),
            pl.BlockSpec((d, d), lambda bi, i: (0, 0)),
        ],
        out_specs=pl.BlockSpec((1, tm, d), lambda bi, i: (bi, i, 0)),
        out_shape=jax.ShapeDtypeStruct((b, s, d), F32),
        compiler_params=_params(("parallel", "parallel")),
        name="even_out_proj",
    )(x, fmix, att, w_out)


def _swiglu_kernel(x_ref, ln_ref, wg_ref, wu_ref, wd_ref, o_ref, h_ref, acc_ref):
    f = pl.program_id(1)

    @pl.when(f == 0)
    def _():
        x = x_ref[...]
        h_ref[...] = _rms(x, ln_ref[...]).astype(BF16)
        acc_ref[...] = x

    h = h_ref[...]
    g = jnp.dot(h, wg_ref[...], preferred_element_type=F32)
    u = jnp.dot(h, wu_ref[...], preferred_element_type=F32)
    act = (_silu(g) * u).astype(BF16)
    acc_ref[...] += jnp.dot(act, wd_ref[...], preferred_element_type=F32)

    @pl.when(f == pl.num_programs(1) - 1)
    def _():
        o_ref[...] = acc_ref[...]


def _swiglu(x, ln, wg, wu, wd, tm=512, tf=1408):
    n, d = x.shape
    dff = wg.shape[1]
    return pl.pallas_call(
        _swiglu_kernel,
        grid=(n // tm, dff // tf),
        in_specs=[
            pl.BlockSpec((tm, d), lambda i, f: (i, 0)),
            pl.BlockSpec((1, d), lambda i, f: (0, 0)),
            pl.BlockSpec((d, tf), lambda i, f: (0, f)),
            pl.BlockSpec((d, tf), lambda i, f: (0, f)),
            pl.BlockSpec((tf, d), lambda i, f: (f, 0)),
        ],
        out_specs=pl.BlockSpec((tm, d), lambda i, f: (i, 0)),
        out_shape=jax.ShapeDtypeStruct((n, d), F32),
        scratch_shapes=[pltpu.VMEM((tm, d), BF16), pltpu.VMEM((tm, d), F32)],
        compiler_params=_params(("parallel", "arbitrary")),
        name="swiglu_ffn",
    )(x, ln, wg, wu, wd)


def _odd_in_kernel(x_ref, ln_ref, w_ref, o_ref, h_ref):
    @pl.when(pl.program_id(1) == 0)
    def _():
        h_ref[...] = _rms(x_ref[...], ln_ref[...]).astype(BF16)

    o_ref[...] = jnp.dot(h_ref[...], w_ref[...], preferred_element_type=F32).astype(BF16)


def _odd_in(x, ln, w_in, tm=1024, tn=2048):
    n, d = x.shape
    return pl.pallas_call(
        _odd_in_kernel,
        grid=(n // tm, ODD_IN // tn),
        in_specs=[
            pl.BlockSpec((tm, d), lambda i, j: (i, 0)),
            pl.BlockSpec((1, d), lambda i, j: (0, 0)),
            pl.BlockSpec((d, tn), lambda i, j: (0, j)),
        ],
        out_specs=pl.BlockSpec((tm, tn), lambda i, j: (i, j)),
        out_shape=jax.ShapeDtypeStruct((n, ODD_IN), BF16),
        scratch_shapes=[pltpu.VMEM((tm, d), BF16)],
        compiler_params=_params(("parallel", "arbitrary")),
        name="odd_in_proj",
    )(x, ln, w_in)


RET_CHUNK = 256


def _chunk_positions():
    return lax.broadcasted_iota(jnp.int32, (RET_CHUNK, 1), 0).astype(F32)


def _kt_v(k_scaled, v):
    return lax.dot_general(k_scaled, v, (((0,), (0,)), ((), ())), preferred_element_type=F32)


def _ret_bwd_state_kernel(lgb_ref, k_ref, v_ref, sb_ref, state):
    head = pl.program_id(1)

    @pl.when(pl.program_id(2) == 0)
    def _():
        state[...] = jnp.zeros_like(state)

    sb_ref[0, 0, 0] = state[...].astype(BF16)
    lg = lgb_ref[head]
    pos = _chunk_positions()
    kd = (k_ref[0].astype(F32) * jnp.exp(lg * pos)).astype(BF16)
    state[...] = state[...] * jnp.exp(jnp.full((1, RET_DV), lg * RET_CHUNK, F32)) + _kt_v(kd, v_ref[0])


def _ret_bwd_states(z, lgb):
    b, s, _ = z.shape
    n = s // RET_CHUNK
    k_off = RET_QK_WIDTH // RET_DK
    v_off = 2 * RET_QK_WIDTH // RET_DV
    return pl.pallas_call(
        _ret_bwd_state_kernel,
        grid=(b, RET_HEADS, n),
        in_specs=[
            pl.BlockSpec(memory_space=pltpu.SMEM),
            pl.BlockSpec((1, RET_CHUNK, RET_DK), lambda bi, h, t: (bi, n - 1 - t, k_off + h)),
            pl.BlockSpec((1, RET_CHUNK, RET_DV), lambda bi, h, t: (bi, n - 1 - t, v_off + h)),
        ],
        out_specs=pl.BlockSpec((1, 1, 1, RET_DK, RET_DV), lambda bi, h, t: (bi, h, n - 1 - t, 0, 0)),
        out_shape=jax.ShapeDtypeStruct((b, RET_HEADS, n, RET_DK, RET_DV), BF16),
        scratch_shapes=[pltpu.VMEM((RET_DK, RET_DV), F32)],
        compiler_params=_params(("parallel", "parallel", "arbitrary")),
        name="retention_bwd_states",
    )(lgb, z, z)


def _ret_main_kernel(lgf_ref, lgb_ref, q_ref, k_ref, v_ref, g_ref, sb_ref, gn_ref, o_ref, state):
    head = pl.program_id(1)

    @pl.when(pl.program_id(2) == 0)
    def _():
        state[...] = jnp.zeros_like(state)

    lgf = lgf_ref[head]
    lgb = lgb_ref[head]
    c = RET_CHUNK
    q = q_ref[0].astype(F32)
    k = k_ref[0]
    v = v_ref[0]
    pos = _chunk_positions()

    rel = (lax.broadcasted_iota(jnp.int32, (c, c), 0) - lax.broadcasted_iota(jnp.int32, (c, c), 1)).astype(F32)
    decay = jnp.exp(jnp.where(rel >= 0, lgf * rel, -lgb * rel))
    sc = lax.dot_general(q_ref[0], k, (((1,), (1,)), ((), ())), preferred_element_type=F32)
    y = jnp.dot((sc * decay).astype(BF16), v, preferred_element_type=F32)
    qf = (q * jnp.exp(lgf * (pos + 1.0))).astype(BF16)
    qb = (q * jnp.exp(lgb * (c - pos))).astype(BF16)
    y = y + jnp.dot(qf, state[...].astype(BF16), preferred_element_type=F32)
    y = y + jnp.dot(qb, sb_ref[0, 0, 0], preferred_element_type=F32)

    kd = (k.astype(F32) * jnp.exp(lgf * (c - 1.0 - pos))).astype(BF16)
    state[...] = state[...] * jnp.exp(jnp.full((1, RET_DV), lgf * c, F32)) + _kt_v(kd, v)

    yn = _rms(y, gn_ref[...])
    o_ref[0] = (_silu(g_ref[0].astype(F32)) * yn).astype(BF16)


def _ret_main(z, sb, lgf, lgb, gn):
    b, s, _ = z.shape
    n = s // RET_CHUNK
    k_off = RET_QK_WIDTH // RET_DK
    v_off = 2 * RET_QK_WIDTH // RET_DV
    g_off = (2 * RET_QK_WIDTH + RET_V_WIDTH) // RET_DV
    smem = pl.BlockSpec(memory_space=pltpu.SMEM)
    return pl.pallas_call(
        _ret_main_kernel,
        grid=(b, RET_HEADS, n),
        in_specs=[
            smem, smem,
            pl.BlockSpec((1, RET_CHUNK, RET_DK), lambda bi, h, t: (bi, t, h)),
            pl.BlockSpec((1, RET_CHUNK, RET_DK), lambda bi, h, t: (bi, t, k_off + h)),
            pl.BlockSpec((1, RET_CHUNK, RET_DV), lambda bi, h, t: (bi, t, v_off + h)),
            pl.BlockSpec((1, RET_CHUNK, RET_DV), lambda bi, h, t: (bi, t, g_off + h)),
            pl.BlockSpec((1, 1, 1, RET_DK, RET_DV), lambda bi, h, t: (bi, h, t, 0, 0)),
            pl.BlockSpec((1, RET_DV), lambda bi, h, t: (0, h)),
        ],
        out_specs=pl.BlockSpec((1, RET_CHUNK, RET_DV), lambda bi, h, t: (bi, t, h)),
        out_shape=jax.ShapeDtypeStruct((b, s, RET_V_WIDTH), BF16),
        scratch_shapes=[pltpu.VMEM((RET_DK, RET_DV), F32)],
        compiler_params=_params(("parallel", "parallel", "arbitrary")),
        name="retention_main",
    )(lgf, lgb, z, z, z, z, sb, gn)


def _odd_out_router_kernel(y_ref, x_ref, w_ref, ln_ref, wr_ref, xo_ref, h_ref, idx_ref, gate_ref):
    x = x_ref[...] + jnp.dot(y_ref[...], w_ref[...], preferred_element_type=F32)
    xo_ref[...] = x
    hn = _rms(x, ln_ref[...])
    h_ref[...] = hn.astype(BF16)
    logits = lax.dot_general(wr_ref[...], hn, (((1,), (1,)), ((), ())),
                             precision=lax.Precision.HIGHEST, preferred_element_type=F32)
    row = lax.broadcasted_iota(jnp.int32, logits.shape, 0).astype(F32)
    none = float(N_EXPERTS)
    m1 = jnp.max(logits, axis=0, keepdims=True)
    i1 = jnp.min(jnp.where(logits == m1, row, none), axis=0, keepdims=True)
    rest = jnp.where(row == i1, -jnp.inf, logits)
    m2 = jnp.max(rest, axis=0, keepdims=True)
    i2 = jnp.min(jnp.where(rest == m2, row, none), axis=0, keepdims=True)
    e2 = jnp.exp(m2 - m1)
    idx_ref[0:1, :] = i1.astype(jnp.int32)
    idx_ref[1:2, :] = i2.astype(jnp.int32)
    gate_ref[0:1, :] = 1.0 / (1.0 + e2)
    gate_ref[1:2, :] = e2 / (1.0 + e2)


def _odd_out_router(y, x, w_out, ln, wr_t, tm=512):
    n, d = x.shape
    full = lambda shp: pl.BlockSpec(shp, lambda i: (0,) * len(shp))
    return pl.pallas_call(
        _odd_out_router_kernel,
        grid=(n // tm,),
        in_specs=[
            pl.BlockSpec((tm, RET_V_WIDTH), lambda i: (i, 0)),
            pl.BlockSpec((tm, d), lambda i: (i, 0)),
            full((RET_V_WIDTH, d)), full((1, d)), full((N_EXPERTS, d)),
        ],
        out_specs=[
            pl.BlockSpec((tm, d), lambda i: (i, 0)),
            pl.BlockSpec((tm, d), lambda i: (i, 0)),
            pl.BlockSpec((2, tm), lambda i: (0, i)),
            pl.BlockSpec((2, tm), lambda i: (0, i)),
        ],
        out_shape=[
            jax.ShapeDtypeStruct((n, d), F32),
            jax.ShapeDtypeStruct((n, d), BF16),
            jax.ShapeDtypeStruct((2, n), jnp.int32),
            jax.ShapeDtypeStruct((2, n), F32),
        ],
        compiler_params=_params(("parallel",)),
        name="odd_out_proj_router",
    )(y, x, w_out, ln, wr_t)


MOE_TM = 1024
MOE_TF = 512


def _moe_kernel(tile_expert_ref, n_used_ref, xs_ref, gate_ref, w1_ref, w3_ref, w2_ref, o_ref, acc_ref):
    t = pl.program_id(0)
    f = pl.program_id(1)
    last = pl.num_programs(1) - 1
    used = t < n_used_ref[0]

    @pl.when(used)
    def _():
        @pl.when(f == 0)
        def _():
            acc_ref[...] = jnp.zeros_like(acc_ref)

        xs = xs_ref[...]
        g = jnp.dot(xs, w1_ref[...], preferred_element_type=F32)
        u = jnp.dot(xs, w3_ref[...], preferred_element_type=F32)
        act = (_silu(g) * u).astype(BF16)
        acc_ref[...] += jnp.dot(act, w2_ref[...], preferred_element_type=F32)

        @pl.when(f == last)
        def _():
            o_ref[...] = (acc_ref[...] * gate_ref[:, 0:1]).astype(BF16)

    @pl.when(jnp.logical_and(jnp.logical_not(used), f == last))
    def _():
        o_ref[...] = jnp.zeros_like(o_ref)


def _moe_experts(xs, gate_rows, tile_expert, n_used, w1, w3, w2):
    p, d = xs.shape
    n_tiles = p // MOE_TM
    n_f = D_FF_EXPERT // MOE_TF

    def row_tile(t, f, te, nu):
        return (jnp.minimum(t, nu[0] - 1), 0)

    def f_idx(t, f, nu):
        return jnp.where(t < nu[0], f, n_f - 1)

    grid_spec = pltpu.PrefetchScalarGridSpec(
        num_scalar_prefetch=2,
        grid=(n_tiles, n_f),
        in_specs=[
            pl.BlockSpec((MOE_TM, d), row_tile),
            pl.BlockSpec((MOE_TM, LANES), row_tile),
            pl.BlockSpec((None, d, MOE_TF), lambda t, f, te, nu: (te[t], 0, f_idx(t, f, nu))),
            pl.BlockSpec((None, d, MOE_TF), lambda t, f, te, nu: (te[t], 0, f_idx(t, f, nu))),
            pl.BlockSpec((None, MOE_TF, d), lambda t, f, te, nu: (te[t], f_idx(t, f, nu), 0)),
        ],
        out_specs=pl.BlockSpec((MOE_TM, d), lambda t, f, te, nu: (t, 0)),
        scratch_shapes=[pltpu.VMEM((MOE_TM, d), F32)],
    )
    return pl.pallas_call(
        _moe_kernel,
        grid_spec=grid_spec,
        out_shape=jax.ShapeDtypeStruct((p, d), BF16),
        compiler_params=_params(("arbitrary", "arbitrary")),
        name="moe_experts",
    )(tile_expert, n_used, xs, gate_rows, w1, w3, w2)


def _combine_kernel(x_ref, y_ref, o_ref):
    o_ref[...] = x_ref[...] + y_ref[0].astype(F32) + y_ref[1].astype(F32)


def _combine(x, y2, tm=1024):
    n, d = x.shape
    return pl.pallas_call(
        _combine_kernel,
        grid=(n // tm,),
        in_specs=[pl.BlockSpec((tm, d), lambda i: (i, 0)), pl.BlockSpec((2, tm, d), lambda i: (0, i, 0))],
        out_specs=pl.BlockSpec((tm, d), lambda i: (i, 0)),
        out_shape=jax.ShapeDtypeStruct((n, d), F32),
        compiler_params=_params(("parallel",)),
        name="moe_combine",
    )(x, y2)


def _route(idx, gates, n):
    e_flat = idx.reshape(-1)
    n_pairs = e_flat.shape[0]
    order = jnp.argsort(e_flat, stable=True)
    counts = jnp.sum(e_flat[None, :] == jnp.arange(N_EXPERTS, dtype=jnp.int32)[:, None], axis=1).astype(jnp.int32)
    starts = jnp.cumsum(counts) - counts
    tiles_per_expert = (counts + MOE_TM - 1) // MOE_TM
    tile_ends = jnp.cumsum(tiles_per_expert)
    padded_starts = (tile_ends - tiles_per_expert) * MOE_TM
    sorted_e = e_flat[order]
    dest_sorted = padded_starts[sorted_e] + (jnp.arange(n_pairs, dtype=jnp.int32) - starts[sorted_e])
    n_rows = n_pairs + N_EXPERTS * MOE_TM
    n_tiles = n_rows // MOE_TM
    src_token = jnp.zeros((n_rows,), jnp.int32).at[dest_sorted].set((order % n).astype(jnp.int32))
    row_gate = jnp.zeros((n_rows,), F32).at[dest_sorted].set(gates.reshape(-1)[order])
    dest = jnp.zeros((n_pairs,), jnp.int32).at[order].set(dest_sorted)
    tile_expert = jnp.minimum(
        jnp.sum(jnp.arange(n_tiles, dtype=jnp.int32)[:, None] >= tile_ends[None, :], axis=1), N_EXPERTS - 1
    ).astype(jnp.int32)
    n_used = tile_ends[-1:].astype(jnp.int32)
    return src_token, row_gate, dest, tile_expert, n_used


def _channel_dft():
    c = jnp.arange(FNET_GROUP_DIM, dtype=jnp.int32)
    ang = ((c[:, None] * c[None, :]) % FNET_GROUP_DIM).astype(F32) * (2.0 * math.pi / FNET_GROUP_DIM)
    eye = jnp.eye(FNET_GROUPS, dtype=F32)
    scale = FNET_GROUP_DIM ** -0.5
    return jnp.concatenate([jnp.kron(eye, jnp.cos(ang)) * scale, -jnp.kron(eye, jnp.sin(ang)) * scale], axis=1).astype(BF16)


def _sequence_dft(s):
    hi = s // 64
    k = jnp.arange(s, dtype=jnp.int32)
    j_hi = jnp.arange(hi, dtype=jnp.int32)
    j_lo = jnp.arange(64, dtype=jnp.int32)
    w = 2.0 * math.pi / s
    ang_a = (((j_hi[:, None] * k[None, :]) % hi) * 64).astype(F32) * w
    ang_b = ((j_lo[:, None] * k[None, :]) % s).astype(F32) * w
    ca, sa = jnp.cos(ang_a)[:, None, :], jnp.sin(ang_a)[:, None, :]
    cb, sb = jnp.cos(ang_b)[None, :, :], jnp.sin(ang_b)[None, :, :]
    scale = s ** -0.5
    cos_m = ((ca * cb - sa * sb) * scale).reshape(s, s)
    sin_m = ((sa * cb + ca * sb) * scale).reshape(s, s)
    return jnp.concatenate([cos_m, sin_m], axis=1).astype(BF16)


def _trunk(x, p):
    b, s, d = x.shape
    n = b * s
    y, qkv = _even_in(x, p["ln_mix_e"], p["w_in_e"], p["chan_dft"], p["head_sum"], p["qn"], p["kn"])
    fmix = _matmul(_sequence_dft(s), y.reshape(2 * s, b * FNET_WIDTH))
    att = _attention(qkv, p["slopes"])
    x = _even_out(x, fmix, att, p["w_out_e"])
    x = _swiglu(x.reshape(n, d), p["ln_ffn_e"], p["w_gate_e"], p["w_up_e"], p["w_down_e"])
    z = _odd_in(x, p["ln_mix_o"], p["w_in_o"]).reshape(b, s, ODD_IN)
    sb = _ret_bwd_states(z, p["lgb"])
    yr = _ret_main(z, sb, p["lgf"], p["lgb"], p["ret_gn"])
    x, h, idx, gates = _odd_out_router(yr.reshape(n, RET_V_WIDTH), x, p["w_out_o"], p["ln_ffn_o"], p["w_router_t"])
    src_token, row_gate, dest, tile_expert, n_used = _route(idx, gates, n)
    xs = jnp.take(h, src_token, axis=0)
    gate_rows = jnp.broadcast_to(row_gate[:, None], (row_gate.shape[0], LANES))
    ye = _moe_experts(xs, gate_rows, tile_expert, n_used, p["moe_w1"], p["moe_w3"], p["moe_w2"])
    y2 = jnp.take(ye, dest, axis=0).reshape(2, n, d)
    return _combine(x, y2).reshape(b, s, d)


def kernel(x_prompt, x_sample, ln_mix_e, w_in_e, w_out_e, qn_e, kn_e, ln_ffn_e, w_gate_e, w_up_e, w_down_e, ln_mix_o, w_in_o, w_out_o, logdecay_fwd, logdecay_bwd, ret_gn, ln_ffn_o, w_router, moe_w1, moe_w3, moe_w2):
    row = lambda w: w.reshape(1, -1).astype(F32)
    k_scale = jnp.concatenate([
        jnp.ones((RET_QK_WIDTH,), F32), jnp.full((RET_QK_WIDTH,), RET_DK ** -0.5, F32),
        jnp.ones((2 * RET_V_WIDTH,), F32)])
    head_id = jnp.arange(LANES, dtype=jnp.int32) // ATT_HEAD_DIM
    p = {
        "ln_mix_e": row(ln_mix_e[0]),
        "w_in_e": w_in_e[0].astype(BF16),
        "w_out_e": w_out_e[0].astype(BF16),
        "qn": row(jnp.tile(qn_e[0], 2) * (ATT_HEAD_DIM ** -0.5)),
        "kn": row(jnp.tile(kn_e[0], 2)),
        "ln_ffn_e": row(ln_ffn_e[0]),
        "w_gate_e": w_gate_e[0].astype(BF16),
        "w_up_e": w_up_e[0].astype(BF16),
        "w_down_e": w_down_e[0].astype(BF16),
        "ln_mix_o": row(ln_mix_o[0]),
        "w_in_o": (w_in_o[0] * k_scale[None, :]).astype(BF16),
        "w_out_o": w_out_o[0].astype(BF16),
        "lgf": logdecay_fwd[0].astype(F32),
        "lgb": logdecay_bwd[0].astype(F32),
        "ret_gn": row(ret_gn[0]),
        "ln_ffn_o": row(ln_ffn_o[0]),
        "w_router_t": w_router[0].T.astype(F32),
        "moe_w1": moe_w1[0].astype(BF16),
        "moe_w3": moe_w3[0].astype(BF16),
        "moe_w2": moe_w2[0].astype(BF16),
        "chan_dft": _channel_dft(),
        "head_sum": (head_id[:, None] == head_id[None, :]).astype(BF16),
        "slopes": jnp.exp2(-8.0 * (jnp.arange(ATT_HEADS, dtype=F32) + 1.0) / ATT_HEADS),
    }
    return (_trunk(x_prompt, p), _trunk(x_sample, p))
```

```python
import functools
import math

import jax
import jax.numpy as jnp
from jax import lax
from jax.experimental import pallas as pl
from jax.experimental.pallas import tpu as pltpu

D_MODEL = 1024
FNET_GROUPS = 4
FNET_GROUP_DIM = 64
FNET_WIDTH = 256
ATT_HEAD_DIM = 64
ATT_HEADS = 12
ATT_WIDTH = 768
ATT_HALF_WINDOW = 64
DILATIONS = (1, 4, 16)
EVEN_IN = FNET_WIDTH + 3 * ATT_WIDTH
RET_HEADS = 4
RET_DK = 256
RET_DV = 512
RET_QK_WIDTH = 1024
RET_V_WIDTH = 2048
ODD_IN = 6144
D_FF = 2816
N_EXPERTS = 8
D_FF_EXPERT = 3584
EPS = 1e-6
NEG = -1e30

LANES = 128
VMEM_LIMIT_BYTES = 56 * 2**20

F32 = jnp.float32
BF16 = jnp.bfloat16


def _params(semantics):
    return pltpu.CompilerParams(dimension_semantics=semantics, vmem_limit_bytes=VMEM_LIMIT_BYTES)


def _rms(x, w):
    return x * lax.rsqrt(jnp.mean(x * x, axis=-1, keepdims=True) + EPS) * w


def _silu(x):
    return x / (1.0 + jnp.exp(-x))


def _even_in_kernel(x_ref, ln_ref, w_ref, cd_ref, hs_ref, qn_ref, kn_ref, y_ref, qkv_ref):
    h = _rms(x_ref[0], ln_ref[...]).astype(BF16)
    u = jnp.dot(h, w_ref[:, 0:FNET_WIDTH], preferred_element_type=F32).astype(BF16)
    yy = jnp.dot(u, cd_ref[...], preferred_element_type=F32)
    y_ref[0] = yy[:, :FNET_WIDTH].astype(BF16)
    y_ref[1] = yy[:, FNET_WIDTH:].astype(BF16)
    for part, n_ref in ((0, qn_ref), (1, kn_ref)):
        c0 = FNET_WIDTH + part * ATT_WIDTH
        z = jnp.dot(h, w_ref[:, c0:c0 + ATT_WIDTH], preferred_element_type=F32)
        for j in range(ATT_WIDTH // LANES):
            zj = z[:, j * LANES:(j + 1) * LANES]
            ss = jnp.dot((zj * zj).astype(BF16), hs_ref[...], preferred_element_type=F32)
            zn = zj * lax.rsqrt(ss * (1.0 / ATT_HEAD_DIM) + EPS) * n_ref[...]
            qkv_ref[0, :, part * ATT_WIDTH + j * LANES:part * ATT_WIDTH + (j + 1) * LANES] = zn.astype(BF16)
    c0 = FNET_WIDTH + 2 * ATT_WIDTH
    v = jnp.dot(h, w_ref[:, c0:c0 + ATT_WIDTH], preferred_element_type=F32)
    qkv_ref[0, :, 2 * ATT_WIDTH:3 * ATT_WIDTH] = v.astype(BF16)


def _even_in(x, ln, w_in, cd, hs, qn, kn, tm=512):
    b, s, d = x.shape
    full = lambda shp: pl.BlockSpec(shp, lambda bi, i: (0,) * len(shp))
    return pl.pallas_call(
        _even_in_kernel,
        grid=(b, s // tm),
        in_specs=[
            pl.BlockSpec((1, tm, d), lambda bi, i: (bi, i, 0)),
            full((1, d)), full((d, EVEN_IN)), full((FNET_WIDTH, 2 * FNET_WIDTH)), full((LANES, LANES)),
            full((1, LANES)), full((1, LANES)),
        ],
        out_specs=[
            pl.BlockSpec((2, tm, FNET_WIDTH), lambda bi, i: (0, i, bi)),
            pl.BlockSpec((1, tm, 3 * ATT_WIDTH), lambda bi, i: (bi, i, 0)),
        ],
        out_shape=[
            jax.ShapeDtypeStruct((2, s, b * FNET_WIDTH), BF16),
            jax.ShapeDtypeStruct((b, s, 3 * ATT_WIDTH), BF16),
        ],
        compiler_params=_params(("parallel", "parallel")),
        name="even_in_proj",
    )(x, ln, w_in, cd, hs, qn, kn)


def _matmul_kernel(a_ref, b_ref, o_ref, acc_ref):
    @pl.when(pl.program_id(2) == 0)
    def _():
        acc_ref[...] = jnp.zeros_like(acc_ref)

    acc_ref[...] += jnp.dot(a_ref[...], b_ref[...], preferred_element_type=F32)

    @pl.when(pl.program_id(2) == pl.num_programs(2) - 1)
    def _():
        o_ref[...] = acc_ref[...].astype(o_ref.dtype)


def _matmul(a, b, tm=1024, tn=2048, tk=1024):
    m, k = a.shape
    _, n = b.shape
    tm, tn, tk = min(tm, m), min(tn, n), min(tk, k)
    return pl.pallas_call(
        _matmul_kernel,
        grid=(m // tm, n // tn, k // tk),
        in_specs=[pl.BlockSpec((tm, tk), lambda i, j, kk: (i, kk)),
                  pl.BlockSpec((tk, tn), lambda i, j, kk: (kk, j))],
        out_specs=pl.BlockSpec((tm, tn), lambda i, j, kk: (i, j)),
        out_shape=jax.ShapeDtypeStruct((m, n), BF16),
        scratch_shapes=[pltpu.VMEM((tm, tn), F32)],
        compiler_params=_params(("parallel", "parallel", "arbitrary")),
        name="seq_dft_matmul",
    )(a, b)


ATT_QBLK = 128
ATT_CONV_ROWS = 512


def _attn_kernel(slopes_ref, q_ref, k_ref, v_ref, o_ref, xf, qd, kd, vd, bias, m_s, l_s, a_s, *, seq):
    pair = pl.program_id(1)
    lane = lax.broadcasted_iota(jnp.int32, (1, LANES), 1)
    first_head = lane < ATT_HEAD_DIM
    slopes = (slopes_ref[2 * pair], slopes_ref[2 * pair + 1])

    def split_heads(x):
        zero = jnp.zeros_like(x)
        return jnp.where(first_head, x, zero), jnp.where(first_head, zero, x)

    def store_split(dst_ref, rows, x):
        x0, x1 = split_heads(x)
        dst_ref[0, rows, :] = x0
        dst_ref[1, rows, :] = x1

    for pattern, r in enumerate(DILATIONS):
        sub_len = seq // r
        n_keys = min(2 * ATT_QBLK, sub_len)
        blocks_per_class = sub_len // ATT_QBLK

        if r == 1:
            def copy(c, carry):
                rows = pl.ds(pl.multiple_of(c * ATT_CONV_ROWS, ATT_CONV_ROWS), ATT_CONV_ROWS)
                qd[rows, :] = q_ref[0, rows, :]
                store_split(kd, rows, k_ref[0, rows, :])
                store_split(vd, rows, v_ref[0, rows, :])
                return carry

            lax.fori_loop(0, seq // ATT_CONV_ROWS, copy, 0)
        else:
            for src_ref, dst_ref in ((q_ref, qd), (k_ref, kd), (v_ref, vd)):
                def widen(c, carry, src_ref=src_ref):
                    rows = pl.ds(pl.multiple_of(c * ATT_CONV_ROWS, ATT_CONV_ROWS), ATT_CONV_ROWS)
                    xf[rows, :] = src_ref[0, rows, :].astype(F32)
                    return carry

                lax.fori_loop(0, seq // ATT_CONV_ROWS, widen, 0)

                def gather(t, carry, dst_ref=dst_ref, r=r, blocks_per_class=blocks_per_class):
                    rho = t // blocks_per_class
                    l0 = (t % blocks_per_class) * ATT_QBLK
                    x = xf[pl.ds(rho + r * l0, ATT_QBLK, stride=r), :].astype(BF16)
                    rows = pl.ds(pl.multiple_of(t * ATT_QBLK, ATT_QBLK), ATT_QBLK)
                    if dst_ref is qd:
                        qd[rows, :] = x
                    else:
                        store_split(dst_ref, rows, x)
                    return carry

                lax.fori_loop(0, seq // ATT_QBLK, gather, 0)

        qi = lax.broadcasted_iota(jnp.int32, (ATT_QBLK, n_keys), 0)
        kj = lax.broadcasted_iota(jnp.int32, (ATT_QBLK, n_keys), 1)
        offsets = (0, ATT_HALF_WINDOW, 2 * ATT_HALF_WINDOW) if blocks_per_class > 1 else (0,)
        for variant, off in enumerate(offsets):
            dist = jnp.abs(kj - off - qi)
            for h in range(2):
                scaled = (-slopes[h] * float(r)) * dist.astype(F32)
                bias[h, variant, :, 0:n_keys] = jnp.where(dist <= ATT_HALF_WINDOW, scaled, NEG)

        def block(t, carry, r=r, n_keys=n_keys, blocks_per_class=blocks_per_class, pattern=pattern):
            rho = t // blocks_per_class
            lb = t % blocks_per_class
            if blocks_per_class > 1:
                variant = jnp.where(lb == 0, 0, jnp.where(lb == blocks_per_class - 1, 2, 1))
            else:
                variant = 0
            q_base = pl.multiple_of(t * ATT_QBLK, ATT_QBLK)
            k_base = pl.multiple_of(q_base - variant * ATT_HALF_WINDOW, ATT_HALF_WINDOW)
            q = qd[pl.ds(q_base, ATT_QBLK), :]
            k_rows = pl.ds(k_base, n_keys)
            probs = []
            stats = []
            for h in range(2):
                sc = lax.dot_general(q, kd[h, k_rows, :], (((1,), (1,)), ((), ())), preferred_element_type=F32)
                sc = sc + bias[h, variant, :, 0:n_keys]
                m = jnp.max(sc, axis=-1, keepdims=True)
                p = jnp.exp(sc - m)
                stats.append((m, jnp.sum(p, axis=-1, keepdims=True)))
                probs.append(p.astype(BF16))
            a_new = (jnp.dot(probs[0], vd[0, k_rows, :], preferred_element_type=F32)
                     + jnp.dot(probs[1], vd[1, k_rows, :], preferred_element_type=F32))
            m_new = jnp.where(first_head, stats[0][0], stats[1][0])
            l_new = jnp.where(first_head, stats[0][1], stats[1][1])
            if r == 1:
                s_rows = pl.ds(q_base, ATT_QBLK)
            else:
                s_rows = pl.ds(rho + r * (lb * ATT_QBLK), ATT_QBLK, stride=r)
            if pattern == 0:
                m_s[s_rows, :] = m_new
                l_s[s_rows, :] = l_new
                a_s[s_rows, :] = a_new
            else:
                m_old = m_s[s_rows, :]
                m_tot = jnp.maximum(m_old, m_new)
                w_old = jnp.exp(m_old - m_tot)
                w_new = jnp.exp(m_new - m_tot)
                m_s[s_rows, :] = m_tot
                l_s[s_rows, :] = w_old * l_s[s_rows, :] + w_new * l_new
                a_s[s_rows, :] = w_old * a_s[s_rows, :] + w_new * a_new
            return carry

        lax.fori_loop(0, seq // ATT_QBLK, block, 0, unroll=2)

    def finish(c, carry):
        rows = pl.ds(pl.multiple_of(c * ATT_CONV_ROWS, ATT_CONV_ROWS), ATT_CONV_ROWS)
        o_ref[0, rows, :] = (a_s[rows, :] / l_s[rows, :]).astype(BF16)
        return carry

    lax.fori_loop(0, seq // ATT_CONV_ROWS, finish, 0)


def _attention(qkv, slopes):
    b, s, _ = qkv.shape
    n_pairs = ATT_WIDTH // LANES
    blk = lambda off: pl.BlockSpec((1, s, LANES), lambda bi, j, off=off: (bi, 0, off + j))
    return pl.pallas_call(
        functools.partial(_attn_kernel, seq=s),
        grid=(b, n_pairs),
        in_specs=[pl.BlockSpec(memory_space=pltpu.SMEM), blk(0), blk(n_pairs), blk(2 * n_pairs)],
        out_specs=pl.BlockSpec((1, s, LANES), lambda bi, j: (bi, 0, j)),
        out_shape=jax.ShapeDtypeStruct((b, s, ATT_WIDTH), BF16),
        scratch_shapes=[
            pltpu.VMEM((s, LANES), F32),
            pltpu.VMEM((s, LANES), BF16),
            pltpu.VMEM((2, s, LANES), BF16),
            pltpu.VMEM((2, s, LANES), BF16),
            pltpu.VMEM((2, 3, ATT_QBLK, 2 * ATT_QBLK), F32),
            pltpu.VMEM((s, LANES), F32),
            pltpu.VMEM((s, LANES), F32),
            pltpu.VMEM((s, LANES), F32),
        ],
        compiler_params=_params(("parallel", "parallel")),
        name="dilated_attention",
    )(slopes, qkv, qkv, qkv)


def _even_out_kernel(x_ref, f_ref, a_ref, w_ref, o_ref):
    acc = x_ref[0]
    acc = acc + jnp.dot(f_ref[...], w_ref[0:FNET_WIDTH, :], preferred_element_type=F32)
    acc = acc + jnp.dot(a_ref[0], w_ref[FNET_WIDTH:, :], preferred_element_type=F32)
    o_ref[0] = acc


def _even_out(x, fmix, att, w_out, tm=512):
    b, s, d = x.shape
    return pl.pallas_call(
        _even_out_kernel,
        grid=(b, s // tm),
        in_specs=[
            pl.BlockSpec((1, tm, d), lambda bi, i: (bi, i, 0)),
            pl.BlockSpec((tm, FNET_WIDTH), lambda bi, i: (i, bi)),
            pl.BlockSpec((1, tm, ATT_WIDTH), lambda bi, i: (bi, i, 0)),
            pl.BlockSpec((d, d), lambda bi, i: (0, 0)),
        ],
        out_specs=pl.BlockSpec((1, tm, d), lambda bi, i: (bi, i, 0)),
        out_shape=jax.ShapeDtypeStruct((b, s, d), F32),
        compiler_params=_params(("parallel", "parallel")),
        name="even_out_proj",
    )(x, fmix, att, w_out)


def _swiglu_kernel(x_ref, ln_ref, wg_ref, wu_ref, wd_ref, o_ref, h_ref, acc_ref):
    f = pl.program_id(1)

    @pl.when(f == 0)
    def _():
        x = x_ref[...]
        h_ref[...] = _rms(x, ln_ref[...]).astype(BF16)
        acc_ref[...] = x

    h = h_ref[...]
    g = jnp.dot(h, wg_ref[...], preferred_element_type=F32)
    u = jnp.dot(h, wu_ref[...], preferred_element_type=F32)
    act = (_silu(g) * u).astype(BF16)
    acc_ref[...] += jnp.dot(act, wd_ref[...], preferred_element_type=F32)

    @pl.when(f == pl.num_programs(1) - 1)
    def _():
        o_ref[...] = acc_ref[...]


def _swiglu(x, ln, wg, wu, wd, tm=512, tf=1408):
    n, d = x.shape
    dff = wg.shape[1]
    return pl.pallas_call(
        _swiglu_kernel,
        grid=(n // tm, dff // tf),
        in_specs=[
            pl.BlockSpec((tm, d), lambda i, f: (i, 0)),
            pl.BlockSpec((1, d), lambda i, f: (0, 0)),
            pl.BlockSpec((d, tf), lambda i, f: (0, f)),
            pl.BlockSpec((d, tf), lambda i, f: (0, f)),
            pl.BlockSpec((tf, d), lambda i, f: (f, 0)),
        ],
        out_specs=pl.BlockSpec((tm, d), lambda i, f: (i, 0)),
        out_shape=jax.ShapeDtypeStruct((n, d), F32),
        scratch_shapes=[pltpu.VMEM((tm, d), BF16), pltpu.VMEM((tm, d), F32)],
        compiler_params=_params(("parallel", "arbitrary")),
        name="swiglu_ffn",
    )(x, ln, wg, wu, wd)


def _odd_in_kernel(x_ref, ln_ref, w_ref, o_ref, h_ref):
    @pl.when(pl.program_id(1) == 0)
    def _():
        h_ref[...] = _rms(x_ref[...], ln_ref[...]).astype(BF16)

    o_ref[...] = jnp.dot(h_ref[...], w_ref[...], preferred_element_type=F32).astype(BF16)


def _odd_in(x, ln, w_in, tm=1024, tn=2048):
    n, d = x.shape
    return pl.pallas_call(
        _odd_in_kernel,
        grid=(n // tm, ODD_IN // tn),
        in_specs=[
            pl.BlockSpec((tm, d), lambda i, j: (i, 0)),
            pl.BlockSpec((1, d), lambda i, j: (0, 0)),
            pl.BlockSpec((d, tn), lambda i, j: (0, j)),
        ],
        out_specs=pl.BlockSpec((tm, tn), lambda i, j: (i, j)),
        out_shape=jax.ShapeDtypeStruct((n, ODD_IN), BF16),
        scratch_shapes=[pltpu.VMEM((tm, d), BF16)],
        compiler_params=_params(("parallel", "arbitrary")),
        name="odd_in_proj",
    )(x, ln, w_in)


RET_CHUNK = 256
RET_TILE = 1024


def _chunk_positions():
    return lax.broadcasted_iota(jnp.int32, (RET_CHUNK, 1), 0).astype(F32)


def _kt_v(k_scaled, v):
    return lax.dot_general(k_scaled, v, (((0,), (0,)), ((), ())), preferred_element_type=F32)


def _ret_bwd_state_kernel(lgb_ref, k_ref, v_ref, sb_ref, state):
    head = pl.program_id(1)

    @pl.when(pl.program_id(2) == 0)
    def _():
        state[...] = jnp.zeros_like(state)

    lg = lgb_ref[head]
    key_decay = jnp.exp(lg * _chunk_positions())
    chunk_decay = jnp.exp(jnp.full((1, RET_DV), lg * RET_CHUNK, F32))
    for c in reversed(range(RET_TILE // RET_CHUNK)):
        rows = slice(c * RET_CHUNK, (c + 1) * RET_CHUNK)
        sb_ref[0, 0, c] = state[...].astype(BF16)
        kd = (k_ref[0, rows, :].astype(F32) * key_decay).astype(BF16)
        state[...] = state[...] * chunk_decay + _kt_v(kd, v_ref[0, rows, :])


def _ret_bwd_states(z, lgb):
    b, s, _ = z.shape
    n = s // RET_CHUNK
    n_tiles = s // RET_TILE
    per_tile = RET_TILE // RET_CHUNK
    k_off = RET_QK_WIDTH // RET_DK
    v_off = 2 * RET_QK_WIDTH // RET_DV
    return pl.pallas_call(
        _ret_bwd_state_kernel,
        grid=(b, RET_HEADS, n_tiles),
        in_specs=[
            pl.BlockSpec(memory_space=pltpu.SMEM),
            pl.BlockSpec((1, RET_TILE, RET_DK), lambda bi, h, t: (bi, n_tiles - 1 - t, k_off + h)),
            pl.BlockSpec((1, RET_TILE, RET_DV), lambda bi, h, t: (bi, n_tiles - 1 - t, v_off + h)),
        ],
        out_specs=pl.BlockSpec((1, 1, per_tile, RET_DK, RET_DV), lambda bi, h, t: (bi, h, n_tiles - 1 - t, 0, 0)),
        out_shape=jax.ShapeDtypeStruct((b, RET_HEADS, n, RET_DK, RET_DV), BF16),
        scratch_shapes=[pltpu.VMEM((RET_DK, RET_DV), F32)],
        compiler_params=_params(("parallel", "parallel", "arbitrary")),
        name="retention_bwd_states",
    )(lgb, z, z)


def _ret_main_kernel(lgf_ref, lgb_ref, q_ref, k_ref, v_ref, g_ref, sb_ref, gn_ref, o_ref, state):
    head = pl.program_id(1)

    @pl.when(pl.program_id(2) == 0)
    def _():
        state[...] = jnp.zeros_like(state)

    lgf = lgf_ref[head]
    lgb = lgb_ref[head]
    c = RET_CHUNK
    pos = _chunk_positions()
    rel = (lax.broadcasted_iota(jnp.int32, (c, c), 0) - lax.broadcasted_iota(jnp.int32, (c, c), 1)).astype(F32)
    decay = jnp.exp(jnp.where(rel >= 0, lgf * rel, -lgb * rel))
    q_decay_f = jnp.exp(lgf * (pos + 1.0))
    q_decay_b = jnp.exp(lgb * (c - pos))
    k_decay_f = jnp.exp(lgf * (c - 1.0 - pos))
    chunk_decay = jnp.exp(jnp.full((1, RET_DV), lgf * c, F32))
    gn = gn_ref[...]

    for ci in range(RET_TILE // RET_CHUNK):
        rows = slice(ci * c, (ci + 1) * c)
        q = q_ref[0, rows, :]
        k = k_ref[0, rows, :]
        v = v_ref[0, rows, :]
        qf32 = q.astype(F32)
        sc = lax.dot_general(q, k, (((1,), (1,)), ((), ())), preferred_element_type=F32)
        y = jnp.dot((sc * decay).astype(BF16), v, preferred_element_type=F32)
        y = y + jnp.dot((qf32 * q_decay_f).astype(BF16), state[...].astype(BF16), preferred_element_type=F32)
        y = y + jnp.dot((qf32 * q_decay_b).astype(BF16), sb_ref[0, 0, ci], preferred_element_type=F32)
        kd = (k.astype(F32) * k_decay_f).astype(BF16)
        state[...] = state[...] * chunk_decay + _kt_v(kd, v)
        yn = _rms(y, gn)
        o_ref[0, rows, :] = (_silu(g_ref[0, rows, :].astype(F32)) * yn).astype(BF16)


def _ret_main(z, sb, lgf, lgb, gn):
    b, s, _ = z.shape
    n_tiles = s // RET_TILE
    per_tile = RET_TILE // RET_CHUNK
    k_off = RET_QK_WIDTH // RET_DK
    v_off = 2 * RET_QK_WIDTH // RET_DV
    g_off = (2 * RET_QK_WIDTH + RET_V_WIDTH) // RET_DV
    smem = pl.BlockSpec(memory_space=pltpu.SMEM)
    return pl.pallas_call(
        _ret_main_kernel,
        grid=(b, RET_HEADS, n_tiles),
        in_specs=[
            smem, smem,
            pl.BlockSpec((1, RET_TILE, RET_DK), lambda bi, h, t: (bi, t, h)),
            pl.BlockSpec((1, RET_TILE, RET_DK), lambda bi, h, t: (bi, t, k_off + h)),
            pl.BlockSpec((1, RET_TILE, RET_DV), lambda bi, h, t: (bi, t, v_off + h)),
            pl.BlockSpec((1, RET_TILE, RET_DV), lambda bi, h, t: (bi, t, g_off + h)),
            pl.BlockSpec((1, 1, per_tile, RET_DK, RET_DV), lambda bi, h, t: (bi, h, t, 0, 0)),
            pl.BlockSpec((1, RET_DV), lambda bi, h, t: (0, h)),
        ],
        out_specs=pl.BlockSpec((1, RET_TILE, RET_DV), lambda bi, h, t: (bi, t, h)),
        out_shape=jax.ShapeDtypeStruct((b, s, RET_V_WIDTH), BF16),
        scratch_shapes=[pltpu.VMEM((RET_DK, RET_DV), F32)],
        compiler_params=_params(("parallel", "parallel", "arbitrary")),
        name="retention_main",
    )(lgf, lgb, z, z, z, z, sb, gn)


def _odd_out_router_kernel(y_ref, x_ref, w_ref, ln_ref, wr_ref, xo_ref, h_ref, idx_ref, gate_ref):
    x = x_ref[...] + jnp.dot(y_ref[...], w_ref[...], preferred_element_type=F32)
    xo_ref[...] = x
    hn = _rms(x, ln_ref[...])
    h_ref[...] = hn.astype(BF16)
    logits = lax.dot_general(wr_ref[...], hn, (((1,), (1,)), ((), ())),
                             precision=lax.Precision.HIGHEST, preferred_element_type=F32)
    row = lax.broadcasted_iota(jnp.int32, logits.shape, 0).astype(F32)
    none = float(N_EXPERTS)
    m1 = jnp.max(logits, axis=0, keepdims=True)
    i1 = jnp.min(jnp.where(logits == m1, row, none), axis=0, keepdims=True)
    rest = jnp.where(row == i1, -jnp.inf, logits)
    m2 = jnp.max(rest, axis=0, keepdims=True)
    i2 = jnp.min(jnp.where(rest == m2, row, none), axis=0, keepdims=True)
    e2 = jnp.exp(m2 - m1)
    idx_ref[0:1, :] = i1.astype(jnp.int32)
    idx_ref[1:2, :] = i2.astype(jnp.int32)
    gate_ref[0:1, :] = 1.0 / (1.0 + e2)
    gate_ref[1:2, :] = e2 / (1.0 + e2)


def _odd_out_router(y, x, w_out, ln, wr_t, tm=512):
    n, d = x.shape
    full = lambda shp: pl.BlockSpec(shp, lambda i: (0,) * len(shp))
    return pl.pallas_call(
        _odd_out_router_kernel,
        grid=(n // tm,),
        in_specs=[
            pl.BlockSpec((tm, RET_V_WIDTH), lambda i: (i, 0)),
            pl.BlockSpec((tm, d), lambda i: (i, 0)),
            full((RET_V_WIDTH, d)), full((1, d)), full((N_EXPERTS, d)),
        ],
        out_specs=[
            pl.BlockSpec((tm, d), lambda i: (i, 0)),
            pl.BlockSpec((tm, d), lambda i: (i, 0)),
            pl.BlockSpec((2, tm), lambda i: (0, i)),
            pl.BlockSpec((2, tm), lambda i: (0, i)),
        ],
        out_shape=[
            jax.ShapeDtypeStruct((n, d), F32),
            jax.ShapeDtypeStruct((n, d), BF16),
            jax.ShapeDtypeStruct((2, n), jnp.int32),
            jax.ShapeDtypeStruct((2, n), F32),
        ],
        compiler_params=_params(("parallel",)),
        name="odd_out_proj_router",
    )(y, x, w_out, ln, wr_t)


MOE_TM = 1024
MOE_TF = 512


def _moe_kernel(tile_expert_ref, n_used_ref, xs_ref, gate_ref, w1_ref, w3_ref, w2_ref, o_ref, acc_ref):
    t = pl.program_id(0)
    f = pl.program_id(1)
    last = pl.num_programs(1) - 1
    used = t < n_used_ref[0]

    @pl.when(used)
    def _():
        @pl.when(f == 0)
        def _():
            acc_ref[...] = jnp.zeros_like(acc_ref)

        xs = xs_ref[...]
        g = jnp.dot(xs, w1_ref[...], preferred_element_type=F32)
        u = jnp.dot(xs, w3_ref[...], preferred_element_type=F32)
        act = (_silu(g) * u).astype(BF16)
        acc_ref[...] += jnp.dot(act, w2_ref[...], preferred_element_type=F32)

        @pl.when(f == last)
        def _():
            o_ref[...] = (acc_ref[...] * gate_ref[:, 0:1]).astype(BF16)

    @pl.when(jnp.logical_and(jnp.logical_not(used), f == last))
    def _():
        o_ref[...] = jnp.zeros_like(o_ref)


def _moe_experts(xs, gate_rows, tile_expert, n_used, w1, w3, w2):
    p, d = xs.shape
    n_tiles = p // MOE_TM
    n_f = D_FF_EXPERT // MOE_TF

    def row_tile(t, f, te, nu):
        return (jnp.minimum(t, nu[0] - 1), 0)

    def f_idx(t, f, nu):
        return jnp.where(t < nu[0], f, n_f - 1)

    grid_spec = pltpu.PrefetchScalarGridSpec(
        num_scalar_prefetch=2,
        grid=(n_tiles, n_f),
        in_specs=[
            pl.BlockSpec((MOE_TM, d), row_tile),
            pl.BlockSpec((MOE_TM, LANES), row_tile),
            pl.BlockSpec((None, d, MOE_TF), lambda t, f, te, nu: (te[t], 0, f_idx(t, f, nu))),
            pl.BlockSpec((None, d, MOE_TF), lambda t, f, te, nu: (te[t], 0, f_idx(t, f, nu))),
            pl.BlockSpec((None, MOE_TF, d), lambda t, f, te, nu: (te[t], f_idx(t, f, nu), 0)),
        ],
        out_specs=pl.BlockSpec((MOE_TM, d), lambda t, f, te, nu: (t, 0)),
        scratch_shapes=[pltpu.VMEM((MOE_TM, d), F32)],
    )
    return pl.pallas_call(
        _moe_kernel,
        grid_spec=grid_spec,
        out_shape=jax.ShapeDtypeStruct((p, d), BF16),
        compiler_params=_params(("arbitrary", "arbitrary")),
        name="moe_experts",
    )(tile_expert, n_used, xs, gate_rows, w1, w3, w2)


def _combine_kernel(x_ref, y_ref, o_ref):
    o_ref[...] = x_ref[...] + y_ref[0].astype(F32) + y_ref[1].astype(F32)


def _combine(x, y2, tm=1024):
    n, d = x.shape
    return pl.pallas_call(
        _combine_kernel,
        grid=(n // tm,),
        in_specs=[pl.BlockSpec((tm, d), lambda i: (i, 0)), pl.BlockSpec((2, tm, d), lambda i: (0, i, 0))],
        out_specs=pl.BlockSpec((tm, d), lambda i: (i, 0)),
        out_shape=jax.ShapeDtypeStruct((n, d), F32),
        compiler_params=_params(("parallel",)),
        name="moe_combine",
    )(x, y2)


def _route(idx, gates, n):
    e_flat = idx.reshape(-1)
    n_pairs = e_flat.shape[0]
    pair_id = jnp.arange(n_pairs, dtype=jnp.int32)
    order = (jnp.sort(e_flat * n_pairs + pair_id) % n_pairs).astype(jnp.int32)
    rank = jnp.argsort(order).astype(jnp.int32)
    counts = jnp.sum(e_flat[None, :] == jnp.arange(N_EXPERTS, dtype=jnp.int32)[:, None], axis=1).astype(jnp.int32)
    starts = jnp.cumsum(counts) - counts
    tiles_per_expert = (counts + MOE_TM - 1) // MOE_TM
    tile_ends = jnp.cumsum(tiles_per_expert)
    padded_starts = (tile_ends - tiles_per_expert) * MOE_TM
    dest = rank + (padded_starts - starts)[e_flat]
    n_rows = n_pairs + N_EXPERTS * MOE_TM
    n_tiles = n_rows // MOE_TM
    tile_expert = jnp.minimum(
        jnp.sum(jnp.arange(n_tiles, dtype=jnp.int32)[:, None] >= tile_ends[None, :], axis=1), N_EXPERTS - 1
    ).astype(jnp.int32)
    row_expert = jnp.repeat(tile_expert, MOE_TM)
    row_in_expert = jnp.arange(n_rows, dtype=jnp.int32) - padded_starts[row_expert]
    row_valid = row_in_expert < counts[row_expert]
    row_pair = order[jnp.clip(starts[row_expert] + row_in_expert, 0, n_pairs - 1)]
    src_token = jnp.where(row_valid, row_pair % n, 0).astype(jnp.int32)
    row_gate = jnp.where(row_valid, gates.reshape(-1)[row_pair], 0.0)
    n_used = tile_ends[-1:].astype(jnp.int32)
    return src_token, row_gate, dest, tile_expert, n_used


def _channel_dft():
    c = jnp.arange(FNET_GROUP_DIM, dtype=jnp.int32)
    ang = ((c[:, None] * c[None, :]) % FNET_GROUP_DIM).astype(F32) * (2.0 * math.pi / FNET_GROUP_DIM)
    eye = jnp.eye(FNET_GROUPS, dtype=F32)
    scale = FNET_GROUP_DIM ** -0.5
    return jnp.concatenate([jnp.kron(eye, jnp.cos(ang)) * scale, -jnp.kron(eye, jnp.sin(ang)) * scale], axis=1).astype(BF16)


def _sequence_dft(s):
    hi = s // 64
    k = jnp.arange(s, dtype=jnp.int32)
    j_hi = jnp.arange(hi, dtype=jnp.int32)
    j_lo = jnp.arange(64, dtype=jnp.int32)
    w = 2.0 * math.pi / s
    ang_a = (((j_hi[:, None] * k[None, :]) % hi) * 64).astype(F32) * w
    ang_b = ((j_lo[:, None] * k[None, :]) % s).astype(F32) * w
    ca, sa = jnp.cos(ang_a)[:, None, :], jnp.sin(ang_a)[:, None, :]
    cb, sb = jnp.cos(ang_b)[None, :, :], jnp.sin(ang_b)[None, :, :]
    scale = s ** -0.5
    cos_m = ((ca * cb - sa * sb) * scale).reshape(s, s)
    sin_m = ((sa * cb + ca * sb) * scale).reshape(s, s)
    return jnp.concatenate([cos_m, sin_m], axis=1).astype(BF16)


def _trunk(x, p):
    b, s, d = x.shape
    n = b * s
    y, qkv = _even_in(x, p["ln_mix_e"], p["w_in_e"], p["chan_dft"], p["head_sum"], p["qn"], p["kn"])
    fmix = _matmul(_sequence_dft(s), y.reshape(2 * s, b * FNET_WIDTH))
    att = _attention(qkv, p["slopes"])
    x = _even_out(x, fmix, att, p["w_out_e"])
    x = _swiglu(x.reshape(n, d), p["ln_ffn_e"], p["w_gate_e"], p["w_up_e"], p["w_down_e"])
    z = _odd_in(x, p["ln_mix_o"], p["w_in_o"]).reshape(b, s, ODD_IN)
    sb = _ret_bwd_states(z, p["lgb"])
    yr = _ret_main(z, sb, p["lgf"], p["lgb"], p["ret_gn"])
    x, h, idx, gates = _odd_out_router(yr.reshape(n, RET_V_WIDTH), x, p["w_out_o"], p["ln_ffn_o"], p["w_router_t"])
    src_token, row_gate, dest, tile_expert, n_used = _route(idx, gates, n)
    xs = jnp.take(h, src_token, axis=0)
    gate_rows = jnp.broadcast_to(row_gate[:, None], (row_gate.shape[0], LANES))
    ye = _moe_experts(xs, gate_rows, tile_expert, n_used, p["moe_w1"], p["moe_w3"], p["moe_w2"])
    y2 = jnp.take(ye, dest, axis=0).reshape(2, n, d)
    return _combine(x, y2).reshape(b, s, d)


def kernel(x_prompt, x_sample, ln_mix_e, w_in_e, w_out_e, qn_e, kn_e, ln_ffn_e, w_gate_e, w_up_e, w_down_e, ln_mix_o, w_in_o, w_out_o, logdecay_fwd, logdecay_bwd, ret_gn, ln_ffn_o, w_router, moe_w1, moe_w3, moe_w2):
    row = lambda w: w.reshape(1, -1).astype(F32)
    k_scale = jnp.concatenate([
        jnp.ones((RET_QK_WIDTH,), F32), jnp.full((RET_QK_WIDTH,), RET_DK ** -0.5, F32),
        jnp.ones((2 * RET_V_WIDTH,), F32)])
    head_id = jnp.arange(LANES, dtype=jnp.int32) // ATT_HEAD_DIM
    p = {
        "ln_mix_e": row(ln_mix_e[0]),
        "w_in_e": w_in_e[0].astype(BF16),
        "w_out_e": w_out_e[0].astype(BF16),
        "qn": row(jnp.tile(qn_e[0], 2) * (ATT_HEAD_DIM ** -0.5)),
        "kn": row(jnp.tile(kn_e[0], 2)),
        "ln_ffn_e": row(ln_ffn_e[0]),
        "w_gate_e": w_gate_e[0].astype(BF16),
        "w_up_e": w_up_e[0].astype(BF16),
        "w_down_e": w_down_e[0].astype(BF16),
        "ln_mix_o": row(ln_mix_o[0]),
        "w_in_o": (w_in_o[0] * k_scale[None, :]).astype(BF16),
        "w_out_o": w_out_o[0].astype(BF16),
        "lgf": logdecay_fwd[0].astype(F32),
        "lgb": logdecay_bwd[0].astype(F32),
        "ret_gn": row(ret_gn[0]),
        "ln_ffn_o": row(ln_ffn_o[0]),
        "w_router_t": w_router[0].T.astype(F32),
        "moe_w1": moe_w1[0].astype(BF16),
        "moe_w3": moe_w3[0].astype(BF16),
        "moe_w2": moe_w2[0].astype(BF16),
        "chan_dft": _channel_dft(),
        "head_sum": (head_id[:, None] == head_id[None, :]).astype(BF16),
        "slopes": jnp.exp2(-8.0 * (jnp.arange(ATT_HEADS, dtype=F32) + 1.0) / ATT_HEADS),
    }
    return (_trunk(x_prompt, p), _trunk(x_sample, p))
```

```python
import functools
import math

import jax
import jax.numpy as jnp
from jax import lax
from jax.experimental import pallas as pl
from jax.experimental.pallas import tpu as pltpu

D_MODEL = 1024
FNET_GROUPS = 4
FNET_GROUP_DIM = 64
FNET_WIDTH = 256
ATT_HEAD_DIM = 64
ATT_HEADS = 12
ATT_WIDTH = 768
ATT_HALF_WINDOW = 64
DILATIONS = (1, 4, 16)
EVEN_IN = FNET_WIDTH + 3 * ATT_WIDTH
RET_HEADS = 4
RET_DK = 256
RET_DV = 512
RET_QK_WIDTH = 1024
RET_V_WIDTH = 2048
ODD_IN = 6144
D_FF = 2816
N_EXPERTS = 8
D_FF_EXPERT = 3584
EPS = 1e-6
NEG = -1e30

LANES = 128
VMEM_LIMIT_BYTES = 56 * 2**20

F32 = jnp.float32
BF16 = jnp.bfloat16


def _params(semantics):
    return pltpu.CompilerParams(dimension_semantics=semantics, vmem_limit_bytes=VMEM_LIMIT_BYTES)


def _rms(x, w):
    return x * lax.rsqrt(jnp.mean(x * x, axis=-1, keepdims=True) + EPS) * w


def _silu(x):
    return x / (1.0 + jnp.exp(-x))


def _even_in_kernel(x_ref, ln_ref, w_ref, cd_ref, hs_ref, qn_ref, kn_ref, y_ref, qkv_ref):
    h = _rms(x_ref[0], ln_ref[...]).astype(BF16)
    u = jnp.dot(h, w_ref[:, 0:FNET_WIDTH], preferred_element_type=F32).astype(BF16)
    yy = jnp.dot(u, cd_ref[...], preferred_element_type=F32)
    y_ref[0] = yy[:, :FNET_WIDTH].astype(BF16)
    y_ref[1] = yy[:, FNET_WIDTH:].astype(BF16)
    for part, n_ref in ((0, qn_ref), (1, kn_ref)):
        c0 = FNET_WIDTH + part * ATT_WIDTH
        z = jnp.dot(h, w_ref[:, c0:c0 + ATT_WIDTH], preferred_element_type=F32)
        for j in range(ATT_WIDTH // LANES):
            zj = z[:, j * LANES:(j + 1) * LANES]
            ss = jnp.dot((zj * zj).astype(BF16), hs_ref[...], preferred_element_type=F32)
            zn = zj * lax.rsqrt(ss * (1.0 / ATT_HEAD_DIM) + EPS) * n_ref[...]
            qkv_ref[0, :, part * ATT_WIDTH + j * LANES:part * ATT_WIDTH + (j + 1) * LANES] = zn.astype(BF16)
    c0 = FNET_WIDTH + 2 * ATT_WIDTH
    v = jnp.dot(h, w_ref[:, c0:c0 + ATT_WIDTH], preferred_element_type=F32)
    qkv_ref[0, :, 2 * ATT_WIDTH:3 * ATT_WIDTH] = v.astype(BF16)


def _even_in(x, ln, w_in, cd, hs, qn, kn, tm=512):
    b, s, d = x.shape
    full = lambda shp: pl.BlockSpec(shp, lambda bi, i: (0,) * len(shp))
    return pl.pallas_call(
        _even_in_kernel,
        grid=(b, s // tm),
        in_specs=[
            pl.BlockSpec((1, tm, d), lambda bi, i: (bi, i, 0)),
            full((1, d)), full((d, EVEN_IN)), full((FNET_WIDTH, 2 * FNET_WIDTH)), full((LANES, LANES)),
            full((1, LANES)), full((1, LANES)),
        ],
        out_specs=[
            pl.BlockSpec((2, tm, FNET_WIDTH), lambda bi, i: (0, i, bi)),
            pl.BlockSpec((1, tm, 3 * ATT_WIDTH), lambda bi, i: (bi, i, 0)),
        ],
        out_shape=[
            jax.ShapeDtypeStruct((2, s, b * FNET_WIDTH), BF16),
            jax.ShapeDtypeStruct((b, s, 3 * ATT_WIDTH), BF16),
        ],
        compiler_params=_params(("parallel", "parallel")),
        name="even_in_proj",
    )(x, ln, w_in, cd, hs, qn, kn)


def _matmul_kernel(a_ref, b_ref, o_ref, acc_ref):
    @pl.when(pl.program_id(2) == 0)
    def _():
        acc_ref[...] = jnp.zeros_like(acc_ref)

    acc_ref[...] += jnp.dot(a_ref[...], b_ref[...], preferred_element_type=F32)

    @pl.when(pl.program_id(2) == pl.num_programs(2) - 1)
    def _():
        o_ref[...] = acc_ref[...].astype(o_ref.dtype)


def _matmul(a, b, tm=1024, tn=2048, tk=1024):
    m, k = a.shape
    _, n = b.shape
    tm, tn, tk = min(tm, m), min(tn, n), min(tk, k)
    return pl.pallas_call(
        _matmul_kernel,
        grid=(m // tm, n // tn, k // tk),
        in_specs=[pl.BlockSpec((tm, tk), lambda i, j, kk: (i, kk)),
                  pl.BlockSpec((tk, tn), lambda i, j, kk: (kk, j))],
        out_specs=pl.BlockSpec((tm, tn), lambda i, j, kk: (i, j)),
        out_shape=jax.ShapeDtypeStruct((m, n), BF16),
        scratch_shapes=[pltpu.VMEM((tm, tn), F32)],
        compiler_params=_params(("parallel", "parallel", "arbitrary")),
        name="seq_dft_matmul",
    )(a, b)


ATT_QBLK = 128
ATT_CONV_ROWS = 512
ATT_GROUP = 8
ATT_MAX_DILATION = 16


def _attn_kernel(slopes_ref, q_ref, k_ref, v_ref, o_ref, xf, qs, kd, vd, bias, sc_buf, p_buf, m_buf, m_s, l_s, a_s,
                 *, seq):
    pair = pl.program_id(1)
    lane = lax.broadcasted_iota(jnp.int32, (1, LANES), 1)
    first_head = lane < ATT_HEAD_DIM
    slopes = (slopes_ref[2 * pair], slopes_ref[2 * pair + 1])
    base_len = seq // ATT_MAX_DILATION

    def conv_rows(c):
        return pl.ds(pl.multiple_of(c * ATT_CONV_ROWS, ATT_CONV_ROWS), ATT_CONV_ROWS)

    def widen(src_ref):
        def body(c, carry):
            xf[conv_rows(c), :] = src_ref[0, conv_rows(c), :].astype(F32)
            return carry

        lax.fori_loop(0, seq // ATT_CONV_ROWS, body, 0)

    def store_split(dst_ref, rows, x, fill):
        other = jnp.full_like(x, fill)
        dst_ref[0, rows, :] = jnp.where(first_head, x, other)
        dst_ref[1, rows, :] = jnp.where(first_head, other, x)

    def class_major_rows(t, r):
        blocks_per_class = seq // r // ATT_QBLK
        rho = t // blocks_per_class
        l0 = (t % blocks_per_class) * ATT_QBLK
        return pl.ds(rho + r * l0, ATT_QBLK, stride=r)

    widen(q_ref)

    def q_gather(t, carry):
        qs[pl.ds(pl.multiple_of(t * ATT_QBLK, ATT_QBLK), ATT_QBLK), :] = xf[class_major_rows(t, ATT_MAX_DILATION), :]
        return carry

    lax.fori_loop(0, seq // ATT_QBLK, q_gather, 0)

    for pattern, r in enumerate(DILATIONS):
        sub_len = seq // r
        n_keys = min(2 * ATT_QBLK, sub_len)
        blocks_per_class = sub_len // ATT_QBLK
        n_runs = ATT_MAX_DILATION // r
        run_len = ATT_QBLK // n_runs

        if r == 1:
            def copy(c, carry):
                store_split(kd, conv_rows(c), k_ref[0, conv_rows(c), :], 0.0)
                store_split(vd, conv_rows(c), v_ref[0, conv_rows(c), :], 1.0)
                return carry

            lax.fori_loop(0, seq // ATT_CONV_ROWS, copy, 0)
        else:
            for src_ref, dst_ref, fill in ((k_ref, kd, 0.0), (v_ref, vd, 1.0)):
                widen(src_ref)

                def gather(t, carry, dst_ref=dst_ref, fill=fill, r=r):
                    rows = pl.ds(pl.multiple_of(t * ATT_QBLK, ATT_QBLK), ATT_QBLK)
                    store_split(dst_ref, rows, xf[class_major_rows(t, r), :].astype(BF16), fill)
                    return carry

                lax.fori_loop(0, seq // ATT_QBLK, gather, 0)

        n_idx = lax.broadcasted_iota(jnp.int32, (ATT_QBLK, n_keys), 0)
        q_rel = n_runs * (n_idx % run_len) + n_idx // run_len
        k_rel = lax.broadcasted_iota(jnp.int32, (ATT_QBLK, n_keys), 1)
        offsets = (0, ATT_HALF_WINDOW, 2 * ATT_HALF_WINDOW) if blocks_per_class > 1 else (0,)
        for variant, off in enumerate(offsets):
            dist = jnp.abs(k_rel - off - q_rel)
            for h in range(2):
                scaled = (-slopes[h] * float(r)) * dist.astype(F32)
                bias[h, variant, :, 0:n_keys] = jnp.where(dist <= ATT_HALF_WINDOW, scaled, NEG)

        def geometry(t, r=r, blocks_per_class=blocks_per_class, n_runs=n_runs, run_len=run_len):
            rho = t // blocks_per_class
            lb = t % blocks_per_class
            if blocks_per_class > 1:
                variant = jnp.where(lb == 0, 0, jnp.where(lb == blocks_per_class - 1, 2, 1))
            else:
                variant = 0
            k_base = pl.multiple_of(t * ATT_QBLK - variant * ATT_HALF_WINDOW, ATT_HALF_WINDOW)
            runs = [pl.ds(pl.multiple_of((rho + r * j) * base_len + lb * run_len, 8), run_len)
                    for j in range(n_runs)]
            return variant, k_base, runs

        def load_runs(ref, runs):
            return jnp.concatenate([ref[rows, :] for rows in runs], axis=0) if len(runs) > 1 else ref[runs[0], :]

        def store_runs(ref, runs, x, run_len=run_len):
            for j, rows in enumerate(runs):
                ref[rows, :] = x[j * run_len:(j + 1) * run_len, :]

        def group(g, carry, n_keys=n_keys, pattern=pattern, geometry=geometry, load_runs=load_runs,
                  store_runs=store_runs):
            for i in range(ATT_GROUP):
                variant, k_base, runs = geometry(g * ATT_GROUP + i)
                q = load_runs(qs, runs).astype(BF16)
                for h in range(2):
                    sc = lax.dot_general(q, kd[h, pl.ds(k_base, n_keys), :], (((1,), (1,)), ((), ())),
                                         preferred_element_type=F32)
                    sc_buf[i, h, :, 0:n_keys] = sc + bias[h, variant, :, 0:n_keys]
            for i in range(ATT_GROUP):
                ms = []
                for h in range(2):
                    sc = sc_buf[i, h, :, 0:n_keys]
                    m = jnp.max(sc, axis=-1, keepdims=True)
                    p_buf[i, h, :, 0:n_keys] = jnp.exp(sc - m).astype(BF16)
                    ms.append(m)
                m_buf[i] = jnp.where(first_head, ms[0], ms[1])
            for i in range(ATT_GROUP):
                _, k_base, runs = geometry(g * ATT_GROUP + i)
                k_rows = pl.ds(k_base, n_keys)
                pv0 = jnp.dot(p_buf[i, 0, :, 0:n_keys], vd[0, k_rows, :], preferred_element_type=F32)
                pv1 = jnp.dot(p_buf[i, 1, :, 0:n_keys], vd[1, k_rows, :], preferred_element_type=F32)
                a_new = jnp.where(first_head, pv0, pv1)
                l_new = pltpu.roll(jnp.where(first_head, pv1, pv0), ATT_HEAD_DIM, 1)
                m_new = m_buf[i]
                if pattern == 0:
                    store_runs(m_s, runs, m_new)
                    store_runs(l_s, runs, l_new)
                    store_runs(a_s, runs, a_new)
                else:
                    m_old = load_runs(m_s, runs)
                    m_tot = jnp.maximum(m_old, m_new)
                    w_old = jnp.exp(m_old - m_tot)
                    w_new = jnp.exp(m_new - m_tot)
                    store_runs(m_s, runs, m_tot)
                    store_runs(l_s, runs, w_old * load_runs(l_s, runs) + w_new * l_new)
                    store_runs(a_s, runs, w_old * load_runs(a_s, runs) + w_new * a_new)
            return carry

        lax.fori_loop(0, seq // (ATT_QBLK * ATT_GROUP), group, 0)

    def scatter(t, carry):
        rows = pl.ds(pl.multiple_of(t * ATT_QBLK, ATT_QBLK), ATT_QBLK)
        xf[class_major_rows(t, ATT_MAX_DILATION), :] = a_s[rows, :] / l_s[rows, :]
        return carry

    lax.fori_loop(0, seq // ATT_QBLK, scatter, 0)

    def finish(c, carry):
        o_ref[0, conv_rows(c), :] = xf[conv_rows(c), :].astype(BF16)
        return carry

    lax.fori_loop(0, seq // ATT_CONV_ROWS, finish, 0)


def _attention(qkv, slopes):
    b, s, _ = qkv.shape
    n_pairs = ATT_WIDTH // LANES
    blk = lambda off: pl.BlockSpec((1, s, LANES), lambda bi, j, off=off: (bi, 0, off + j))
    return pl.pallas_call(
        functools.partial(_attn_kernel, seq=s),
        grid=(b, n_pairs),
        in_specs=[pl.BlockSpec(memory_space=pltpu.SMEM), blk(0), blk(n_pairs), blk(2 * n_pairs)],
        out_specs=pl.BlockSpec((1, s, LANES), lambda bi, j: (bi, 0, j)),
        out_shape=jax.ShapeDtypeStruct((b, s, ATT_WIDTH), BF16),
        scratch_shapes=[
            pltpu.VMEM((s, LANES), F32),
            pltpu.VMEM((s, LANES), F32),
            pltpu.VMEM((2, s, LANES), BF16),
            pltpu.VMEM((2, s, LANES), BF16),
            pltpu.VMEM((2, 3, ATT_QBLK, 2 * ATT_QBLK), F32),
            pltpu.VMEM((ATT_GROUP, 2, ATT_QBLK, 2 * ATT_QBLK), F32),
            pltpu.VMEM((ATT_GROUP, 2, ATT_QBLK, 2 * ATT_QBLK), BF16),
            pltpu.VMEM((ATT_GROUP, ATT_QBLK, LANES), F32),
            pltpu.VMEM((s, LANES), F32),
            pltpu.VMEM((s, LANES), F32),
            pltpu.VMEM((s, LANES), F32),
        ],
        compiler_params=_params(("parallel", "parallel")),
        name="dilated_attention",
    )(slopes, qkv, qkv, qkv)


def _even_out_kernel(x_ref, f_ref, a_ref, w_ref, o_ref):
    acc = x_ref[0]
    acc = acc + jnp.dot(f_ref[...], w_ref[0:FNET_WIDTH, :], preferred_element_type=F32)
    acc = acc + jnp.dot(a_ref[0], w_ref[FNET_WIDTH:, :], preferred_element_type=F32)
    o_ref[0] = acc


def _even_out(x, fmix, att, w_out, tm=512):
    b, s, d = x.shape
    return pl.pallas_call(
        _even_out_kernel,
        grid=(b, s // tm),
        in_specs=[
            pl.BlockSpec((1, tm, d), lambda bi, i: (bi, i, 0)),
            pl.BlockSpec((tm, FNET_WIDTH), lambda bi, i: (i, bi)),
            pl.BlockSpec((1, tm, ATT_WIDTH), lambda bi, i: (bi, i, 0)),
            pl.BlockSpec((d, d), lambda bi, i: (0, 0)),
        ],
        out_specs=pl.BlockSpec((1, tm, d), lambda bi, i: (bi, i, 0)),
        out_shape=jax.ShapeDtypeStruct((b, s, d), F32),
        compiler_params=_params(("parallel", "parallel")),
        name="even_out_proj",
    )(x, fmix, att, w_out)


def _swiglu_kernel(x_ref, ln_ref, wg_ref, wu_ref, wd_ref, o_ref, h_ref, acc_ref):
    f = pl.program_id(1)

    @pl.when(f == 0)
    def _():
        x = x_ref[...]
        h_ref[...] = _rms(x, ln_ref[...]).astype(BF16)
        acc_ref[...] = x

    h = h_ref[...]
    g = jnp.dot(h, wg_ref[...], preferred_element_type=F32)
    u = jnp.dot(h, wu_ref[...], preferred_element_type=F32)
    act = (_silu(g) * u).astype(BF16)
    acc_ref[...] += jnp.dot(act, wd_ref[...], preferred_element_type=F32)

    @pl.when(f == pl.num_programs(1) - 1)
    def _():
        o_ref[...] = acc_ref[...]


def _swiglu(x, ln, wg, wu, wd, tm=512, tf=1408):
    n, d = x.shape
    dff = wg.shape[1]
    return pl.pallas_call(
        _swiglu_kernel,
        grid=(n // tm, dff // tf),
        in_specs=[
            pl.BlockSpec((tm, d), lambda i, f: (i, 0)),
            pl.BlockSpec((1, d), lambda i, f: (0, 0)),
            pl.BlockSpec((d, tf), lambda i, f: (0, f)),
            pl.BlockSpec((d, tf), lambda i, f: (0, f)),
            pl.BlockSpec((tf, d), lambda i, f: (f, 0)),
        ],
        out_specs=pl.BlockSpec((tm, d), lambda i, f: (i, 0)),
        out_shape=jax.ShapeDtypeStruct((n, d), F32),
        scratch_shapes=[pltpu.VMEM((tm, d), BF16), pltpu.VMEM((tm, d), F32)],
        compiler_params=_params(("parallel", "arbitrary")),
        name="swiglu_ffn",
    )(x, ln, wg, wu, wd)


def _odd_in_kernel(x_ref, ln_ref, w_ref, o_ref, h_ref):
    @pl.when(pl.program_id(1) == 0)
    def _():
        h_ref[...] = _rms(x_ref[...], ln_ref[...]).astype(BF16)

    o_ref[...] = jnp.dot(h_ref[...], w_ref[...], preferred_element_type=F32).astype(BF16)


def _odd_in(x, ln, w_in, tm=1024, tn=2048):
    n, d = x.shape
    return pl.pallas_call(
        _odd_in_kernel,
        grid=(n // tm, ODD_IN // tn),
        in_specs=[
            pl.BlockSpec((tm, d), lambda i, j: (i, 0)),
            pl.BlockSpec((1, d), lambda i, j: (0, 0)),
            pl.BlockSpec((d, tn), lambda i, j: (0, j)),
        ],
        out_specs=pl.BlockSpec((tm, tn), lambda i, j: (i, j)),
        out_shape=jax.ShapeDtypeStruct((n, ODD_IN), BF16),
        scratch_shapes=[pltpu.VMEM((tm, d), BF16)],
        compiler_params=_params(("parallel", "arbitrary")),
        name="odd_in_proj",
    )(x, ln, w_in)


RET_CHUNK = 256
RET_TILE = 1024


def _chunk_positions():
    return lax.broadcasted_iota(jnp.int32, (RET_CHUNK, 1), 0).astype(F32)


def _kt_v(k_scaled, v):
    return lax.dot_general(k_scaled, v, (((0,), (0,)), ((), ())), preferred_element_type=F32)


def _ret_bwd_state_kernel(lgb_ref, k_ref, v_ref, sb_ref, state):
    head = pl.program_id(1)

    @pl.when(pl.program_id(2) == 0)
    def _():
        state[...] = jnp.zeros_like(state)

    lg = lgb_ref[head]
    key_decay = jnp.exp(lg * _chunk_positions())
    chunk_decay = jnp.exp(jnp.full((1, RET_DV), lg * RET_CHUNK, F32))
    for c in reversed(range(RET_TILE // RET_CHUNK)):
        rows = slice(c * RET_CHUNK, (c + 1) * RET_CHUNK)
        sb_ref[0, 0, c] = state[...].astype(BF16)
        kd = (k_ref[0, rows, :].astype(F32) * key_decay).astype(BF16)
        state[...] = state[...] * chunk_decay + _kt_v(kd, v_ref[0, rows, :])


def _ret_bwd_states(z, lgb):
    b, s, _ = z.shape
    n = s // RET_CHUNK
    n_tiles = s // RET_TILE
    per_tile = RET_TILE // RET_CHUNK
    k_off = RET_QK_WIDTH // RET_DK
    v_off = 2 * RET_QK_WIDTH // RET_DV
    return pl.pallas_call(
        _ret_bwd_state_kernel,
        grid=(b, RET_HEADS, n_tiles),
        in_specs=[
            pl.BlockSpec(memory_space=pltpu.SMEM),
            pl.BlockSpec((1, RET_TILE, RET_DK), lambda bi, h, t: (bi, n_tiles - 1 - t, k_off + h)),
            pl.BlockSpec((1, RET_TILE, RET_DV), lambda bi, h, t: (bi, n_tiles - 1 - t, v_off + h)),
        ],
        out_specs=pl.BlockSpec((1, 1, per_tile, RET_DK, RET_DV), lambda bi, h, t: (bi, h, n_tiles - 1 - t, 0, 0)),
        out_shape=jax.ShapeDtypeStruct((b, RET_HEADS, n, RET_DK, RET_DV), BF16),
        scratch_shapes=[pltpu.VMEM((RET_DK, RET_DV), F32)],
        compiler_params=_params(("parallel", "parallel", "arbitrary")),
        name="retention_bwd_states",
    )(lgb, z, z)


def _ret_main_kernel(lgf_ref, lgb_ref, q_ref, k_ref, v_ref, g_ref, sb_ref, gn_ref, o_ref, state):
    head = pl.program_id(1)

    @pl.when(pl.program_id(2) == 0)
    def _():
        state[...] = jnp.zeros_like(state)

    lgf = lgf_ref[head]
    lgb = lgb_ref[head]
    c = RET_CHUNK
    pos = _chunk_positions()
    rel = (lax.broadcasted_iota(jnp.int32, (c, c), 0) - lax.broadcasted_iota(jnp.int32, (c, c), 1)).astype(F32)
    decay = jnp.exp(jnp.where(rel >= 0, lgf * rel, -lgb * rel))
    q_decay_f = jnp.exp(lgf * (pos + 1.0))
    q_decay_b = jnp.exp(lgb * (c - pos))
    k_decay_f = jnp.exp(lgf * (c - 1.0 - pos))
    chunk_decay = jnp.exp(jnp.full((1, RET_DV), lgf * c, F32))
    gn = gn_ref[...]

    for ci in range(RET_TILE // RET_CHUNK):
        rows = slice(ci * c, (ci + 1) * c)
        q = q_ref[0, rows, :]
        k = k_ref[0, rows, :]
        v = v_ref[0, rows, :]
        qf32 = q.astype(F32)
        sc = lax.dot_general(q, k, (((1,), (1,)), ((), ())), preferred_element_type=F32)
        y = jnp.dot((sc * decay).astype(BF16), v, preferred_element_type=F32)
        y = y + jnp.dot((qf32 * q_decay_f).astype(BF16), state[...].astype(BF16), preferred_element_type=F32)
        y = y + jnp.dot((qf32 * q_decay_b).astype(BF16), sb_ref[0, 0, ci], preferred_element_type=F32)
        kd = (k.astype(F32) * k_decay_f).astype(BF16)
        state[...] = state[...] * chunk_decay + _kt_v(kd, v)
        yn = _rms(y, gn)
        o_ref[0, rows, :] = (_silu(g_ref[0, rows, :].astype(F32)) * yn).astype(BF16)


def _ret_main(z, sb, lgf, lgb, gn):
    b, s, _ = z.shape
    n_tiles = s // RET_TILE
    per_tile = RET_TILE // RET_CHUNK
    k_off = RET_QK_WIDTH // RET_DK
    v_off = 2 * RET_QK_WIDTH // RET_DV
    g_off = (2 * RET_QK_WIDTH + RET_V_WIDTH) // RET_DV
    smem = pl.BlockSpec(memory_space=pltpu.SMEM)
    return pl.pallas_call(
        _ret_main_kernel,
        grid=(b, RET_HEADS, n_tiles),
        in_specs=[
            smem, smem,
            pl.BlockSpec((1, RET_TILE, RET_DK), lambda bi, h, t: (bi, t, h)),
            pl.BlockSpec((1, RET_TILE, RET_DK), lambda bi, h, t: (bi, t, k_off + h)),
            pl.BlockSpec((1, RET_TILE, RET_DV), lambda bi, h, t: (bi, t, v_off + h)),
            pl.BlockSpec((1, RET_TILE, RET_DV), lambda bi, h, t: (bi, t, g_off + h)),
            pl.BlockSpec((1, 1, per_tile, RET_DK, RET_DV), lambda bi, h, t: (bi, h, t, 0, 0)),
            pl.BlockSpec((1, RET_DV), lambda bi, h, t: (0, h)),
        ],
        out_specs=pl.BlockSpec((1, RET_TILE, RET_DV), lambda bi, h, t: (bi, t, h)),
        out_shape=jax.ShapeDtypeStruct((b, s, RET_V_WIDTH), BF16),
        scratch_shapes=[pltpu.VMEM((RET_DK, RET_DV), F32)],
        compiler_params=_params(("parallel", "parallel", "arbitrary")),
        name="retention_main",
    )(lgf, lgb, z, z, z, z, sb, gn)


def _odd_out_router_kernel(y_ref, x_ref, w_ref, ln_ref, wr_ref, xo_ref, h_ref, idx_ref, gate_ref):
    x = x_ref[...] + jnp.dot(y_ref[...], w_ref[...], preferred_element_type=F32)
    xo_ref[...] = x
    hn = _rms(x, ln_ref[...])
    h_ref[...] = hn.astype(BF16)
    logits = lax.dot_general(wr_ref[...], hn, (((1,), (1,)), ((), ())),
                             precision=lax.Precision.HIGHEST, preferred_element_type=F32)
    row = lax.broadcasted_iota(jnp.int32, logits.shape, 0).astype(F32)
    none = float(N_EXPERTS)
    m1 = jnp.max(logits, axis=0, keepdims=True)
    i1 = jnp.min(jnp.where(logits == m1, row, none), axis=0, keepdims=True)
    rest = jnp.where(row == i1, -jnp.inf, logits)
    m2 = jnp.max(rest, axis=0, keepdims=True)
    i2 = jnp.min(jnp.where(rest == m2, row, none), axis=0, keepdims=True)
    e2 = jnp.exp(m2 - m1)
    idx_ref[0:1, :] = i1.astype(jnp.int32)
    idx_ref[1:2, :] = i2.astype(jnp.int32)
    gate_ref[0:1, :] = 1.0 / (1.0 + e2)
    gate_ref[1:2, :] = e2 / (1.0 + e2)


def _odd_out_router(y, x, w_out, ln, wr_t, tm=512):
    n, d = x.shape
    full = lambda shp: pl.BlockSpec(shp, lambda i: (0,) * len(shp))
    return pl.pallas_call(
        _odd_out_router_kernel,
        grid=(n // tm,),
        in_specs=[
            pl.BlockSpec((tm, RET_V_WIDTH), lambda i: (i, 0)),
            pl.BlockSpec((tm, d), lambda i: (i, 0)),
            full((RET_V_WIDTH, d)), full((1, d)), full((N_EXPERTS, d)),
        ],
        out_specs=[
            pl.BlockSpec((tm, d), lambda i: (i, 0)),
            pl.BlockSpec((tm, d), lambda i: (i, 0)),
            pl.BlockSpec((2, tm), lambda i: (0, i)),
            pl.BlockSpec((2, tm), lambda i: (0, i)),
        ],
        out_shape=[
            jax.ShapeDtypeStruct((n, d), F32),
            jax.ShapeDtypeStruct((n, d), BF16),
            jax.ShapeDtypeStruct((2, n), jnp.int32),
            jax.ShapeDtypeStruct((2, n), F32),
        ],
        compiler_params=_params(("parallel",)),
        name="odd_out_proj_router",
    )(y, x, w_out, ln, wr_t)


MOE_TM = 1024
MOE_TF = 512


def _moe_kernel(tile_expert_ref, n_used_ref, xs_ref, gate_ref, w1_ref, w3_ref, w2_ref, o_ref, acc_ref):
    t = pl.program_id(0)
    f = pl.program_id(1)
    last = pl.num_programs(1) - 1
    used = t < n_used_ref[0]

    @pl.when(used)
    def _():
        @pl.when(f == 0)
        def _():
            acc_ref[...] = jnp.zeros_like(acc_ref)

        xs = xs_ref[...]
        g = jnp.dot(xs, w1_ref[...], preferred_element_type=F32)
        u = jnp.dot(xs, w3_ref[...], preferred_element_type=F32)
        act = (_silu(g) * u).astype(BF16)
        acc_ref[...] += jnp.dot(act, w2_ref[...], preferred_element_type=F32)

        @pl.when(f == last)
        def _():
            o_ref[...] = (acc_ref[...] * gate_ref[:, 0:1]).astype(BF16)

    @pl.when(jnp.logical_and(jnp.logical_not(used), f == last))
    def _():
        o_ref[...] = jnp.zeros_like(o_ref)


def _moe_experts(xs, gate_rows, tile_expert, n_used, w1, w3, w2):
    p, d = xs.shape
    n_tiles = p // MOE_TM
    n_f = D_FF_EXPERT // MOE_TF

    def row_tile(t, f, te, nu):
        return (jnp.minimum(t, nu[0] - 1), 0)

    def f_idx(t, f, nu):
        return jnp.where(t < nu[0], f, n_f - 1)

    grid_spec = pltpu.PrefetchScalarGridSpec(
        num_scalar_prefetch=2,
        grid=(n_tiles, n_f),
        in_specs=[
            pl.BlockSpec((MOE_TM, d), row_tile),
            pl.BlockSpec((MOE_TM, LANES), row_tile),
            pl.BlockSpec((None, d, MOE_TF), lambda t, f, te, nu: (te[t], 0, f_idx(t, f, nu))),
            pl.BlockSpec((None, d, MOE_TF), lambda t, f, te, nu: (te[t], 0, f_idx(t, f, nu))),
            pl.BlockSpec((None, MOE_TF, d), lambda t, f, te, nu: (te[t], f_idx(t, f, nu), 0)),
        ],
        out_specs=pl.BlockSpec((MOE_TM, d), lambda t, f, te, nu: (t, 0)),
        scratch_shapes=[pltpu.VMEM((MOE_TM, d), F32)],
    )
    return pl.pallas_call(
        _moe_kernel,
        grid_spec=grid_spec,
        out_shape=jax.ShapeDtypeStruct((p, d), BF16),
        compiler_params=_params(("arbitrary", "arbitrary")),
        name="moe_experts",
    )(tile_expert, n_used, xs, gate_rows, w1, w3, w2)


def _combine_kernel(x_ref, y_ref, o_ref):
    o_ref[...] = x_ref[...] + y_ref[0].astype(F32) + y_ref[1].astype(F32)


def _combine(x, y2, tm=1024):
    n, d = x.shape
    return pl.pallas_call(
        _combine_kernel,
        grid=(n // tm,),
        in_specs=[pl.BlockSpec((tm, d), lambda i: (i, 0)), pl.BlockSpec((2, tm, d), lambda i: (0, i, 0))],
        out_specs=pl.BlockSpec((tm, d), lambda i: (i, 0)),
        out_shape=jax.ShapeDtypeStruct((n, d), F32),
        compiler_params=_params(("parallel",)),
        name="moe_combine",
    )(x, y2)


def _route(idx, gates, n):
    e_flat = idx.reshape(-1)
    n_pairs = e_flat.shape[0]
    pair_id = jnp.arange(n_pairs, dtype=jnp.int32)
    order = (jnp.sort(e_flat * n_pairs + pair_id) % n_pairs).astype(jnp.int32)
    rank = jnp.argsort(order).astype(jnp.int32)
    counts = jnp.sum(e_flat[None, :] == jnp.arange(N_EXPERTS, dtype=jnp.int32)[:, None], axis=1).astype(jnp.int32)
    starts = jnp.cumsum(counts) - counts
    tiles_per_expert = (counts + MOE_TM - 1) // MOE_TM
    tile_ends = jnp.cumsum(tiles_per_expert)
    padded_starts = (tile_ends - tiles_per_expert) * MOE_TM
    dest = rank + (padded_starts - starts)[e_flat]
    n_rows = n_pairs + N_EXPERTS * MOE_TM
    n_tiles = n_rows // MOE_TM
    tile_expert = jnp.minimum(
        jnp.sum(jnp.arange(n_tiles, dtype=jnp.int32)[:, None] >= tile_ends[None, :], axis=1), N_EXPERTS - 1
    ).astype(jnp.int32)
    row_expert = jnp.repeat(tile_expert, MOE_TM)
    row_in_expert = jnp.arange(n_rows, dtype=jnp.int32) - padded_starts[row_expert]
    row_valid = row_in_expert < counts[row_expert]
    row_pair = order[jnp.clip(starts[row_expert] + row_in_expert, 0, n_pairs - 1)]
    src_token = jnp.where(row_valid, row_pair % n, 0).astype(jnp.int32)
    row_gate = jnp.where(row_valid, gates.reshape(-1)[row_pair], 0.0)
    n_used = tile_ends[-1:].astype(jnp.int32)
    return src_token, row_gate, dest, tile_expert, n_used


def _channel_dft():
    c = jnp.arange(FNET_GROUP_DIM, dtype=jnp.int32)
    ang = ((c[:, None] * c[None, :]) % FNET_GROUP_DIM).astype(F32) * (2.0 * math.pi / FNET_GROUP_DIM)
    eye = jnp.eye(FNET_GROUPS, dtype=F32)
    scale = FNET_GROUP_DIM ** -0.5
    return jnp.concatenate([jnp.kron(eye, jnp.cos(ang)) * scale, -jnp.kron(eye, jnp.sin(ang)) * scale], axis=1).astype(BF16)


def _sequence_dft(s):
    hi = s // 64
    k = jnp.arange(s, dtype=jnp.int32)
    j_hi = jnp.arange(hi, dtype=jnp.int32)
    j_lo = jnp.arange(64, dtype=jnp.int32)
    w = 2.0 * math.pi / s
    ang_a = (((j_hi[:, None] * k[None, :]) % hi) * 64).astype(F32) * w
    ang_b = ((j_lo[:, None] * k[None, :]) % s).astype(F32) * w
    ca, sa = jnp.cos(ang_a)[:, None, :], jnp.sin(ang_a)[:, None, :]
    cb, sb = jnp.cos(ang_b)[None, :, :], jnp.sin(ang_b)[None, :, :]
    scale = s ** -0.5
    cos_m = ((ca * cb - sa * sb) * scale).reshape(s, s)
    sin_m = ((sa * cb + ca * sb) * scale).reshape(s, s)
    return jnp.concatenate([cos_m, sin_m], axis=1).astype(BF16)


def _trunk(x, p):
    b, s, d = x.shape
    n = b * s
    y, qkv = _even_in(x, p["ln_mix_e"], p["w_in_e"], p["chan_dft"], p["head_sum"], p["qn"], p["kn"])
    fmix = _matmul(_sequence_dft(s), y.reshape(2 * s, b * FNET_WIDTH))
    att = _attention(qkv, p["slopes"])
    x = _even_out(x, fmix, att, p["w_out_e"])
    x = _swiglu(x.reshape(n, d), p["ln_ffn_e"], p["w_gate_e"], p["w_up_e"], p["w_down_e"])
    z = _odd_in(x, p["ln_mix_o"], p["w_in_o"]).reshape(b, s, ODD_IN)
    sb = _ret_bwd_states(z, p["lgb"])
    yr = _ret_main(z, sb, p["lgf"], p["lgb"], p["ret_gn"])
    x, h, idx, gates = _odd_out_router(yr.reshape(n, RET_V_WIDTH), x, p["w_out_o"], p["ln_ffn_o"], p["w_router_t"])
    src_token, row_gate, dest, tile_expert, n_used = _route(idx, gates, n)
    xs = jnp.take(h, src_token, axis=0)
    gate_rows = jnp.broadcast_to(row_gate[:, None], (row_gate.shape[0], LANES))
    ye = _moe_experts(xs, gate_rows, tile_expert, n_used, p["moe_w1"], p["moe_w3"], p["moe_w2"])
    y2 = jnp.take(ye, dest, axis=0).reshape(2, n, d)
    return _combine(x, y2).reshape(b, s, d)


def kernel(x_prompt, x_sample, ln_mix_e, w_in_e, w_out_e, qn_e, kn_e, ln_ffn_e, w_gate_e, w_up_e, w_down_e, ln_mix_o, w_in_o, w_out_o, logdecay_fwd, logdecay_bwd, ret_gn, ln_ffn_o, w_router, moe_w1, moe_w3, moe_w2):
    row = lambda w: w.reshape(1, -1).astype(F32)
    k_scale = jnp.concatenate([
        jnp.ones((RET_QK_WIDTH,), F32), jnp.full((RET_QK_WIDTH,), RET_DK ** -0.5, F32),
        jnp.ones((2 * RET_V_WIDTH,), F32)])
    head_id = jnp.arange(LANES, dtype=jnp.int32) // ATT_HEAD_DIM
    p = {
        "ln_mix_e": row(ln_mix_e[0]),
        "w_in_e": w_in_e[0].astype(BF16),
        "w_out_e": w_out_e[0].astype(BF16),
        "qn": row(jnp.tile(qn_e[0], 2) * (ATT_HEAD_DIM ** -0.5)),
        "kn": row(jnp.tile(kn_e[0], 2)),
        "ln_ffn_e": row(ln_ffn_e[0]),
        "w_gate_e": w_gate_e[0].astype(BF16),
        "w_up_e": w_up_e[0].astype(BF16),
        "w_down_e": w_down_e[0].astype(BF16),
        "ln_mix_o": row(ln_mix_o[0]),
        "w_in_o": (w_in_o[0] * k_scale[None, :]).astype(BF16),
        "w_out_o": w_out_o[0].astype(BF16),
        "lgf": logdecay_fwd[0].astype(F32),
        "lgb": logdecay_bwd[0].astype(F32),
        "ret_gn": row(ret_gn[0]),
        "ln_ffn_o": row(ln_ffn_o[0]),
        "w_router_t": w_router[0].T.astype(F32),
        "moe_w1": moe_w1[0].astype(BF16),
        "moe_w3": moe_w3[0].astype(BF16),
        "moe_w2": moe_w2[0].astype(BF16),
        "chan_dft": _channel_dft(),
        "head_sum": (head_id[:, None] == head_id[None, :]).astype(BF16),
        "slopes": jnp.exp2(-8.0 * (jnp.arange(ATT_HEADS, dtype=F32) + 1.0) / ATT_HEADS),
    }
    return (_trunk(x_prompt, p), _trunk(x_sample, p))
```

```python
import functools
import math

import jax
import jax.numpy as jnp
from jax import lax
from jax.experimental import pallas as pl
from jax.experimental.pallas import tpu as pltpu

D_MODEL = 1024
FNET_GROUPS = 4
FNET_GROUP_DIM = 64
FNET_WIDTH = 256
ATT_HEAD_DIM = 64
ATT_HEADS = 12
ATT_WIDTH = 768
ATT_HALF_WINDOW = 64
DILATIONS = (1, 4, 16)
EVEN_IN = FNET_WIDTH + 3 * ATT_WIDTH
RET_HEADS = 4
RET_DK = 256
RET_DV = 512
RET_QK_WIDTH = 1024
RET_V_WIDTH = 2048
ODD_IN = 6144
D_FF = 2816
N_EXPERTS = 8
D_FF_EXPERT = 3584
EPS = 1e-6
NEG = -1e30

LANES = 128
VMEM_LIMIT_BYTES = 56 * 2**20

F32 = jnp.float32
BF16 = jnp.bfloat16


def _params(semantics):
    return pltpu.CompilerParams(dimension_semantics=semantics, vmem_limit_bytes=VMEM_LIMIT_BYTES)


def _rms(x, w):
    return x * lax.rsqrt(jnp.mean(x * x, axis=-1, keepdims=True) + EPS) * w


def _silu(x):
    return x / (1.0 + jnp.exp(-x))


def _even_in_kernel(x_ref, ln_ref, w_ref, cd_ref, hs_ref, qn_ref, kn_ref, y_ref, qkv_ref):
    h = _rms(x_ref[0], ln_ref[...]).astype(BF16)
    u = jnp.dot(h, w_ref[:, 0:FNET_WIDTH], preferred_element_type=F32).astype(BF16)
    yy = jnp.dot(u, cd_ref[...], preferred_element_type=F32)
    y_ref[0] = yy[:, :FNET_WIDTH].astype(BF16)
    y_ref[1] = yy[:, FNET_WIDTH:].astype(BF16)
    for part, n_ref in ((0, qn_ref), (1, kn_ref)):
        c0 = FNET_WIDTH + part * ATT_WIDTH
        z = jnp.dot(h, w_ref[:, c0:c0 + ATT_WIDTH], preferred_element_type=F32)
        for j in range(ATT_WIDTH // LANES):
            zj = z[:, j * LANES:(j + 1) * LANES]
            ss = jnp.dot((zj * zj).astype(BF16), hs_ref[...], preferred_element_type=F32)
            zn = zj * lax.rsqrt(ss * (1.0 / ATT_HEAD_DIM) + EPS) * n_ref[...]
            qkv_ref[0, :, part * ATT_WIDTH + j * LANES:part * ATT_WIDTH + (j + 1) * LANES] = zn.astype(BF16)
    c0 = FNET_WIDTH + 2 * ATT_WIDTH
    v = jnp.dot(h, w_ref[:, c0:c0 + ATT_WIDTH], preferred_element_type=F32)
    qkv_ref[0, :, 2 * ATT_WIDTH:3 * ATT_WIDTH] = v.astype(BF16)


def _even_in(x, ln, w_in, cd, hs, qn, kn, tm=512):
    b, s, d = x.shape
    full = lambda shp: pl.BlockSpec(shp, lambda bi, i: (0,) * len(shp))
    return pl.pallas_call(
        _even_in_kernel,
        grid=(b, s // tm),
        in_specs=[
            pl.BlockSpec((1, tm, d), lambda bi, i: (bi, i, 0)),
            full((1, d)), full((d, EVEN_IN)), full((FNET_WIDTH, 2 * FNET_WIDTH)), full((LANES, LANES)),
            full((1, LANES)), full((1, LANES)),
        ],
        out_specs=[
            pl.BlockSpec((2, tm, FNET_WIDTH), lambda bi, i: (0, i, bi)),
            pl.BlockSpec((1, tm, 3 * ATT_WIDTH), lambda bi, i: (bi, i, 0)),
        ],
        out_shape=[
            jax.ShapeDtypeStruct((2, s, b * FNET_WIDTH), BF16),
            jax.ShapeDtypeStruct((b, s, 3 * ATT_WIDTH), BF16),
        ],
        compiler_params=_params(("parallel", "parallel")),
        name="even_in_proj",
    )(x, ln, w_in, cd, hs, qn, kn)


def _matmul_kernel(a_ref, b_ref, o_ref, acc_ref):
    @pl.when(pl.program_id(2) == 0)
    def _():
        acc_ref[...] = jnp.zeros_like(acc_ref)

    acc_ref[...] += jnp.dot(a_ref[...], b_ref[...], preferred_element_type=F32)

    @pl.when(pl.program_id(2) == pl.num_programs(2) - 1)
    def _():
        o_ref[...] = acc_ref[...].astype(o_ref.dtype)


def _matmul(a, b, tm=1024, tn=2048, tk=1024):
    m, k = a.shape
    _, n = b.shape
    tm, tn, tk = min(tm, m), min(tn, n), min(tk, k)
    return pl.pallas_call(
        _matmul_kernel,
        grid=(m // tm, n // tn, k // tk),
        in_specs=[pl.BlockSpec((tm, tk), lambda i, j, kk: (i, kk)),
                  pl.BlockSpec((tk, tn), lambda i, j, kk: (kk, j))],
        out_specs=pl.BlockSpec((tm, tn), lambda i, j, kk: (i, j)),
        out_shape=jax.ShapeDtypeStruct((m, n), BF16),
        scratch_shapes=[pltpu.VMEM((tm, tn), F32)],
        compiler_params=_params(("parallel", "parallel", "arbitrary")),
        name="seq_dft_matmul",
    )(a, b)


ATT_QBLK = 128
ATT_CONV_ROWS = 512
ATT_GROUP = 8
ATT_MAX_DILATION = 16


def _attn_kernel(slopes_ref, q_ref, k_ref, v_ref, o_ref, xf, qs, kd, vd, bias, sc_buf, p_buf, m_buf, m_s, l_s, a_s,
                 *, seq):
    pair = pl.program_id(1)
    lane = lax.broadcasted_iota(jnp.int32, (1, LANES), 1)
    first_head = lane < ATT_HEAD_DIM
    slopes = (slopes_ref[2 * pair], slopes_ref[2 * pair + 1])
    base_len = seq // ATT_MAX_DILATION

    def conv_rows(c):
        return pl.ds(pl.multiple_of(c * ATT_CONV_ROWS, ATT_CONV_ROWS), ATT_CONV_ROWS)

    def widen(src_ref):
        def body(c, carry):
            xf[conv_rows(c), :] = src_ref[0, conv_rows(c), :].astype(F32)
            return carry

        lax.fori_loop(0, seq // ATT_CONV_ROWS, body, 0)

    def store_split(dst_ref, rows, x, fill):
        other = jnp.full_like(x, fill)
        dst_ref[0, rows, :] = jnp.where(first_head, x, other)
        dst_ref[1, rows, :] = jnp.where(first_head, other, x)

    def class_major_rows(t, r):
        blocks_per_class = seq // r // ATT_QBLK
        rho = t // blocks_per_class
        l0 = (t % blocks_per_class) * ATT_QBLK
        return pl.ds(rho + r * l0, ATT_QBLK, stride=r)

    widen(q_ref)

    def q_gather(t, carry):
        qs[pl.ds(pl.multiple_of(t * ATT_QBLK, ATT_QBLK), ATT_QBLK), :] = xf[class_major_rows(t, ATT_MAX_DILATION), :]
        return carry

    lax.fori_loop(0, seq // ATT_QBLK, q_gather, 0)

    for pattern, r in enumerate(DILATIONS):
        sub_len = seq // r
        n_keys = min(2 * ATT_QBLK, sub_len)
        blocks_per_class = sub_len // ATT_QBLK
        n_runs = ATT_MAX_DILATION // r
        run_len = ATT_QBLK // n_runs

        if r == 1:
            def copy(c, carry):
                store_split(kd, conv_rows(c), k_ref[0, conv_rows(c), :], 0.0)
                store_split(vd, conv_rows(c), v_ref[0, conv_rows(c), :], 1.0)
                return carry

            lax.fori_loop(0, seq // ATT_CONV_ROWS, copy, 0)
        else:
            for src_ref, dst_ref, fill in ((k_ref, kd, 0.0), (v_ref, vd, 1.0)):
                widen(src_ref)

                def gather(t, carry, dst_ref=dst_ref, fill=fill, r=r):
                    rows = pl.ds(pl.multiple_of(t * ATT_QBLK, ATT_QBLK), ATT_QBLK)
                    store_split(dst_ref, rows, xf[class_major_rows(t, r), :].astype(BF16), fill)
                    return carry

                lax.fori_loop(0, seq // ATT_QBLK, gather, 0)

        n_idx = lax.broadcasted_iota(jnp.int32, (ATT_QBLK, n_keys), 0)
        q_rel = n_runs * (n_idx % run_len) + n_idx // run_len
        k_rel = lax.broadcasted_iota(jnp.int32, (ATT_QBLK, n_keys), 1)
        offsets = (0, ATT_HALF_WINDOW, 2 * ATT_HALF_WINDOW) if blocks_per_class > 1 else (0,)
        for variant, off in enumerate(offsets):
            dist = jnp.abs(k_rel - off - q_rel)
            for h in range(2):
                scaled = (-slopes[h] * float(r)) * dist.astype(F32)
                bias[h, variant, :, 0:n_keys] = jnp.where(dist <= ATT_HALF_WINDOW, scaled, NEG)

        def geometry(t, r=r, blocks_per_class=blocks_per_class, n_runs=n_runs, run_len=run_len):
            rho = t // blocks_per_class
            lb = t % blocks_per_class
            if blocks_per_class > 1:
                variant = jnp.where(lb == 0, 0, jnp.where(lb == blocks_per_class - 1, 2, 1))
            else:
                variant = 0
            k_base = pl.multiple_of(t * ATT_QBLK - variant * ATT_HALF_WINDOW, ATT_HALF_WINDOW)
            runs = [pl.ds(pl.multiple_of((rho + r * j) * base_len + lb * run_len, 8), run_len)
                    for j in range(n_runs)]
            return variant, k_base, runs

        def load_runs(ref, runs):
            return jnp.concatenate([ref[rows, :] for rows in runs], axis=0) if len(runs) > 1 else ref[runs[0], :]

        def store_runs(ref, runs, x, run_len=run_len):
            for j, rows in enumerate(runs):
                ref[rows, :] = x[j * run_len:(j + 1) * run_len, :]

        def group(g, carry, n_keys=n_keys, pattern=pattern, geometry=geometry, load_runs=load_runs,
                  store_runs=store_runs):
            for i in range(ATT_GROUP):
                variant, k_base, runs = geometry(g * ATT_GROUP + i)
                q = load_runs(qs, runs).astype(BF16)
                for h in range(2):
                    sc = lax.dot_general(q, kd[h, pl.ds(k_base, n_keys), :], (((1,), (1,)), ((), ())),
                                         preferred_element_type=F32)
                    sc_buf[i, h, :, 0:n_keys] = sc + bias[h, variant, :, 0:n_keys]
            for i in range(ATT_GROUP):
                ms = []
                for h in range(2):
                    sc = sc_buf[i, h, :, 0:n_keys]
                    m = jnp.max(sc, axis=-1, keepdims=True)
                    p_buf[i, h, :, 0:n_keys] = jnp.exp(sc - m).astype(BF16)
                    ms.append(m)
                m_buf[i] = jnp.where(first_head, ms[0], ms[1])
            for i in range(ATT_GROUP):
                _, k_base, runs = geometry(g * ATT_GROUP + i)
                k_rows = pl.ds(k_base, n_keys)
                pv0 = jnp.dot(p_buf[i, 0, :, 0:n_keys], vd[0, k_rows, :], preferred_element_type=F32)
                pv1 = jnp.dot(p_buf[i, 1, :, 0:n_keys], vd[1, k_rows, :], preferred_element_type=F32)
                a_new = jnp.where(first_head, pv0, pv1)
                l_new = pltpu.roll(jnp.where(first_head, pv1, pv0), ATT_HEAD_DIM, 1)
                m_new = m_buf[i]
                if pattern == 0:
                    store_runs(m_s, runs, m_new)
                    store_runs(l_s, runs, l_new)
                    store_runs(a_s, runs, a_new)
                else:
                    m_old = load_runs(m_s, runs)
                    m_tot = jnp.maximum(m_old, m_new)
                    w_old = jnp.exp(m_old - m_tot)
                    w_new = jnp.exp(m_new - m_tot)
                    store_runs(m_s, runs, m_tot)
                    store_runs(l_s, runs, w_old * load_runs(l_s, runs) + w_new * l_new)
                    store_runs(a_s, runs, w_old * load_runs(a_s, runs) + w_new * a_new)
            return carry

        lax.fori_loop(0, seq // (ATT_QBLK * ATT_GROUP), group, 0)

    def scatter(t, carry):
        rows = pl.ds(pl.multiple_of(t * ATT_QBLK, ATT_QBLK), ATT_QBLK)
        xf[class_major_rows(t, ATT_MAX_DILATION), :] = a_s[rows, :] / l_s[rows, :]
        return carry

    lax.fori_loop(0, seq // ATT_QBLK, scatter, 0)

    def finish(c, carry):
        o_ref[0, conv_rows(c), :] = xf[conv_rows(c), :].astype(BF16)
        return carry

    lax.fori_loop(0, seq // ATT_CONV_ROWS, finish, 0)


def _attention(qkv, slopes):
    b, s, _ = qkv.shape
    n_pairs = ATT_WIDTH // LANES
    blk = lambda off: pl.BlockSpec((1, s, LANES), lambda bi, j, off=off: (bi, 0, off + j))
    return pl.pallas_call(
        functools.partial(_attn_kernel, seq=s),
        grid=(b, n_pairs),
        in_specs=[pl.BlockSpec(memory_space=pltpu.SMEM), blk(0), blk(n_pairs), blk(2 * n_pairs)],
        out_specs=pl.BlockSpec((1, s, LANES), lambda bi, j: (bi, 0, j)),
        out_shape=jax.ShapeDtypeStruct((b, s, ATT_WIDTH), BF16),
        scratch_shapes=[
            pltpu.VMEM((s, LANES), F32),
            pltpu.VMEM((s, LANES), F32),
            pltpu.VMEM((2, s, LANES), BF16),
            pltpu.VMEM((2, s, LANES), BF16),
            pltpu.VMEM((2, 3, ATT_QBLK, 2 * ATT_QBLK), F32),
            pltpu.VMEM((ATT_GROUP, 2, ATT_QBLK, 2 * ATT_QBLK), F32),
            pltpu.VMEM((ATT_GROUP, 2, ATT_QBLK, 2 * ATT_QBLK), BF16),
            pltpu.VMEM((ATT_GROUP, ATT_QBLK, LANES), F32),
            pltpu.VMEM((s, LANES), F32),
            pltpu.VMEM((s, LANES), F32),
            pltpu.VMEM((s, LANES), F32),
        ],
        compiler_params=_params(("parallel", "parallel")),
        name="dilated_attention",
    )(slopes, qkv, qkv, qkv)


def _even_out_kernel(x_ref, f_ref, a_ref, w_ref, o_ref):
    acc = x_ref[0]
    acc = acc + jnp.dot(f_ref[...], w_ref[0:FNET_WIDTH, :], preferred_element_type=F32)
    acc = acc + jnp.dot(a_ref[0], w_ref[FNET_WIDTH:, :], preferred_element_type=F32)
    o_ref[0] = acc


def _even_out(x, fmix, att, w_out, tm=512):
    b, s, d = x.shape
    return pl.pallas_call(
        _even_out_kernel,
        grid=(b, s // tm),
        in_specs=[
            pl.BlockSpec((1, tm, d), lambda bi, i: (bi, i, 0)),
            pl.BlockSpec((tm, FNET_WIDTH), lambda bi, i: (i, bi)),
            pl.BlockSpec((1, tm, ATT_WIDTH), lambda bi, i: (bi, i, 0)),
            pl.BlockSpec((d, d), lambda bi, i: (0, 0)),
        ],
        out_specs=pl.BlockSpec((1, tm, d), lambda bi, i: (bi, i, 0)),
        out_shape=jax.ShapeDtypeStruct((b, s, d), F32),
        compiler_params=_params(("parallel", "parallel")),
        name="even_out_proj",
    )(x, fmix, att, w_out)


def _resident(shape):
    return pl.BlockSpec(shape, lambda *_: (0,) * len(shape), pipeline_mode=pl.Buffered(1))


def _swiglu_kernel(x_ref, ln_ref, wg_ref, wu_ref, wd_ref, o_ref):
    x = x_ref[...]
    h = _rms(x, ln_ref[...]).astype(BF16)
    g = jnp.dot(h, wg_ref[...], preferred_element_type=F32)
    u = jnp.dot(h, wu_ref[...], preferred_element_type=F32)
    act = (_silu(g) * u).astype(BF16)
    o_ref[...] = x + jnp.dot(act, wd_ref[...], preferred_element_type=F32)


def _swiglu(x, ln, wg, wu, wd, tm=512):
    n, d = x.shape
    dff = wg.shape[1]
    return pl.pallas_call(
        _swiglu_kernel,
        grid=(n // tm,),
        in_specs=[
            pl.BlockSpec((tm, d), lambda i: (i, 0)),
            _resident((1, d)), _resident((d, dff)), _resident((d, dff)), _resident((dff, d)),
        ],
        out_specs=pl.BlockSpec((tm, d), lambda i: (i, 0)),
        out_shape=jax.ShapeDtypeStruct((n, d), F32),
        compiler_params=_params(("parallel",)),
        name="swiglu_ffn",
    )(x, ln, wg, wu, wd)


ODD_IN_COLS = 2048


def _odd_in_kernel(x_ref, ln_ref, w_ref, o_ref):
    h = _rms(x_ref[...], ln_ref[...]).astype(BF16)
    for c0 in range(0, ODD_IN, ODD_IN_COLS):
        cols = slice(c0, c0 + ODD_IN_COLS)
        o_ref[:, cols] = jnp.dot(h, w_ref[:, cols], preferred_element_type=F32).astype(BF16)


def _odd_in(x, ln, w_in, tm=512):
    n, d = x.shape
    return pl.pallas_call(
        _odd_in_kernel,
        grid=(n // tm,),
        in_specs=[pl.BlockSpec((tm, d), lambda i: (i, 0)), _resident((1, d)), _resident((d, ODD_IN))],
        out_specs=pl.BlockSpec((tm, ODD_IN), lambda i: (i, 0)),
        out_shape=jax.ShapeDtypeStruct((n, ODD_IN), BF16),
        compiler_params=_params(("parallel",)),
        name="odd_in_proj",
    )(x, ln, w_in)


RET_CHUNK = 256
RET_TILE = 1024


def _chunk_positions():
    return lax.broadcasted_iota(jnp.int32, (RET_CHUNK, 1), 0).astype(F32)


def _kt_v(k_scaled, v):
    return lax.dot_general(k_scaled, v, (((0,), (0,)), ((), ())), preferred_element_type=F32)


def _ret_bwd_state_kernel(lgb_ref, k_ref, v_ref, sb_ref, state):
    head = pl.program_id(1)

    @pl.when(pl.program_id(2) == 0)
    def _():
        state[...] = jnp.zeros_like(state)

    lg = lgb_ref[head]
    key_decay = jnp.exp(lg * _chunk_positions())
    chunk_decay = jnp.exp(jnp.full((1, RET_DV), lg * RET_CHUNK, F32))
    for c in reversed(range(RET_TILE // RET_CHUNK)):
        rows = slice(c * RET_CHUNK, (c + 1) * RET_CHUNK)
        sb_ref[0, 0, c] = state[...].astype(BF16)
        kd = (k_ref[0, rows, :].astype(F32) * key_decay).astype(BF16)
        state[...] = state[...] * chunk_decay + _kt_v(kd, v_ref[0, rows, :])


def _ret_bwd_states(z, lgb):
    b, s, _ = z.shape
    n = s // RET_CHUNK
    n_tiles = s // RET_TILE
    per_tile = RET_TILE // RET_CHUNK
    k_off = RET_QK_WIDTH // RET_DK
    v_off = 2 * RET_QK_WIDTH // RET_DV
    return pl.pallas_call(
        _ret_bwd_state_kernel,
        grid=(b, RET_HEADS, n_tiles),
        in_specs=[
            pl.BlockSpec(memory_space=pltpu.SMEM),
            pl.BlockSpec((1, RET_TILE, RET_DK), lambda bi, h, t: (bi, n_tiles - 1 - t, k_off + h)),
            pl.BlockSpec((1, RET_TILE, RET_DV), lambda bi, h, t: (bi, n_tiles - 1 - t, v_off + h)),
        ],
        out_specs=pl.BlockSpec((1, 1, per_tile, RET_DK, RET_DV), lambda bi, h, t: (bi, h, n_tiles - 1 - t, 0, 0)),
        out_shape=jax.ShapeDtypeStruct((b, RET_HEADS, n, RET_DK, RET_DV), BF16),
        scratch_shapes=[pltpu.VMEM((RET_DK, RET_DV), F32)],
        compiler_params=_params(("parallel", "parallel", "arbitrary")),
        name="retention_bwd_states",
    )(lgb, z, z)


def _ret_main_kernel(lgf_ref, lgb_ref, q_ref, k_ref, v_ref, g_ref, sb_ref, gn_ref, o_ref, state):
    head = pl.program_id(1)

    @pl.when(pl.program_id(2) == 0)
    def _():
        state[...] = jnp.zeros_like(state)

    lgf = lgf_ref[head]
    lgb = lgb_ref[head]
    c = RET_CHUNK
    pos = _chunk_positions()
    rel = (lax.broadcasted_iota(jnp.int32, (c, c), 0) - lax.broadcasted_iota(jnp.int32, (c, c), 1)).astype(F32)
    decay = jnp.exp(jnp.where(rel >= 0, lgf * rel, -lgb * rel))
    q_decay_f = jnp.exp(lgf * (pos + 1.0))
    q_decay_b = jnp.exp(lgb * (c - pos))
    k_decay_f = jnp.exp(lgf * (c - 1.0 - pos))
    chunk_decay = jnp.exp(jnp.full((1, RET_DV), lgf * c, F32))
    gn = gn_ref[...]

    for ci in range(RET_TILE // RET_CHUNK):
        rows = slice(ci * c, (ci + 1) * c)
        q = q_ref[0, rows, :]
        k = k_ref[0, rows, :]
        v = v_ref[0, rows, :]
        qf32 = q.astype(F32)
        sc = lax.dot_general(q, k, (((1,), (1,)), ((), ())), preferred_element_type=F32)
        y = jnp.dot((sc * decay).astype(BF16), v, preferred_element_type=F32)
        y = y + jnp.dot((qf32 * q_decay_f).astype(BF16), state[...].astype(BF16), preferred_element_type=F32)
        y = y + jnp.dot((qf32 * q_decay_b).astype(BF16), sb_ref[0, 0, ci], preferred_element_type=F32)
        kd = (k.astype(F32) * k_decay_f).astype(BF16)
        state[...] = state[...] * chunk_decay + _kt_v(kd, v)
        yn = _rms(y, gn)
        o_ref[0, rows, :] = (_silu(g_ref[0, rows, :].astype(F32)) * yn).astype(BF16)


def _ret_main(z, sb, lgf, lgb, gn):
    b, s, _ = z.shape
    n_tiles = s // RET_TILE
    per_tile = RET_TILE // RET_CHUNK
    k_off = RET_QK_WIDTH // RET_DK
    v_off = 2 * RET_QK_WIDTH // RET_DV
    g_off = (2 * RET_QK_WIDTH + RET_V_WIDTH) // RET_DV
    smem = pl.BlockSpec(memory_space=pltpu.SMEM)
    return pl.pallas_call(
        _ret_main_kernel,
        grid=(b, RET_HEADS, n_tiles),
        in_specs=[
            smem, smem,
            pl.BlockSpec((1, RET_TILE, RET_DK), lambda bi, h, t: (bi, t, h)),
            pl.BlockSpec((1, RET_TILE, RET_DK), lambda bi, h, t: (bi, t, k_off + h)),
            pl.BlockSpec((1, RET_TILE, RET_DV), lambda bi, h, t: (bi, t, v_off + h)),
            pl.BlockSpec((1, RET_TILE, RET_DV), lambda bi, h, t: (bi, t, g_off + h)),
            pl.BlockSpec((1, 1, per_tile, RET_DK, RET_DV), lambda bi, h, t: (bi, h, t, 0, 0)),
            pl.BlockSpec((1, RET_DV), lambda bi, h, t: (0, h)),
        ],
        out_specs=pl.BlockSpec((1, RET_TILE, RET_DV), lambda bi, h, t: (bi, t, h)),
        out_shape=jax.ShapeDtypeStruct((b, s, RET_V_WIDTH), BF16),
        scratch_shapes=[pltpu.VMEM((RET_DK, RET_DV), F32)],
        compiler_params=_params(("parallel", "parallel", "arbitrary")),
        name="retention_main",
    )(lgf, lgb, z, z, z, z, sb, gn)


def _odd_out_router_kernel(y_ref, x_ref, w_ref, ln_ref, wr_ref, xo_ref, h_ref, idx_ref, gate_ref):
    x = x_ref[...] + jnp.dot(y_ref[...], w_ref[...], preferred_element_type=F32)
    xo_ref[...] = x
    hn = _rms(x, ln_ref[...])
    h_ref[...] = hn.astype(BF16)
    logits = lax.dot_general(wr_ref[...], hn, (((1,), (1,)), ((), ())),
                             precision=lax.Precision.HIGHEST, preferred_element_type=F32)
    row = lax.broadcasted_iota(jnp.int32, logits.shape, 0).astype(F32)
    none = float(N_EXPERTS)
    m1 = jnp.max(logits, axis=0, keepdims=True)
    i1 = jnp.min(jnp.where(logits == m1, row, none), axis=0, keepdims=True)
    rest = jnp.where(row == i1, -jnp.inf, logits)
    m2 = jnp.max(rest, axis=0, keepdims=True)
    i2 = jnp.min(jnp.where(rest == m2, row, none), axis=0, keepdims=True)
    e2 = jnp.exp(m2 - m1)
    idx_ref[0:1, :] = i1.astype(jnp.int32)
    idx_ref[1:2, :] = i2.astype(jnp.int32)
    gate_ref[0:1, :] = 1.0 / (1.0 + e2)
    gate_ref[1:2, :] = e2 / (1.0 + e2)


def _odd_out_router(y, x, w_out, ln, wr_t, tm=512):
    n, d = x.shape
    full = lambda shp: pl.BlockSpec(shp, lambda i: (0,) * len(shp))
    return pl.pallas_call(
        _odd_out_router_kernel,
        grid=(n // tm,),
        in_specs=[
            pl.BlockSpec((tm, RET_V_WIDTH), lambda i: (i, 0)),
            pl.BlockSpec((tm, d), lambda i: (i, 0)),
            full((RET_V_WIDTH, d)), full((1, d)), full((N_EXPERTS, d)),
        ],
        out_specs=[
            pl.BlockSpec((tm, d), lambda i: (i, 0)),
            pl.BlockSpec((tm, d), lambda i: (i, 0)),
            pl.BlockSpec((2, tm), lambda i: (0, i)),
            pl.BlockSpec((2, tm), lambda i: (0, i)),
        ],
        out_shape=[
            jax.ShapeDtypeStruct((n, d), F32),
            jax.ShapeDtypeStruct((n, d), BF16),
            jax.ShapeDtypeStruct((2, n), jnp.int32),
            jax.ShapeDtypeStruct((2, n), F32),
        ],
        compiler_params=_params(("parallel",)),
        name="odd_out_proj_router",
    )(y, x, w_out, ln, wr_t)


MOE_TM = 1024
MOE_TF = 1792
MOE_SUB = 512


def _moe_kernel(tile_expert_ref, n_used_ref, xs_ref, gate_ref, w1_ref, w3_ref, w2_ref, o_ref, acc_ref):
    t = pl.program_id(0)
    f = pl.program_id(1)
    last = pl.num_programs(1) - 1
    used = t < n_used_ref[0]

    @pl.when(used)
    def _():
        for r0 in range(0, MOE_TM, MOE_SUB):
            rows = slice(r0, r0 + MOE_SUB)
            xs = xs_ref[rows, :]
            g = jnp.dot(xs, w1_ref[...], preferred_element_type=F32)
            u = jnp.dot(xs, w3_ref[...], preferred_element_type=F32)
            act = (_silu(g) * u).astype(BF16)
            part = jnp.dot(act, w2_ref[...], preferred_element_type=F32)

            @pl.when(f == 0)
            def _():
                acc_ref[rows, :] = part

            @pl.when(jnp.logical_and(f > 0, f < last))
            def _():
                acc_ref[rows, :] += part

            @pl.when(f == last)
            def _():
                o_ref[rows, :] = ((acc_ref[rows, :] + part) * gate_ref[rows, 0:1]).astype(BF16)

    @pl.when(jnp.logical_and(jnp.logical_not(used), f == last))
    def _():
        o_ref[...] = jnp.zeros_like(o_ref)


def _moe_experts(xs, gate_rows, tile_expert, n_used, w1, w3, w2):
    p, d = xs.shape
    n_tiles = p // MOE_TM
    n_f = D_FF_EXPERT // MOE_TF
    assert n_f >= 2 and n_f * MOE_TF == D_FF_EXPERT

    def row_tile(t, f, te, nu):
        return (jnp.minimum(t, nu[0] - 1), 0)

    def f_idx(t, f, nu):
        return jnp.where(t < nu[0], f, n_f - 1)

    grid_spec = pltpu.PrefetchScalarGridSpec(
        num_scalar_prefetch=2,
        grid=(n_tiles, n_f),
        in_specs=[
            pl.BlockSpec((MOE_TM, d), row_tile),
            pl.BlockSpec((MOE_TM, LANES), row_tile),
            pl.BlockSpec((None, d, MOE_TF), lambda t, f, te, nu: (te[t], 0, f_idx(t, f, nu))),
            pl.BlockSpec((None, d, MOE_TF), lambda t, f, te, nu: (te[t], 0, f_idx(t, f, nu))),
            pl.BlockSpec((None, MOE_TF, d), lambda t, f, te, nu: (te[t], f_idx(t, f, nu), 0)),
        ],
        out_specs=pl.BlockSpec((MOE_TM, d), lambda t, f, te, nu: (t, 0)),
        scratch_shapes=[pltpu.VMEM((MOE_TM, d), F32)],
    )
    return pl.pallas_call(
        _moe_kernel,
        grid_spec=grid_spec,
        out_shape=jax.ShapeDtypeStruct((p, d), BF16),
        compiler_params=_params(("arbitrary", "arbitrary")),
        name="moe_experts",
    )(tile_expert, n_used, xs, gate_rows, w1, w3, w2)


def _combine_kernel(x_ref, y_ref, o_ref):
    o_ref[...] = x_ref[...] + y_ref[0].astype(F32) + y_ref[1].astype(F32)


def _combine(x, y2, tm=1024):
    n, d = x.shape
    return pl.pallas_call(
        _combine_kernel,
        grid=(n // tm,),
        in_specs=[pl.BlockSpec((tm, d), lambda i: (i, 0)), pl.BlockSpec((2, tm, d), lambda i: (0, i, 0))],
        out_specs=pl.BlockSpec((tm, d), lambda i: (i, 0)),
        out_shape=jax.ShapeDtypeStruct((n, d), F32),
        compiler_params=_params(("parallel",)),
        name="moe_combine",
    )(x, y2)


def _route(idx, gates, n):
    e_flat = idx.reshape(-1)
    n_pairs = e_flat.shape[0]
    pair_id = jnp.arange(n_pairs, dtype=jnp.int32)
    order = (jnp.sort(e_flat * n_pairs + pair_id) % n_pairs).astype(jnp.int32)
    rank = jnp.argsort(order).astype(jnp.int32)
    counts = jnp.sum(e_flat[None, :] == jnp.arange(N_EXPERTS, dtype=jnp.int32)[:, None], axis=1).astype(jnp.int32)
    starts = jnp.cumsum(counts) - counts
    tiles_per_expert = (counts + MOE_TM - 1) // MOE_TM
    tile_ends = jnp.cumsum(tiles_per_expert)
    padded_starts = (tile_ends - tiles_per_expert) * MOE_TM
    dest = rank + (padded_starts - starts)[e_flat]
    n_rows = n_pairs + N_EXPERTS * MOE_TM
    n_tiles = n_rows // MOE_TM
    tile_expert = jnp.minimum(
        jnp.sum(jnp.arange(n_tiles, dtype=jnp.int32)[:, None] >= tile_ends[None, :], axis=1), N_EXPERTS - 1
    ).astype(jnp.int32)
    row_expert = jnp.repeat(tile_expert, MOE_TM)
    row_in_expert = jnp.arange(n_rows, dtype=jnp.int32) - padded_starts[row_expert]
    row_valid = row_in_expert < counts[row_expert]
    row_pair = order[jnp.clip(starts[row_expert] + row_in_expert, 0, n_pairs - 1)]
    src_token = jnp.where(row_valid, row_pair % n, 0).astype(jnp.int32)
    row_gate = jnp.where(row_valid, gates.reshape(-1)[row_pair], 0.0)
    n_used = tile_ends[-1:].astype(jnp.int32)
    return src_token, row_gate, dest, tile_expert, n_used


def _channel_dft():
    c = jnp.arange(FNET_GROUP_DIM, dtype=jnp.int32)
    ang = ((c[:, None] * c[None, :]) % FNET_GROUP_DIM).astype(F32) * (2.0 * math.pi / FNET_GROUP_DIM)
    eye = jnp.eye(FNET_GROUPS, dtype=F32)
    scale = FNET_GROUP_DIM ** -0.5
    return jnp.concatenate([jnp.kron(eye, jnp.cos(ang)) * scale, -jnp.kron(eye, jnp.sin(ang)) * scale], axis=1).astype(BF16)


def _sequence_dft(s):
    hi = s // 64
    k = jnp.arange(s, dtype=jnp.int32)
    j_hi = jnp.arange(hi, dtype=jnp.int32)
    j_lo = jnp.arange(64, dtype=jnp.int32)
    w = 2.0 * math.pi / s
    ang_a = (((j_hi[:, None] * k[None, :]) % hi) * 64).astype(F32) * w
    ang_b = ((j_lo[:, None] * k[None, :]) % s).astype(F32) * w
    ca, sa = jnp.cos(ang_a)[:, None, :], jnp.sin(ang_a)[:, None, :]
    cb, sb = jnp.cos(ang_b)[None, :, :], jnp.sin(ang_b)[None, :, :]
    scale = s ** -0.5
    cos_m = ((ca * cb - sa * sb) * scale).reshape(s, s)
    sin_m = ((sa * cb + ca * sb) * scale).reshape(s, s)
    return jnp.concatenate([cos_m, sin_m], axis=1).astype(BF16)


def _trunk(x, p):
    b, s, d = x.shape
    n = b * s
    y, qkv = _even_in(x, p["ln_mix_e"], p["w_in_e"], p["chan_dft"], p["head_sum"], p["qn"], p["kn"])
    fmix = _matmul(_sequence_dft(s), y.reshape(2 * s, b * FNET_WIDTH))
    att = _attention(qkv, p["slopes"])
    x = _even_out(x, fmix, att, p["w_out_e"])
    x = _swiglu(x.reshape(n, d), p["ln_ffn_e"], p["w_gate_e"], p["w_up_e"], p["w_down_e"])
    z = _odd_in(x, p["ln_mix_o"], p["w_in_o"]).reshape(b, s, ODD_IN)
    sb = _ret_bwd_states(z, p["lgb"])
    yr = _ret_main(z, sb, p["lgf"], p["lgb"], p["ret_gn"])
    x, h, idx, gates = _odd_out_router(yr.reshape(n, RET_V_WIDTH), x, p["w_out_o"], p["ln_ffn_o"], p["w_router_t"])
    src_token, row_gate, dest, tile_expert, n_used = _route(idx, gates, n)
    xs = h.at[src_token].get(mode="promise_in_bounds")
    gate_rows = jnp.broadcast_to(row_gate[:, None], (row_gate.shape[0], LANES))
    ye = _moe_experts(xs, gate_rows, tile_expert, n_used, p["moe_w1"], p["moe_w3"], p["moe_w2"])
    y2 = ye.at[dest].get(mode="promise_in_bounds").reshape(2, n, d)
    return _combine(x, y2).reshape(b, s, d)


def kernel(x_prompt, x_sample, ln_mix_e, w_in_e, w_out_e, qn_e, kn_e, ln_ffn_e, w_gate_e, w_up_e, w_down_e, ln_mix_o, w_in_o, w_out_o, logdecay_fwd, logdecay_bwd, ret_gn, ln_ffn_o, w_router, moe_w1, moe_w3, moe_w2):
    row = lambda w: w.reshape(1, -1).astype(F32)
    k_scale = jnp.concatenate([
        jnp.ones((RET_QK_WIDTH,), F32), jnp.full((RET_QK_WIDTH,), RET_DK ** -0.5, F32),
        jnp.ones((2 * RET_V_WIDTH,), F32)])
    head_id = jnp.arange(LANES, dtype=jnp.int32) // ATT_HEAD_DIM
    p = {
        "ln_mix_e": row(ln_mix_e[0]),
        "w_in_e": w_in_e[0].astype(BF16),
        "w_out_e": w_out_e[0].astype(BF16),
        "qn": row(jnp.tile(qn_e[0], 2) * (ATT_HEAD_DIM ** -0.5)),
        "kn": row(jnp.tile(kn_e[0], 2)),
        "ln_ffn_e": row(ln_ffn_e[0]),
        "w_gate_e": w_gate_e[0].astype(BF16),
        "w_up_e": w_up_e[0].astype(BF16),
        "w_down_e": w_down_e[0].astype(BF16),
        "ln_mix_o": row(ln_mix_o[0]),
        "w_in_o": (w_in_o[0] * k_scale[None, :]).astype(BF16),
        "w_out_o": w_out_o[0].astype(BF16),
        "lgf": logdecay_fwd[0].astype(F32),
        "lgb": logdecay_bwd[0].astype(F32),
        "ret_gn": row(ret_gn[0]),
        "ln_ffn_o": row(ln_ffn_o[0]),
        "w_router_t": w_router[0].T.astype(F32),
        "moe_w1": moe_w1[0].astype(BF16),
        "moe_w3": moe_w3[0].astype(BF16),
        "moe_w2": moe_w2[0].astype(BF16),
        "chan_dft": _channel_dft(),
        "head_sum": (head_id[:, None] == head_id[None, :]).astype(BF16),
        "slopes": jnp.exp2(-8.0 * (jnp.arange(ATT_HEADS, dtype=F32) + 1.0) / ATT_HEADS),
    }
    return (_trunk(x_prompt, p), _trunk(x_sample, p))
```

```python
import functools
import math

import jax
import jax.numpy as jnp
from jax import lax
from jax.experimental import pallas as pl
from jax.experimental.pallas import tpu as pltpu

D_MODEL = 1024
FNET_GROUPS = 4
FNET_GROUP_DIM = 64
FNET_WIDTH = 256
ATT_HEAD_DIM = 64
ATT_HEADS = 12
ATT_WIDTH = 768
ATT_HALF_WINDOW = 64
DILATIONS = (1, 4, 16)
EVEN_IN = FNET_WIDTH + 3 * ATT_WIDTH
RET_HEADS = 4
RET_DK = 256
RET_DV = 512
RET_QK_WIDTH = 1024
RET_V_WIDTH = 2048
ODD_IN = 6144
D_FF = 2816
N_EXPERTS = 8
D_FF_EXPERT = 3584
EPS = 1e-6
NEG = -1e30

LANES = 128
VMEM_LIMIT_BYTES = 56 * 2**20

F32 = jnp.float32
BF16 = jnp.bfloat16


def _params(semantics):
    return pltpu.CompilerParams(dimension_semantics=semantics, vmem_limit_bytes=VMEM_LIMIT_BYTES)


def _rms(x, w):
    return x * lax.rsqrt(jnp.mean(x * x, axis=-1, keepdims=True) + EPS) * w


def _silu(x):
    return x / (1.0 + jnp.exp(-x))


def _even_in_kernel(x_ref, ln_ref, w_ref, cd_ref, hs_ref, qn_ref, kn_ref, y_ref, qkv_ref):
    h = _rms(x_ref[0], ln_ref[...]).astype(BF16)
    u = jnp.dot(h, w_ref[:, 0:FNET_WIDTH], preferred_element_type=F32).astype(BF16)
    yy = jnp.dot(u, cd_ref[...], preferred_element_type=F32)
    y_ref[0] = yy[:, :FNET_WIDTH].astype(BF16)
    y_ref[1] = yy[:, FNET_WIDTH:].astype(BF16)
    for part, n_ref in ((0, qn_ref), (1, kn_ref)):
        c0 = FNET_WIDTH + part * ATT_WIDTH
        z = jnp.dot(h, w_ref[:, c0:c0 + ATT_WIDTH], preferred_element_type=F32)
        for j in range(ATT_WIDTH // LANES):
            zj = z[:, j * LANES:(j + 1) * LANES]
            ss = jnp.dot((zj * zj).astype(BF16), hs_ref[...], preferred_element_type=F32)
            zn = zj * lax.rsqrt(ss * (1.0 / ATT_HEAD_DIM) + EPS) * n_ref[...]
            qkv_ref[0, :, part * ATT_WIDTH + j * LANES:part * ATT_WIDTH + (j + 1) * LANES] = zn.astype(BF16)
    c0 = FNET_WIDTH + 2 * ATT_WIDTH
    v = jnp.dot(h, w_ref[:, c0:c0 + ATT_WIDTH], preferred_element_type=F32)
    qkv_ref[0, :, 2 * ATT_WIDTH:3 * ATT_WIDTH] = v.astype(BF16)


def _even_in(x, ln, w_in, cd, hs, qn, kn, tm=512):
    b, s, d = x.shape
    full = lambda shp: pl.BlockSpec(shp, lambda bi, i: (0,) * len(shp))
    return pl.pallas_call(
        _even_in_kernel,
        grid=(b, s // tm),
        in_specs=[
            pl.BlockSpec((1, tm, d), lambda bi, i: (bi, i, 0)),
            full((1, d)), full((d, EVEN_IN)), full((FNET_WIDTH, 2 * FNET_WIDTH)), full((LANES, LANES)),
            full((1, LANES)), full((1, LANES)),
        ],
        out_specs=[
            pl.BlockSpec((2, tm, FNET_WIDTH), lambda bi, i: (0, i, bi)),
            pl.BlockSpec((1, tm, 3 * ATT_WIDTH), lambda bi, i: (bi, i, 0)),
        ],
        out_shape=[
            jax.ShapeDtypeStruct((2, s, b * FNET_WIDTH), BF16),
            jax.ShapeDtypeStruct((b, s, 3 * ATT_WIDTH), BF16),
        ],
        compiler_params=_params(("parallel", "parallel")),
        name="even_in_proj",
    )(x, ln, w_in, cd, hs, qn, kn)


def _matmul_kernel(a_ref, b_ref, o_ref, acc_ref):
    @pl.when(pl.program_id(2) == 0)
    def _():
        acc_ref[...] = jnp.zeros_like(acc_ref)

    acc_ref[...] += jnp.dot(a_ref[...], b_ref[...], preferred_element_type=F32)

    @pl.when(pl.program_id(2) == pl.num_programs(2) - 1)
    def _():
        o_ref[...] = acc_ref[...].astype(o_ref.dtype)


def _matmul(a, b, tm=1024, tn=2048, tk=1024):
    m, k = a.shape
    _, n = b.shape
    tm, tn, tk = min(tm, m), min(tn, n), min(tk, k)
    return pl.pallas_call(
        _matmul_kernel,
        grid=(m // tm, n // tn, k // tk),
        in_specs=[pl.BlockSpec((tm, tk), lambda i, j, kk: (i, kk)),
                  pl.BlockSpec((tk, tn), lambda i, j, kk: (kk, j))],
        out_specs=pl.BlockSpec((tm, tn), lambda i, j, kk: (i, j)),
        out_shape=jax.ShapeDtypeStruct((m, n), BF16),
        scratch_shapes=[pltpu.VMEM((tm, tn), F32)],
        compiler_params=_params(("parallel", "parallel", "arbitrary")),
        name="seq_dft_matmul",
    )(a, b)


ATT_QBLK = 128
ATT_CONV_ROWS = 512
ATT_GROUP = 16
ATT_MAX_DILATION = 16


def _attn_kernel(slopes_ref, q_ref, k_ref, v_ref, o_ref, xf, qs, kd, vd, bias, sc_buf, p_buf, m_buf, m_s, l_s, a_s,
                 *, seq):
    pair = pl.program_id(1)
    lane = lax.broadcasted_iota(jnp.int32, (1, LANES), 1)
    first_head = lane < ATT_HEAD_DIM
    slopes = (slopes_ref[2 * pair], slopes_ref[2 * pair + 1])
    base_len = seq // ATT_MAX_DILATION

    def conv_rows(c):
        return pl.ds(pl.multiple_of(c * ATT_CONV_ROWS, ATT_CONV_ROWS), ATT_CONV_ROWS)

    def widen(src_ref):
        def body(c, carry):
            xf[conv_rows(c), :] = src_ref[0, conv_rows(c), :].astype(F32)
            return carry

        lax.fori_loop(0, seq // ATT_CONV_ROWS, body, 0)

    def store_split(dst_ref, rows, x, fill):
        other = jnp.full_like(x, fill)
        dst_ref[0, rows, :] = jnp.where(first_head, x, other)
        dst_ref[1, rows, :] = jnp.where(first_head, other, x)

    def class_major_rows(t, r):
        blocks_per_class = seq // r // ATT_QBLK
        rho = t // blocks_per_class
        l0 = (t % blocks_per_class) * ATT_QBLK
        return pl.ds(rho + r * l0, ATT_QBLK, stride=r)

    widen(q_ref)

    def q_gather(t, carry):
        qs[pl.ds(pl.multiple_of(t * ATT_QBLK, ATT_QBLK), ATT_QBLK), :] = xf[class_major_rows(t, ATT_MAX_DILATION), :]
        return carry

    lax.fori_loop(0, seq // ATT_QBLK, q_gather, 0)

    for pattern, r in enumerate(DILATIONS):
        sub_len = seq // r
        n_keys = min(2 * ATT_QBLK, sub_len)
        blocks_per_class = sub_len // ATT_QBLK
        n_runs = ATT_MAX_DILATION // r
        run_len = ATT_QBLK // n_runs

        n_kruns = 1 if r == 1 else n_runs
        krun_len = n_keys // n_kruns
        if r == 1:
            def copy(c, carry):
                store_split(kd, conv_rows(c), k_ref[0, conv_rows(c), :], 0.0)
                store_split(vd, conv_rows(c), v_ref[0, conv_rows(c), :], 1.0)
                return carry

            lax.fori_loop(0, seq // ATT_CONV_ROWS, copy, 0)
        elif pattern == 1:
            for src_ref, dst_ref, fill in ((k_ref, kd, 0.0), (v_ref, vd, 1.0)):
                widen(src_ref)

                def gather(t, carry, dst_ref=dst_ref, fill=fill):
                    rows = pl.ds(pl.multiple_of(t * ATT_QBLK, ATT_QBLK), ATT_QBLK)
                    store_split(dst_ref, rows, xf[class_major_rows(t, ATT_MAX_DILATION), :].astype(BF16), fill)
                    return carry

                lax.fori_loop(0, seq // ATT_QBLK, gather, 0)

        n_idx = lax.broadcasted_iota(jnp.int32, (ATT_QBLK, n_keys), 0)
        q_rel = n_runs * (n_idx % run_len) + n_idx // run_len
        c_idx = lax.broadcasted_iota(jnp.int32, (ATT_QBLK, n_keys), 1)
        k_rel = n_kruns * (c_idx % krun_len) + c_idx // krun_len
        offsets = (0, ATT_HALF_WINDOW, 2 * ATT_HALF_WINDOW) if blocks_per_class > 1 else (0,)
        for variant, off in enumerate(offsets):
            dist = jnp.abs(k_rel - off - q_rel)
            for h in range(2):
                scaled = (-slopes[h] * float(r)) * dist.astype(F32)
                bias[h, variant, :, 0:n_keys] = jnp.where(dist <= ATT_HALF_WINDOW, scaled, NEG)

        def geometry(t, r=r, blocks_per_class=blocks_per_class, n_runs=n_runs, run_len=run_len,
                     n_kruns=n_kruns, krun_len=krun_len, n_keys=n_keys):
            rho = t // blocks_per_class
            lb = t % blocks_per_class
            if blocks_per_class > 1:
                variant = jnp.where(lb == 0, 0, jnp.where(lb == blocks_per_class - 1, 2, 1))
            else:
                variant = 0
            runs = [pl.ds(pl.multiple_of((rho + r * j) * base_len + lb * run_len, 8), run_len)
                    for j in range(n_runs)]
            if r == 1:
                k_runs = [pl.ds(pl.multiple_of(t * ATT_QBLK - variant * ATT_HALF_WINDOW, ATT_HALF_WINDOW), n_keys)]
            else:
                start = lb * run_len - variant * (ATT_HALF_WINDOW // n_kruns)
                k_runs = [pl.ds(pl.multiple_of((rho + r * j) * base_len + start, 16), krun_len)
                          for j in range(n_kruns)]
            return variant, k_runs, runs

        def load_runs(ref, runs):
            return jnp.concatenate([ref[rows, :] for rows in runs], axis=0) if len(runs) > 1 else ref[runs[0], :]

        def store_runs(ref, runs, x, run_len=run_len):
            for j, rows in enumerate(runs):
                ref[rows, :] = x[j * run_len:(j + 1) * run_len, :]

        def group(g, carry, n_keys=n_keys, pattern=pattern, geometry=geometry, load_runs=load_runs,
                  store_runs=store_runs):
            for i in range(ATT_GROUP):
                variant, k_runs, runs = geometry(g * ATT_GROUP + i)
                q = load_runs(qs, runs).astype(BF16)
                for h in range(2):
                    sc = lax.dot_general(q, load_runs(kd.at[h], k_runs), (((1,), (1,)), ((), ())),
                                         preferred_element_type=F32)
                    sc_buf[i, h, :, 0:n_keys] = sc + bias[h, variant, :, 0:n_keys]
            for i in range(ATT_GROUP):
                ms = []
                for h in range(2):
                    sc = sc_buf[i, h, :, 0:n_keys]
                    m = jnp.max(sc, axis=-1, keepdims=True)
                    p_buf[i, h, :, 0:n_keys] = jnp.exp(sc - m).astype(BF16)
                    ms.append(m)
                m_buf[i] = jnp.where(first_head, ms[0], ms[1])
            for i in range(ATT_GROUP):
                _, k_runs, runs = geometry(g * ATT_GROUP + i)
                pv0 = jnp.dot(p_buf[i, 0, :, 0:n_keys], load_runs(vd.at[0], k_runs), preferred_element_type=F32)
                pv1 = jnp.dot(p_buf[i, 1, :, 0:n_keys], load_runs(vd.at[1], k_runs), preferred_element_type=F32)
                a_new = jnp.where(first_head, pv0, pv1)
                l_new = pltpu.roll(jnp.where(first_head, pv1, pv0), ATT_HEAD_DIM, 1)
                m_new = m_buf[i]
                if pattern == 0:
                    store_runs(m_s, runs, m_new)
                    store_runs(l_s, runs, l_new)
                    store_runs(a_s, runs, a_new)
                else:
                    m_old = load_runs(m_s, runs)
                    m_tot = jnp.maximum(m_old, m_new)
                    w_old = jnp.exp(m_old - m_tot)
                    w_new = jnp.exp(m_new - m_tot)
                    store_runs(m_s, runs, m_tot)
                    store_runs(l_s, runs, w_old * load_runs(l_s, runs) + w_new * l_new)
                    store_runs(a_s, runs, w_old * load_runs(a_s, runs) + w_new * a_new)
            return carry

        lax.fori_loop(0, seq // (ATT_QBLK * ATT_GROUP), group, 0)

    def scatter(t, carry):
        rows = pl.ds(pl.multiple_of(t * ATT_QBLK, ATT_QBLK), ATT_QBLK)
        xf[class_major_rows(t, ATT_MAX_DILATION), :] = a_s[rows, :] / l_s[rows, :]
        return carry

    lax.fori_loop(0, seq // ATT_QBLK, scatter, 0)

    def finish(c, carry):
        o_ref[0, conv_rows(c), :] = xf[conv_rows(c), :].astype(BF16)
        return carry

    lax.fori_loop(0, seq // ATT_CONV_ROWS, finish, 0)


def _attention(qkv, slopes):
    b, s, _ = qkv.shape
    n_pairs = ATT_WIDTH // LANES
    blk = lambda off: pl.BlockSpec((1, s, LANES), lambda bi, j, off=off: (bi, 0, off + j))
    return pl.pallas_call(
        functools.partial(_attn_kernel, seq=s),
        grid=(b, n_pairs),
        in_specs=[pl.BlockSpec(memory_space=pltpu.SMEM), blk(0), blk(n_pairs), blk(2 * n_pairs)],
        out_specs=pl.BlockSpec((1, s, LANES), lambda bi, j: (bi, 0, j)),
        out_shape=jax.ShapeDtypeStruct((b, s, ATT_WIDTH), BF16),
        scratch_shapes=[
            pltpu.VMEM((s, LANES), F32),
            pltpu.VMEM((s, LANES), F32),
            pltpu.VMEM((2, s, LANES), BF16),
            pltpu.VMEM((2, s, LANES), BF16),
            pltpu.VMEM((2, 3, ATT_QBLK, 2 * ATT_QBLK), F32),
            pltpu.VMEM((ATT_GROUP, 2, ATT_QBLK, 2 * ATT_QBLK), F32),
            pltpu.VMEM((ATT_GROUP, 2, ATT_QBLK, 2 * ATT_QBLK), BF16),
            pltpu.VMEM((ATT_GROUP, ATT_QBLK, LANES), F32),
            pltpu.VMEM((s, LANES), F32),
            pltpu.VMEM((s, LANES), F32),
            pltpu.VMEM((s, LANES), F32),
        ],
        compiler_params=_params(("parallel", "parallel")),
        name="dilated_attention",
    )(slopes, qkv, qkv, qkv)


def _even_out_kernel(x_ref, f_ref, a_ref, w_ref, o_ref):
    acc = x_ref[0]
    acc = acc + jnp.dot(f_ref[...], w_ref[0:FNET_WIDTH, :], preferred_element_type=F32)
    acc = acc + jnp.dot(a_ref[0], w_ref[FNET_WIDTH:, :], preferred_element_type=F32)
    o_ref[0] = acc


def _even_out(x, fmix, att, w_out, tm=512):
    b, s, d = x.shape
    return pl.pallas_call(
        _even_out_kernel,
        grid=(b, s // tm),
        in_specs=[
            pl.BlockSpec((1, tm, d), lambda bi, i: (bi, i, 0)),
            pl.BlockSpec((tm, FNET_WIDTH), lambda bi, i: (i, bi)),
            pl.BlockSpec((1, tm, ATT_WIDTH), lambda bi, i: (bi, i, 0)),
            pl.BlockSpec((d, d), lambda bi, i: (0, 0)),
        ],
        out_specs=pl.BlockSpec((1, tm, d), lambda bi, i: (bi, i, 0)),
        out_shape=jax.ShapeDtypeStruct((b, s, d), F32),
        compiler_params=_params(("parallel", "parallel")),
        name="even_out_proj",
    )(x, fmix, att, w_out)


def _resident(shape):
    return pl.BlockSpec(shape, lambda *_: (0,) * len(shape), pipeline_mode=pl.Buffered(1))


def _swiglu_kernel(x_ref, ln_ref, wg_ref, wu_ref, wd_ref, o_ref):
    x = x_ref[...]
    h = _rms(x, ln_ref[...]).astype(BF16)
    g = jnp.dot(h, wg_ref[...], preferred_element_type=F32)
    u = jnp.dot(h, wu_ref[...], preferred_element_type=F32)
    act = (_silu(g) * u).astype(BF16)
    o_ref[...] = x + jnp.dot(act, wd_ref[...], preferred_element_type=F32)


def _swiglu(x, ln, wg, wu, wd, tm=512):
    n, d = x.shape
    dff = wg.shape[1]
    return pl.pallas_call(
        _swiglu_kernel,
        grid=(n // tm,),
        in_specs=[
            pl.BlockSpec((tm, d), lambda i: (i, 0)),
            _resident((1, d)), _resident((d, dff)), _resident((d, dff)), _resident((dff, d)),
        ],
        out_specs=pl.BlockSpec((tm, d), lambda i: (i, 0)),
        out_shape=jax.ShapeDtypeStruct((n, d), F32),
        compiler_params=_params(("parallel",)),
        name="swiglu_ffn",
    )(x, ln, wg, wu, wd)


ODD_IN_COLS = 2048


def _odd_in_kernel(x_ref, ln_ref, w_ref, o_ref):
    h = _rms(x_ref[...], ln_ref[...]).astype(BF16)
    for c0 in range(0, ODD_IN, ODD_IN_COLS):
        cols = slice(c0, c0 + ODD_IN_COLS)
        o_ref[:, cols] = jnp.dot(h, w_ref[:, cols], preferred_element_type=F32).astype(BF16)


def _odd_in(x, ln, w_in, tm=512):
    n, d = x.shape
    return pl.pallas_call(
        _odd_in_kernel,
        grid=(n // tm,),
        in_specs=[pl.BlockSpec((tm, d), lambda i: (i, 0)), _resident((1, d)), _resident((d, ODD_IN))],
        out_specs=pl.BlockSpec((tm, ODD_IN), lambda i: (i, 0)),
        out_shape=jax.ShapeDtypeStruct((n, ODD_IN), BF16),
        compiler_params=_params(("parallel",)),
        name="odd_in_proj",
    )(x, ln, w_in)


RET_CHUNK = 256
RET_TILE = 1024


def _chunk_positions():
    return lax.broadcasted_iota(jnp.int32, (RET_CHUNK, 1), 0).astype(F32)


def _kt_v(k_scaled, v):
    return lax.dot_general(k_scaled, v, (((0,), (0,)), ((), ())), preferred_element_type=F32)


def _ret_bwd_state_kernel(lgb_ref, k_ref, v_ref, sb_ref, state):
    head = pl.program_id(1)

    @pl.when(pl.program_id(2) == 0)
    def _():
        state[...] = jnp.zeros_like(state)

    lg = lgb_ref[head]
    key_decay = jnp.exp(lg * _chunk_positions())
    chunk_decay = jnp.exp(jnp.full((1, RET_DV), lg * RET_CHUNK, F32))
    for c in reversed(range(RET_TILE // RET_CHUNK)):
        rows = slice(c * RET_CHUNK, (c + 1) * RET_CHUNK)
        sb_ref[0, 0, c] = state[...].astype(BF16)
        kd = (k_ref[0, rows, :].astype(F32) * key_decay).astype(BF16)
        state[...] = state[...] * chunk_decay + _kt_v(kd, v_ref[0, rows, :])


def _ret_bwd_states(z, lgb):
    b, s, _ = z.shape
    n = s // RET_CHUNK
    n_tiles = s // RET_TILE
    per_tile = RET_TILE // RET_CHUNK
    k_off = RET_QK_WIDTH // RET_DK
    v_off = 2 * RET_QK_WIDTH // RET_DV
    return pl.pallas_call(
        _ret_bwd_state_kernel,
        grid=(b, RET_HEADS, n_tiles),
        in_specs=[
            pl.BlockSpec(memory_space=pltpu.SMEM),
            pl.BlockSpec((1, RET_TILE, RET_DK), lambda bi, h, t: (bi, n_tiles - 1 - t, k_off + h)),
            pl.BlockSpec((1, RET_TILE, RET_DV), lambda bi, h, t: (bi, n_tiles - 1 - t, v_off + h)),
        ],
        out_specs=pl.BlockSpec((1, 1, per_tile, RET_DK, RET_DV), lambda bi, h, t: (bi, h, n_tiles - 1 - t, 0, 0)),
        out_shape=jax.ShapeDtypeStruct((b, RET_HEADS, n, RET_DK, RET_DV), BF16),
        scratch_shapes=[pltpu.VMEM((RET_DK, RET_DV), F32)],
        compiler_params=_params(("parallel", "parallel", "arbitrary")),
        name="retention_bwd_states",
    )(lgb, z, z)


def _ret_main_kernel(lgf_ref, lgb_ref, q_ref, k_ref, v_ref, g_ref, sb_ref, gn_ref, o_ref, state):
    head = pl.program_id(1)

    @pl.when(pl.program_id(2) == 0)
    def _():
        state[...] = jnp.zeros_like(state)

    lgf = lgf_ref[head]
    lgb = lgb_ref[head]
    c = RET_CHUNK
    pos = _chunk_positions()
    rel = (lax.broadcasted_iota(jnp.int32, (c, c), 0) - lax.broadcasted_iota(jnp.int32, (c, c), 1)).astype(F32)
    decay = jnp.exp(jnp.where(rel >= 0, lgf * rel, -lgb * rel))
    q_decay_f = jnp.exp(lgf * (pos + 1.0))
    q_decay_b = jnp.exp(lgb * (c - pos))
    k_decay_f = jnp.exp(lgf * (c - 1.0 - pos))
    chunk_decay = jnp.exp(jnp.full((1, RET_DV), lgf * c, F32))
    gn = gn_ref[...]

    for ci in range(RET_TILE // RET_CHUNK):
        rows = slice(ci * c, (ci + 1) * c)
        q = q_ref[0, rows, :]
        k = k_ref[0, rows, :]
        v = v_ref[0, rows, :]
        qf32 = q.astype(F32)
        sc = lax.dot_general(q, k, (((1,), (1,)), ((), ())), preferred_element_type=F32)
        y = jnp.dot((sc * decay).astype(BF16), v, preferred_element_type=F32)
        y = y + jnp.dot((qf32 * q_decay_f).astype(BF16), state[...].astype(BF16), preferred_element_type=F32)
        y = y + jnp.dot((qf32 * q_decay_b).astype(BF16), sb_ref[0, 0, ci], preferred_element_type=F32)
        kd = (k.astype(F32) * k_decay_f).astype(BF16)
        state[...] = state[...] * chunk_decay + _kt_v(kd, v)
        yn = _rms(y, gn)
        o_ref[0, rows, :] = (_silu(g_ref[0, rows, :].astype(F32)) * yn).astype(BF16)


def _ret_main(z, sb, lgf, lgb, gn):
    b, s, _ = z.shape
    n_tiles = s // RET_TILE
    per_tile = RET_TILE // RET_CHUNK
    k_off = RET_QK_WIDTH // RET_DK
    v_off = 2 * RET_QK_WIDTH // RET_DV
    g_off = (2 * RET_QK_WIDTH + RET_V_WIDTH) // RET_DV
    smem = pl.BlockSpec(memory_space=pltpu.SMEM)
    return pl.pallas_call(
        _ret_main_kernel,
        grid=(b, RET_HEADS, n_tiles),
        in_specs=[
            smem, smem,
            pl.BlockSpec((1, RET_TILE, RET_DK), lambda bi, h, t: (bi, t, h)),
            pl.BlockSpec((1, RET_TILE, RET_DK), lambda bi, h, t: (bi, t, k_off + h)),
            pl.BlockSpec((1, RET_TILE, RET_DV), lambda bi, h, t: (bi, t, v_off + h)),
            pl.BlockSpec((1, RET_TILE, RET_DV), lambda bi, h, t: (bi, t, g_off + h)),
            pl.BlockSpec((1, 1, per_tile, RET_DK, RET_DV), lambda bi, h, t: (bi, h, t, 0, 0)),
            pl.BlockSpec((1, RET_DV), lambda bi, h, t: (0, h)),
        ],
        out_specs=pl.BlockSpec((1, RET_TILE, RET_DV), lambda bi, h, t: (bi, t, h)),
        out_shape=jax.ShapeDtypeStruct((b, s, RET_V_WIDTH), BF16),
        scratch_shapes=[pltpu.VMEM((RET_DK, RET_DV), F32)],
        compiler_params=_params(("parallel", "parallel", "arbitrary")),
        name="retention_main",
    )(lgf, lgb, z, z, z, z, sb, gn)


def _odd_out_router_kernel(y_ref, x_ref, w_ref, ln_ref, wr_ref, xo_ref, h_ref, idx_ref, gate_ref):
    x = x_ref[...] + jnp.dot(y_ref[...], w_ref[...], preferred_element_type=F32)
    xo_ref[...] = x
    hn = _rms(x, ln_ref[...])
    h_ref[...] = hn.astype(BF16)
    logits = lax.dot_general(wr_ref[...], hn, (((1,), (1,)), ((), ())),
                             precision=lax.Precision.HIGHEST, preferred_element_type=F32)
    row = lax.broadcasted_iota(jnp.int32, logits.shape, 0).astype(F32)
    none = float(N_EXPERTS)
    m1 = jnp.max(logits, axis=0, keepdims=True)
    i1 = jnp.min(jnp.where(logits == m1, row, none), axis=0, keepdims=True)
    rest = jnp.where(row == i1, -jnp.inf, logits)
    m2 = jnp.max(rest, axis=0, keepdims=True)
    i2 = jnp.min(jnp.where(rest == m2, row, none), axis=0, keepdims=True)
    e2 = jnp.exp(m2 - m1)
    idx_ref[0:1, :] = i1.astype(jnp.int32)
    idx_ref[1:2, :] = i2.astype(jnp.int32)
    gate_ref[0:1, :] = 1.0 / (1.0 + e2)
    gate_ref[1:2, :] = e2 / (1.0 + e2)


def _odd_out_router(y, x, w_out, ln, wr_t, tm=512):
    n, d = x.shape
    full = lambda shp: pl.BlockSpec(shp, lambda i: (0,) * len(shp))
    return pl.pallas_call(
        _odd_out_router_kernel,
        grid=(n // tm,),
        in_specs=[
            pl.BlockSpec((tm, RET_V_WIDTH), lambda i: (i, 0)),
            pl.BlockSpec((tm, d), lambda i: (i, 0)),
            full((RET_V_WIDTH, d)), full((1, d)), full((N_EXPERTS, d)),
        ],
        out_specs=[
            pl.BlockSpec((tm, d), lambda i: (i, 0)),
            pl.BlockSpec((tm, d), lambda i: (i, 0)),
            pl.BlockSpec((2, tm), lambda i: (0, i)),
            pl.BlockSpec((2, tm), lambda i: (0, i)),
        ],
        out_shape=[
            jax.ShapeDtypeStruct((n, d), F32),
            jax.ShapeDtypeStruct((n, d), BF16),
            jax.ShapeDtypeStruct((2, n), jnp.int32),
            jax.ShapeDtypeStruct((2, n), F32),
        ],
        compiler_params=_params(("parallel",)),
        name="odd_out_proj_router",
    )(y, x, w_out, ln, wr_t)


MOE_TM = 1024
MOE_TF = 1792
MOE_SUB = 512


def _moe_kernel(tile_expert_ref, n_used_ref, xs_ref, gate_ref, w1_ref, w3_ref, w2_ref, o_ref, acc_ref):
    t = pl.program_id(0)
    f = pl.program_id(1)
    last = pl.num_programs(1) - 1
    used = t < n_used_ref[0]

    @pl.when(used)
    def _():
        for r0 in range(0, MOE_TM, MOE_SUB):
            rows = slice(r0, r0 + MOE_SUB)
            xs = xs_ref[rows, :]
            g = jnp.dot(xs, w1_ref[...], preferred_element_type=F32)
            u = jnp.dot(xs, w3_ref[...], preferred_element_type=F32)
            act = (_silu(g) * u).astype(BF16)
            part = jnp.dot(act, w2_ref[...], preferred_element_type=F32)

            @pl.when(f == 0)
            def _():
                acc_ref[rows, :] = part

            @pl.when(jnp.logical_and(f > 0, f < last))
            def _():
                acc_ref[rows, :] += part

            @pl.when(f == last)
            def _():
                o_ref[rows, :] = ((acc_ref[rows, :] + part) * gate_ref[rows, 0:1]).astype(BF16)

    @pl.when(jnp.logical_and(jnp.logical_not(used), f == last))
    def _():
        o_ref[...] = jnp.zeros_like(o_ref)


def _moe_experts(xs, gate_rows, tile_expert, n_used, w1, w3, w2):
    p, d = xs.shape
    n_tiles = p // MOE_TM
    n_f = D_FF_EXPERT // MOE_TF
    assert n_f >= 2 and n_f * MOE_TF == D_FF_EXPERT

    def row_tile(t, f, te, nu):
        return (jnp.minimum(t, nu[0] - 1), 0)

    def f_idx(t, f, nu):
        return jnp.where(t < nu[0], f, n_f - 1)

    grid_spec = pltpu.PrefetchScalarGridSpec(
        num_scalar_prefetch=2,
        grid=(n_tiles, n_f),
        in_specs=[
            pl.BlockSpec((MOE_TM, d), row_tile),
            pl.BlockSpec((MOE_TM, LANES), row_tile),
            pl.BlockSpec((None, d, MOE_TF), lambda t, f, te, nu: (te[t], 0, f_idx(t, f, nu))),
            pl.BlockSpec((None, d, MOE_TF), lambda t, f, te, nu: (te[t], 0, f_idx(t, f, nu))),
            pl.BlockSpec((None, MOE_TF, d), lambda t, f, te, nu: (te[t], f_idx(t, f, nu), 0)),
        ],
        out_specs=pl.BlockSpec((MOE_TM, d), lambda t, f, te, nu: (t, 0)),
        scratch_shapes=[pltpu.VMEM((MOE_TM, d), F32)],
    )
    return pl.pallas_call(
        _moe_kernel,
        grid_spec=grid_spec,
        out_shape=jax.ShapeDtypeStruct((p, d), BF16),
        compiler_params=_params(("arbitrary", "arbitrary")),
        name="moe_experts",
    )(tile_expert, n_used, xs, gate_rows, w1, w3, w2)


def _combine_kernel(x_ref, y_ref, o_ref):
    o_ref[...] = x_ref[...] + y_ref[0].astype(F32) + y_ref[1].astype(F32)


def _combine(x, y2, tm=1024):
    n, d = x.shape
    return pl.pallas_call(
        _combine_kernel,
        grid=(n // tm,),
        in_specs=[pl.BlockSpec((tm, d), lambda i: (i, 0)), pl.BlockSpec((2, tm, d), lambda i: (0, i, 0))],
        out_specs=pl.BlockSpec((tm, d), lambda i: (i, 0)),
        out_shape=jax.ShapeDtypeStruct((n, d), F32),
        compiler_params=_params(("parallel",)),
        name="moe_combine",
    )(x, y2)


def _route(idx, gates, n):
    e_flat = idx.reshape(-1)
    n_pairs = e_flat.shape[0]
    pair_id = jnp.arange(n_pairs, dtype=jnp.int32)
    order = (jnp.sort(e_flat * n_pairs + pair_id) % n_pairs).astype(jnp.int32)
    rank = jnp.argsort(order).astype(jnp.int32)
    counts = jnp.sum(e_flat[None, :] == jnp.arange(N_EXPERTS, dtype=jnp.int32)[:, None], axis=1).astype(jnp.int32)
    starts = jnp.cumsum(counts) - counts
    tiles_per_expert = (counts + MOE_TM - 1) // MOE_TM
    tile_ends = jnp.cumsum(tiles_per_expert)
    padded_starts = (tile_ends - tiles_per_expert) * MOE_TM
    dest = rank + (padded_starts - starts)[e_flat]
    n_rows = n_pairs + N_EXPERTS * MOE_TM
    n_tiles = n_rows // MOE_TM
    tile_expert = jnp.minimum(
        jnp.sum(jnp.arange(n_tiles, dtype=jnp.int32)[:, None] >= tile_ends[None, :], axis=1), N_EXPERTS - 1
    ).astype(jnp.int32)
    row_expert = jnp.repeat(tile_expert, MOE_TM)
    row_in_expert = jnp.arange(n_rows, dtype=jnp.int32) - padded_starts[row_expert]
    row_valid = row_in_expert < counts[row_expert]
    row_pair = order[jnp.clip(starts[row_expert] + row_in_expert, 0, n_pairs - 1)]
    src_token = jnp.where(row_valid, row_pair % n, 0).astype(jnp.int32)
    row_gate = jnp.where(row_valid, gates.reshape(-1)[row_pair], 0.0)
    n_used = tile_ends[-1:].astype(jnp.int32)
    return src_token, row_gate, dest, tile_expert, n_used


def _channel_dft():
    c = jnp.arange(FNET_GROUP_DIM, dtype=jnp.int32)
    ang = ((c[:, None] * c[None, :]) % FNET_GROUP_DIM).astype(F32) * (2.0 * math.pi / FNET_GROUP_DIM)
    eye = jnp.eye(FNET_GROUPS, dtype=F32)
    scale = FNET_GROUP_DIM ** -0.5
    return jnp.concatenate([jnp.kron(eye, jnp.cos(ang)) * scale, -jnp.kron(eye, jnp.sin(ang)) * scale], axis=1).astype(BF16)


def _sequence_dft(s):
    hi = s // 64
    k = jnp.arange(s, dtype=jnp.int32)
    j_hi = jnp.arange(hi, dtype=jnp.int32)
    j_lo = jnp.arange(64, dtype=jnp.int32)
    w = 2.0 * math.pi / s
    ang_a = (((j_hi[:, None] * k[None, :]) % hi) * 64).astype(F32) * w
    ang_b = ((j_lo[:, None] * k[None, :]) % s).astype(F32) * w
    scale = s ** -0.5
    ca, sa = jnp.cos(ang_a) * scale, jnp.sin(ang_a) * scale
    cb, sb = jnp.cos(ang_b), jnp.sin(ang_b)
    ca2 = jnp.concatenate([ca, ca], axis=1)[:, None, :]
    sa2 = jnp.concatenate([sa, sa], axis=1)[:, None, :]
    p = jnp.concatenate([cb, sb], axis=1)[None, :, :]
    q = jnp.concatenate([sb, -cb], axis=1)[None, :, :]
    return (ca2 * p - sa2 * q).reshape(s, 2 * s).astype(BF16)


def _trunk(x, p):
    b, s, d = x.shape
    n = b * s
    y, qkv = _even_in(x, p["ln_mix_e"], p["w_in_e"], p["chan_dft"], p["head_sum"], p["qn"], p["kn"])
    fmix = _matmul(_sequence_dft(s), y.reshape(2 * s, b * FNET_WIDTH))
    att = _attention(qkv, p["slopes"])
    x = _even_out(x, fmix, att, p["w_out_e"])
    x = _swiglu(x.reshape(n, d), p["ln_ffn_e"], p["w_gate_e"], p["w_up_e"], p["w_down_e"])
    z = _odd_in(x, p["ln_mix_o"], p["w_in_o"]).reshape(b, s, ODD_IN)
    sb = _ret_bwd_states(z, p["lgb"])
    yr = _ret_main(z, sb, p["lgf"], p["lgb"], p["ret_gn"])
    x, h, idx, gates = _odd_out_router(yr.reshape(n, RET_V_WIDTH), x, p["w_out_o"], p["ln_ffn_o"], p["w_router_t"])
    src_token, row_gate, dest, tile_expert, n_used = _route(idx, gates, n)
    xs = h.at[src_token].get(mode="promise_in_bounds")
    gate_rows = jnp.broadcast_to(row_gate[:, None], (row_gate.shape[0], LANES))
    ye = _moe_experts(xs, gate_rows, tile_expert, n_used, p["moe_w1"], p["moe_w3"], p["moe_w2"])
    y2 = ye.at[dest].get(mode="promise_in_bounds").reshape(2, n, d)
    return _combine(x, y2).reshape(b, s, d)


def kernel(x_prompt, x_sample, ln_mix_e, w_in_e, w_out_e, qn_e, kn_e, ln_ffn_e, w_gate_e, w_up_e, w_down_e, ln_mix_o, w_in_o, w_out_o, logdecay_fwd, logdecay_bwd, ret_gn, ln_ffn_o, w_router, moe_w1, moe_w3, moe_w2):
    row = lambda w: w.reshape(1, -1).astype(F32)
    k_scale = jnp.concatenate([
        jnp.ones((RET_QK_WIDTH,), F32), jnp.full((RET_QK_WIDTH,), RET_DK ** -0.5, F32),
        jnp.ones((2 * RET_V_WIDTH,), F32)])
    head_id = jnp.arange(LANES, dtype=jnp.int32) // ATT_HEAD_DIM
    p = {
        "ln_mix_e": row(ln_mix_e[0]),
        "w_in_e": w_in_e[0].astype(BF16),
        "w_out_e": w_out_e[0].astype(BF16),
        "qn": row(jnp.tile(qn_e[0], 2) * (ATT_HEAD_DIM ** -0.5)),
        "kn": row(jnp.tile(kn_e[0], 2)),
        "ln_ffn_e": row(ln_ffn_e[0]),
        "w_gate_e": w_gate_e[0].astype(BF16),
        "w_up_e": w_up_e[0].astype(BF16),
        "w_down_e": w_down_e[0].astype(BF16),
        "ln_mix_o": row(ln_mix_o[0]),
        "w_in_o": (w_in_o[0] * k_scale[None, :]).astype(BF16),
        "w_out_o": w_out_o[0].astype(BF16),
        "lgf": logdecay_fwd[0].astype(F32),
        "lgb": logdecay_bwd[0].astype(F32),
        "ret_gn": row(ret_gn[0]),
        "ln_ffn_o": row(ln_ffn_o[0]),
        "w_router_t": w_router[0].T.astype(F32),
        "moe_w1": moe_w1[0].astype(BF16),
        "moe_w3": moe_w3[0].astype(BF16),
        "moe_w2": moe_w2[0].astype(BF16),
        "chan_dft": _channel_dft(),
        "head_sum": (head_id[:, None] == head_id[None, :]).astype(BF16),
        "slopes": jnp.exp2(-8.0 * (jnp.arange(ATT_HEADS, dtype=F32) + 1.0) / ATT_HEADS),
    }
    return (_trunk(x_prompt, p), _trunk(x_sample, p))
```

```python
import functools
import math

import jax
import jax.numpy as jnp
from jax import lax
from jax.experimental import pallas as pl
from jax.experimental.pallas import tpu as pltpu

D_MODEL = 1024
FNET_GROUPS = 4
FNET_GROUP_DIM = 64
FNET_WIDTH = 256
ATT_HEAD_DIM = 64
ATT_HEADS = 12
ATT_WIDTH = 768
ATT_HALF_WINDOW = 64
DILATIONS = (1, 4, 16)
EVEN_IN = FNET_WIDTH + 3 * ATT_WIDTH
RET_HEADS = 4
RET_DK = 256
RET_DV = 512
RET_QK_WIDTH = 1024
RET_V_WIDTH = 2048
ODD_IN = 6144
D_FF = 2816
N_EXPERTS = 8
D_FF_EXPERT = 3584
EPS = 1e-6
NEG = -1e30

LANES = 128
VMEM_LIMIT_BYTES = 56 * 2**20

F32 = jnp.float32
BF16 = jnp.bfloat16


def _params(semantics):
    return pltpu.CompilerParams(dimension_semantics=semantics, vmem_limit_bytes=VMEM_LIMIT_BYTES)


def _rms(x, w):
    return x * lax.rsqrt(jnp.mean(x * x, axis=-1, keepdims=True) + EPS) * w


def _silu(x):
    return x / (1.0 + jnp.exp(-x))


def _even_in_kernel(x_ref, ln_ref, w_ref, cd_ref, hs_ref, qn_ref, kn_ref, y_ref, qkv_ref):
    h = _rms(x_ref[0], ln_ref[...]).astype(BF16)
    u = jnp.dot(h, w_ref[:, 0:FNET_WIDTH], preferred_element_type=F32).astype(BF16)
    yy = jnp.dot(u, cd_ref[...], preferred_element_type=F32)
    y_ref[0] = yy[:, :FNET_WIDTH].astype(BF16)
    y_ref[1] = yy[:, FNET_WIDTH:].astype(BF16)
    for part, n_ref in ((0, qn_ref), (1, kn_ref)):
        c0 = FNET_WIDTH + part * ATT_WIDTH
        z = jnp.dot(h, w_ref[:, c0:c0 + ATT_WIDTH], preferred_element_type=F32)
        for j in range(ATT_WIDTH // LANES):
            zj = z[:, j * LANES:(j + 1) * LANES]
            ss = jnp.dot((zj * zj).astype(BF16), hs_ref[...], preferred_element_type=F32)
            zn = zj * lax.rsqrt(ss * (1.0 / ATT_HEAD_DIM) + EPS) * n_ref[...]
            qkv_ref[0, :, part * ATT_WIDTH + j * LANES:part * ATT_WIDTH + (j + 1) * LANES] = zn.astype(BF16)
    c0 = FNET_WIDTH + 2 * ATT_WIDTH
    v = jnp.dot(h, w_ref[:, c0:c0 + ATT_WIDTH], preferred_element_type=F32)
    qkv_ref[0, :, 2 * ATT_WIDTH:3 * ATT_WIDTH] = v.astype(BF16)


def _even_in(x, ln, w_in, cd, hs, qn, kn, tm=512):
    b, s, d = x.shape
    full = lambda shp: pl.BlockSpec(shp, lambda bi, i: (0,) * len(shp))
    return pl.pallas_call(
        _even_in_kernel,
        grid=(b, s // tm),
        in_specs=[
            pl.BlockSpec((1, tm, d), lambda bi, i: (bi, i, 0)),
            full((1, d)), full((d, EVEN_IN)), full((FNET_WIDTH, 2 * FNET_WIDTH)), full((LANES, LANES)),
            full((1, LANES)), full((1, LANES)),
        ],
        out_specs=[
            pl.BlockSpec((2, tm, FNET_WIDTH), lambda bi, i: (0, i, bi)),
            pl.BlockSpec((1, tm, 3 * ATT_WIDTH), lambda bi, i: (bi, i, 0)),
        ],
        out_shape=[
            jax.ShapeDtypeStruct((2, s, b * FNET_WIDTH), BF16),
            jax.ShapeDtypeStruct((b, s, 3 * ATT_WIDTH), BF16),
        ],
        compiler_params=_params(("parallel", "parallel")),
        name="even_in_proj",
    )(x, ln, w_in, cd, hs, qn, kn)


def _matmul_kernel(a_ref, b_ref, o_ref, acc_ref):
    @pl.when(pl.program_id(2) == 0)
    def _():
        acc_ref[...] = jnp.zeros_like(acc_ref)

    acc_ref[...] += jnp.dot(a_ref[...], b_ref[...], preferred_element_type=F32)

    @pl.when(pl.program_id(2) == pl.num_programs(2) - 1)
    def _():
        o_ref[...] = acc_ref[...].astype(o_ref.dtype)


def _matmul(a, b, tm=1024, tn=2048, tk=1024):
    m, k = a.shape
    _, n = b.shape
    tm, tn, tk = min(tm, m), min(tn, n), min(tk, k)
    return pl.pallas_call(
        _matmul_kernel,
        grid=(m // tm, n // tn, k // tk),
        in_specs=[pl.BlockSpec((tm, tk), lambda i, j, kk: (i, kk)),
                  pl.BlockSpec((tk, tn), lambda i, j, kk: (kk, j))],
        out_specs=pl.BlockSpec((tm, tn), lambda i, j, kk: (i, j)),
        out_shape=jax.ShapeDtypeStruct((m, n), BF16),
        scratch_shapes=[pltpu.VMEM((tm, tn), F32)],
        compiler_params=_params(("parallel", "parallel", "arbitrary")),
        name="seq_dft_matmul",
    )(a, b)


ATT_QBLK = 128
ATT_CONV_ROWS = 512
ATT_GROUP = 16
ATT_MAX_DILATION = 16


def _attn_kernel(slopes_ref, q_ref, k_ref, v_ref, o_ref, xf, qs, kd, vd, bias, sc_buf, p_buf, m_buf, m_s, l_s, a_s,
                 *, seq):
    pair = pl.program_id(1)
    lane = lax.broadcasted_iota(jnp.int32, (1, LANES), 1)
    first_head = lane < ATT_HEAD_DIM
    slopes = (slopes_ref[2 * pair], slopes_ref[2 * pair + 1])
    base_len = seq // ATT_MAX_DILATION

    def conv_rows(c):
        return pl.ds(pl.multiple_of(c * ATT_CONV_ROWS, ATT_CONV_ROWS), ATT_CONV_ROWS)

    def widen(src_ref):
        def body(c, carry):
            xf[conv_rows(c), :] = src_ref[0, conv_rows(c), :].astype(F32)
            return carry

        lax.fori_loop(0, seq // ATT_CONV_ROWS, body, 0)

    def store_split(dst_ref, rows, x, fill):
        other = jnp.full_like(x, fill)
        dst_ref[0, rows, :] = jnp.where(first_head, x, other)
        dst_ref[1, rows, :] = jnp.where(first_head, other, x)

    def class_major_rows(t, r):
        blocks_per_class = seq // r // ATT_QBLK
        rho = t // blocks_per_class
        l0 = (t % blocks_per_class) * ATT_QBLK
        return pl.ds(rho + r * l0, ATT_QBLK, stride=r)

    widen(q_ref)

    def q_gather(t, carry):
        qs[pl.ds(pl.multiple_of(t * ATT_QBLK, ATT_QBLK), ATT_QBLK), :] = xf[class_major_rows(t, ATT_MAX_DILATION), :]
        return carry

    lax.fori_loop(0, seq // ATT_QBLK, q_gather, 0)

    for pattern, r in enumerate(DILATIONS):
        sub_len = seq // r
        n_keys = min(2 * ATT_QBLK, sub_len)
        blocks_per_class = sub_len // ATT_QBLK
        n_runs = ATT_MAX_DILATION // r
        run_len = ATT_QBLK // n_runs

        n_kruns = 1 if r == 1 else n_runs
        krun_len = n_keys // n_kruns
        if r == 1:
            def copy(c, carry):
                store_split(kd, conv_rows(c), k_ref[0, conv_rows(c), :], 0.0)
                store_split(vd, conv_rows(c), v_ref[0, conv_rows(c), :], 1.0)
                return carry

            lax.fori_loop(0, seq // ATT_CONV_ROWS, copy, 0)
        elif pattern == 1:
            for src_ref, dst_ref, fill in ((k_ref, kd, 0.0), (v_ref, vd, 1.0)):
                widen(src_ref)

                def gather(t, carry, dst_ref=dst_ref, fill=fill):
                    rows = pl.ds(pl.multiple_of(t * ATT_QBLK, ATT_QBLK), ATT_QBLK)
                    store_split(dst_ref, rows, xf[class_major_rows(t, ATT_MAX_DILATION), :].astype(BF16), fill)
                    return carry

                lax.fori_loop(0, seq // ATT_QBLK, gather, 0)

        n_idx = lax.broadcasted_iota(jnp.int32, (ATT_QBLK, n_keys), 0)
        q_rel = n_runs * (n_idx % run_len) + n_idx // run_len
        c_idx = lax.broadcasted_iota(jnp.int32, (ATT_QBLK, n_keys), 1)
        k_rel = n_kruns * (c_idx % krun_len) + c_idx // krun_len
        offsets = (0, ATT_HALF_WINDOW, 2 * ATT_HALF_WINDOW) if blocks_per_class > 1 else (0,)
        for variant, off in enumerate(offsets):
            dist = jnp.abs(k_rel - off - q_rel)
            for h in range(2):
                scaled = (-slopes[h] * float(r)) * dist.astype(F32)
                bias[h, variant, :, 0:n_keys] = jnp.where(dist <= ATT_HALF_WINDOW, scaled, NEG)

        def geometry(t, r=r, blocks_per_class=blocks_per_class, n_runs=n_runs, run_len=run_len,
                     n_kruns=n_kruns, krun_len=krun_len, n_keys=n_keys):
            rho = t // blocks_per_class
            lb = t % blocks_per_class
            if blocks_per_class > 1:
                variant = jnp.where(lb == 0, 0, jnp.where(lb == blocks_per_class - 1, 2, 1))
            else:
                variant = 0
            runs = [pl.ds(pl.multiple_of((rho + r * j) * base_len + lb * run_len, 8), run_len)
                    for j in range(n_runs)]
            if r == 1:
                k_runs = [pl.ds(pl.multiple_of(t * ATT_QBLK - variant * ATT_HALF_WINDOW, ATT_HALF_WINDOW), n_keys)]
            else:
                start = lb * run_len - variant * (ATT_HALF_WINDOW // n_kruns)
                k_runs = [pl.ds(pl.multiple_of((rho + r * j) * base_len + start, 16), krun_len)
                          for j in range(n_kruns)]
            return variant, k_runs, runs

        def load_runs(ref, runs):
            return jnp.concatenate([ref[rows, :] for rows in runs], axis=0) if len(runs) > 1 else ref[runs[0], :]

        def store_runs(ref, runs, x, run_len=run_len):
            for j, rows in enumerate(runs):
                ref[rows, :] = x[j * run_len:(j + 1) * run_len, :]

        def group(g, carry, n_keys=n_keys, pattern=pattern, geometry=geometry, load_runs=load_runs,
                  store_runs=store_runs):
            for i in range(ATT_GROUP):
                variant, k_runs, runs = geometry(g * ATT_GROUP + i)
                q = load_runs(qs, runs).astype(BF16)
                for h in range(2):
                    sc = lax.dot_general(q, load_runs(kd.at[h], k_runs), (((1,), (1,)), ((), ())),
                                         preferred_element_type=F32)
                    sc_buf[i, h, :, 0:n_keys] = sc + bias[h, variant, :, 0:n_keys]
            for i in range(ATT_GROUP):
                ms = []
                for h in range(2):
                    sc = sc_buf[i, h, :, 0:n_keys]
                    m = jnp.max(sc, axis=-1, keepdims=True)
                    p_buf[i, h, :, 0:n_keys] = jnp.exp(sc - m).astype(BF16)
                    ms.append(m)
                m_buf[i] = jnp.where(first_head, ms[0], ms[1])
            for i in range(ATT_GROUP):
                _, k_runs, runs = geometry(g * ATT_GROUP + i)
                pv0 = jnp.dot(p_buf[i, 0, :, 0:n_keys], load_runs(vd.at[0], k_runs), preferred_element_type=F32)
                pv1 = jnp.dot(p_buf[i, 1, :, 0:n_keys], load_runs(vd.at[1], k_runs), preferred_element_type=F32)
                a_new = jnp.where(first_head, pv0, pv1)
                l_new = pltpu.roll(jnp.where(first_head, pv1, pv0), ATT_HEAD_DIM, 1)
                m_new = m_buf[i]
                if pattern == 0:
                    store_runs(m_s, runs, m_new)
                    store_runs(l_s, runs, l_new)
                    store_runs(a_s, runs, a_new)
                else:
                    m_old = load_runs(m_s, runs)
                    m_tot = jnp.maximum(m_old, m_new)
                    w_old = jnp.exp(m_old - m_tot)
                    w_new = jnp.exp(m_new - m_tot)
                    store_runs(m_s, runs, m_tot)
                    store_runs(l_s, runs, w_old * load_runs(l_s, runs) + w_new * l_new)
                    store_runs(a_s, runs, w_old * load_runs(a_s, runs) + w_new * a_new)
            return carry

        lax.fori_loop(0, seq // (ATT_QBLK * ATT_GROUP), group, 0)

    def scatter(t, carry):
        rows = pl.ds(pl.multiple_of(t * ATT_QBLK, ATT_QBLK), ATT_QBLK)
        xf[class_major_rows(t, ATT_MAX_DILATION), :] = a_s[rows, :] / l_s[rows, :]
        return carry

    lax.fori_loop(0, seq // ATT_QBLK, scatter, 0)

    def finish(c, carry):
        o_ref[0, conv_rows(c), :] = xf[conv_rows(c), :].astype(BF16)
        return carry

    lax.fori_loop(0, seq // ATT_CONV_ROWS, finish, 0)


def _attention(qkv, slopes):
    b, s, _ = qkv.shape
    n_pairs = ATT_WIDTH // LANES
    blk = lambda off: pl.BlockSpec((1, s, LANES), lambda bi, j, off=off: (bi, 0, off + j))
    return pl.pallas_call(
        functools.partial(_attn_kernel, seq=s),
        grid=(b, n_pairs),
        in_specs=[pl.BlockSpec(memory_space=pltpu.SMEM), blk(0), blk(n_pairs), blk(2 * n_pairs)],
        out_specs=pl.BlockSpec((1, s, LANES), lambda bi, j: (bi, 0, j)),
        out_shape=jax.ShapeDtypeStruct((b, s, ATT_WIDTH), BF16),
        scratch_shapes=[
            pltpu.VMEM((s, LANES), F32),
            pltpu.VMEM((s, LANES), F32),
            pltpu.VMEM((2, s, LANES), BF16),
            pltpu.VMEM((2, s, LANES), BF16),
            pltpu.VMEM((2, 3, ATT_QBLK, 2 * ATT_QBLK), F32),
            pltpu.VMEM((ATT_GROUP, 2, ATT_QBLK, 2 * ATT_QBLK), F32),
            pltpu.VMEM((ATT_GROUP, 2, ATT_QBLK, 2 * ATT_QBLK), BF16),
            pltpu.VMEM((ATT_GROUP, ATT_QBLK, LANES), F32),
            pltpu.VMEM((s, LANES), F32),
            pltpu.VMEM((s, LANES), F32),
            pltpu.VMEM((s, LANES), F32),
        ],
        compiler_params=_params(("parallel", "parallel")),
        name="dilated_attention",
    )(slopes, qkv, qkv, qkv)


def _even_out_kernel(x_ref, f_ref, a_ref, w_ref, o_ref):
    acc = x_ref[0]
    acc = acc + jnp.dot(f_ref[...], w_ref[0:FNET_WIDTH, :], preferred_element_type=F32)
    acc = acc + jnp.dot(a_ref[0], w_ref[FNET_WIDTH:, :], preferred_element_type=F32)
    o_ref[0] = acc


def _even_out(x, fmix, att, w_out, tm=512):
    b, s, d = x.shape
    return pl.pallas_call(
        _even_out_kernel,
        grid=(b, s // tm),
        in_specs=[
            pl.BlockSpec((1, tm, d), lambda bi, i: (bi, i, 0)),
            pl.BlockSpec((tm, FNET_WIDTH), lambda bi, i: (i, bi)),
            pl.BlockSpec((1, tm, ATT_WIDTH), lambda bi, i: (bi, i, 0)),
            pl.BlockSpec((d, d), lambda bi, i: (0, 0)),
        ],
        out_specs=pl.BlockSpec((1, tm, d), lambda bi, i: (bi, i, 0)),
        out_shape=jax.ShapeDtypeStruct((b, s, d), F32),
        compiler_params=_params(("parallel", "parallel")),
        name="even_out_proj",
    )(x, fmix, att, w_out)


def _resident(shape):
    return pl.BlockSpec(shape, lambda *_: (0,) * len(shape), pipeline_mode=pl.Buffered(1))


def _swiglu_kernel(x_ref, ln_ref, wg_ref, wu_ref, wd_ref, o_ref):
    x = x_ref[...]
    h = _rms(x, ln_ref[...]).astype(BF16)
    g = jnp.dot(h, wg_ref[...], preferred_element_type=F32)
    u = jnp.dot(h, wu_ref[...], preferred_element_type=F32)
    act = (_silu(g) * u).astype(BF16)
    o_ref[...] = x + jnp.dot(act, wd_ref[...], preferred_element_type=F32)


def _swiglu(x, ln, wg, wu, wd, tm=512):
    n, d = x.shape
    dff = wg.shape[1]
    return pl.pallas_call(
        _swiglu_kernel,
        grid=(n // tm,),
        in_specs=[
            pl.BlockSpec((tm, d), lambda i: (i, 0)),
            _resident((1, d)), _resident((d, dff)), _resident((d, dff)), _resident((dff, d)),
        ],
        out_specs=pl.BlockSpec((tm, d), lambda i: (i, 0)),
        out_shape=jax.ShapeDtypeStruct((n, d), F32),
        compiler_params=_params(("parallel",)),
        name="swiglu_ffn",
    )(x, ln, wg, wu, wd)


ODD_IN_COLS = 2048


def _odd_in_kernel(x_ref, ln_ref, w_ref, o_ref):
    h = _rms(x_ref[...], ln_ref[...]).astype(BF16)
    for c0 in range(0, ODD_IN, ODD_IN_COLS):
        cols = slice(c0, c0 + ODD_IN_COLS)
        o_ref[:, cols] = jnp.dot(h, w_ref[:, cols], preferred_element_type=F32).astype(BF16)


def _odd_in(x, ln, w_in, tm=512):
    n, d = x.shape
    return pl.pallas_call(
        _odd_in_kernel,
        grid=(n // tm,),
        in_specs=[pl.BlockSpec((tm, d), lambda i: (i, 0)), _resident((1, d)), _resident((d, ODD_IN))],
        out_specs=pl.BlockSpec((tm, ODD_IN), lambda i: (i, 0)),
        out_shape=jax.ShapeDtypeStruct((n, ODD_IN), BF16),
        compiler_params=_params(("parallel",)),
        name="odd_in_proj",
    )(x, ln, w_in)


RET_CHUNK = 256
RET_TILE = 1024


def _chunk_positions():
    return lax.broadcasted_iota(jnp.int32, (RET_CHUNK, 1), 0).astype(F32)


def _kt_v(k_scaled, v):
    return lax.dot_general(k_scaled, v, (((0,), (0,)), ((), ())), preferred_element_type=F32)


def _ret_bwd_state_kernel(lgb_ref, k_ref, v_ref, sb_ref, state):
    head = pl.program_id(1)

    @pl.when(pl.program_id(2) == 0)
    def _():
        state[...] = jnp.zeros_like(state)

    lg = lgb_ref[head]
    key_decay = jnp.exp(lg * _chunk_positions())
    chunk_decay = jnp.exp(jnp.full((1, RET_DV), lg * RET_CHUNK, F32))
    for c in reversed(range(RET_TILE // RET_CHUNK)):
        rows = slice(c * RET_CHUNK, (c + 1) * RET_CHUNK)
        sb_ref[0, 0, c] = state[...].astype(BF16)
        kd = (k_ref[0, rows, :].astype(F32) * key_decay).astype(BF16)
        state[...] = state[...] * chunk_decay + _kt_v(kd, v_ref[0, rows, :])


def _ret_bwd_states(z, lgb):
    b, s, _ = z.shape
    n = s // RET_CHUNK
    n_tiles = s // RET_TILE
    per_tile = RET_TILE // RET_CHUNK
    k_off = RET_QK_WIDTH // RET_DK
    v_off = 2 * RET_QK_WIDTH // RET_DV
    return pl.pallas_call(
        _ret_bwd_state_kernel,
        grid=(b, RET_HEADS, n_tiles),
        in_specs=[
            pl.BlockSpec(memory_space=pltpu.SMEM),
            pl.BlockSpec((1, RET_TILE, RET_DK), lambda bi, h, t: (bi, n_tiles - 1 - t, k_off + h)),
            pl.BlockSpec((1, RET_TILE, RET_DV), lambda bi, h, t: (bi, n_tiles - 1 - t, v_off + h)),
        ],
        out_specs=pl.BlockSpec((1, 1, per_tile, RET_DK, RET_DV), lambda bi, h, t: (bi, h, n_tiles - 1 - t, 0, 0)),
        out_shape=jax.ShapeDtypeStruct((b, RET_HEADS, n, RET_DK, RET_DV), BF16),
        scratch_shapes=[pltpu.VMEM((RET_DK, RET_DV), F32)],
        compiler_params=_params(("parallel", "parallel", "arbitrary")),
        name="retention_bwd_states",
    )(lgb, z, z)


def _ret_main_kernel(lgf_ref, lgb_ref, q_ref, k_ref, v_ref, g_ref, sb_ref, gn_ref, o_ref, state):
    head = pl.program_id(1)

    @pl.when(pl.program_id(2) == 0)
    def _():
        state[...] = jnp.zeros_like(state)

    lgf = lgf_ref[head]
    lgb = lgb_ref[head]
    c = RET_CHUNK
    pos = _chunk_positions()
    rel = (lax.broadcasted_iota(jnp.int32, (c, c), 0) - lax.broadcasted_iota(jnp.int32, (c, c), 1)).astype(F32)
    decay = jnp.exp(jnp.where(rel >= 0, lgf * rel, -lgb * rel))
    q_decay_f = jnp.exp(lgf * (pos + 1.0))
    q_decay_b = jnp.exp(lgb * (c - pos))
    k_decay_f = jnp.exp(lgf * (c - 1.0 - pos))
    chunk_decay = jnp.exp(jnp.full((1, RET_DV), lgf * c, F32))
    gn = gn_ref[...]

    for ci in range(RET_TILE // RET_CHUNK):
        rows = slice(ci * c, (ci + 1) * c)
        q = q_ref[0, rows, :]
        k = k_ref[0, rows, :]
        v = v_ref[0, rows, :]
        qf32 = q.astype(F32)
        sc = lax.dot_general(q, k, (((1,), (1,)), ((), ())), preferred_element_type=F32)
        y = jnp.dot((sc * decay).astype(BF16), v, preferred_element_type=F32)
        y = y + jnp.dot((qf32 * q_decay_f).astype(BF16), state[...].astype(BF16), preferred_element_type=F32)
        y = y + jnp.dot((qf32 * q_decay_b).astype(BF16), sb_ref[0, 0, ci], preferred_element_type=F32)
        kd = (k.astype(F32) * k_decay_f).astype(BF16)
        state[...] = state[...] * chunk_decay + _kt_v(kd, v)
        yn = _rms(y, gn)
        o_ref[0, rows, :] = (_silu(g_ref[0, rows, :].astype(F32)) * yn).astype(BF16)


def _ret_main(z, sb, lgf, lgb, gn):
    b, s, _ = z.shape
    n_tiles = s // RET_TILE
    per_tile = RET_TILE // RET_CHUNK
    k_off = RET_QK_WIDTH // RET_DK
    v_off = 2 * RET_QK_WIDTH // RET_DV
    g_off = (2 * RET_QK_WIDTH + RET_V_WIDTH) // RET_DV
    smem = pl.BlockSpec(memory_space=pltpu.SMEM)
    return pl.pallas_call(
        _ret_main_kernel,
        grid=(b, RET_HEADS, n_tiles),
        in_specs=[
            smem, smem,
            pl.BlockSpec((1, RET_TILE, RET_DK), lambda bi, h, t: (bi, t, h)),
            pl.BlockSpec((1, RET_TILE, RET_DK), lambda bi, h, t: (bi, t, k_off + h)),
            pl.BlockSpec((1, RET_TILE, RET_DV), lambda bi, h, t: (bi, t, v_off + h)),
            pl.BlockSpec((1, RET_TILE, RET_DV), lambda bi, h, t: (bi, t, g_off + h)),
            pl.BlockSpec((1, 1, per_tile, RET_DK, RET_DV), lambda bi, h, t: (bi, h, t, 0, 0)),
            pl.BlockSpec((1, RET_DV), lambda bi, h, t: (0, h)),
        ],
        out_specs=pl.BlockSpec((1, RET_TILE, RET_DV), lambda bi, h, t: (bi, t, h)),
        out_shape=jax.ShapeDtypeStruct((b, s, RET_V_WIDTH), BF16),
        scratch_shapes=[pltpu.VMEM((RET_DK, RET_DV), F32)],
        compiler_params=_params(("parallel", "parallel", "arbitrary")),
        name="retention_main",
    )(lgf, lgb, z, z, z, z, sb, gn)


def _odd_out_router_kernel(y_ref, x_ref, w_ref, ln_ref, wr_ref, xo_ref, h_ref, idx_ref, gate_ref):
    x = x_ref[...] + jnp.dot(y_ref[...], w_ref[...], preferred_element_type=F32)
    xo_ref[...] = x
    hn = _rms(x, ln_ref[...])
    h_ref[...] = hn.astype(BF16)
    logits = lax.dot_general(wr_ref[...], hn, (((1,), (1,)), ((), ())),
                             precision=lax.Precision.HIGHEST, preferred_element_type=F32)
    row = lax.broadcasted_iota(jnp.int32, logits.shape, 0).astype(F32)
    none = float(N_EXPERTS)
    m1 = jnp.max(logits, axis=0, keepdims=True)
    i1 = jnp.min(jnp.where(logits == m1, row, none), axis=0, keepdims=True)
    rest = jnp.where(row == i1, -jnp.inf, logits)
    m2 = jnp.max(rest, axis=0, keepdims=True)
    i2 = jnp.min(jnp.where(rest == m2, row, none), axis=0, keepdims=True)
    e2 = jnp.exp(m2 - m1)
    idx_ref[0:1, :] = i1.astype(jnp.int32)
    idx_ref[1:2, :] = i2.astype(jnp.int32)
    gate_ref[0:1, :] = 1.0 / (1.0 + e2)
    gate_ref[1:2, :] = e2 / (1.0 + e2)


def _odd_out_router(y, x, w_out, ln, wr_t, tm=512):
    n, d = x.shape
    full = lambda shp: pl.BlockSpec(shp, lambda i: (0,) * len(shp))
    return pl.pallas_call(
        _odd_out_router_kernel,
        grid=(n // tm,),
        in_specs=[
            pl.BlockSpec((tm, RET_V_WIDTH), lambda i: (i, 0)),
            pl.BlockSpec((tm, d), lambda i: (i, 0)),
            full((RET_V_WIDTH, d)), full((1, d)), full((N_EXPERTS, d)),
        ],
        out_specs=[
            pl.BlockSpec((tm, d), lambda i: (i, 0)),
            pl.BlockSpec((tm, d), lambda i: (i, 0)),
            pl.BlockSpec((2, tm), lambda i: (0, i)),
            pl.BlockSpec((2, tm), lambda i: (0, i)),
        ],
        out_shape=[
            jax.ShapeDtypeStruct((n, d), F32),
            jax.ShapeDtypeStruct((n, d), BF16),
            jax.ShapeDtypeStruct((2, n), jnp.int32),
            jax.ShapeDtypeStruct((2, n), F32),
        ],
        compiler_params=_params(("parallel",)),
        name="odd_out_proj_router",
    )(y, x, w_out, ln, wr_t)


MOE_TM = 1024
MOE_TF = 1792
MOE_SUB = 512


def _moe_kernel(tile_expert_ref, n_used_ref, xs_ref, gate_ref, w1_ref, w3_ref, w2_ref, o_ref, acc_ref):
    t = pl.program_id(0)
    f = pl.program_id(1)
    last = pl.num_programs(1) - 1
    used = t < n_used_ref[0]

    @pl.when(used)
    def _():
        for r0 in range(0, MOE_TM, MOE_SUB):
            rows = slice(r0, r0 + MOE_SUB)
            xs = xs_ref[rows, :]
            g = jnp.dot(xs, w1_ref[...], preferred_element_type=F32)
            u = jnp.dot(xs, w3_ref[...], preferred_element_type=F32)
            act = (_silu(g) * u).astype(BF16)
            part = jnp.dot(act, w2_ref[...], preferred_element_type=F32)

            @pl.when(f == 0)
            def _():
                acc_ref[rows, :] = part

            @pl.when(jnp.logical_and(f > 0, f < last))
            def _():
                acc_ref[rows, :] += part

            @pl.when(f == last)
            def _():
                o_ref[rows, :] = ((acc_ref[rows, :] + part) * gate_ref[rows, 0:1]).astype(BF16)

    @pl.when(jnp.logical_and(jnp.logical_not(used), f == last))
    def _():
        o_ref[...] = jnp.zeros_like(o_ref)


def _moe_experts(xs, gate_rows, tile_expert, n_used, w1, w3, w2):
    p, d = xs.shape
    n_tiles = p // MOE_TM
    n_f = D_FF_EXPERT // MOE_TF
    assert n_f >= 2 and n_f * MOE_TF == D_FF_EXPERT

    def row_tile(t, f, te, nu):
        return (jnp.minimum(t, nu[0] - 1), 0)

    def f_idx(t, f, nu):
        return jnp.where(t < nu[0], f, n_f - 1)

    grid_spec = pltpu.PrefetchScalarGridSpec(
        num_scalar_prefetch=2,
        grid=(n_tiles, n_f),
        in_specs=[
            pl.BlockSpec((MOE_TM, d), row_tile),
            pl.BlockSpec((MOE_TM, LANES), row_tile),
            pl.BlockSpec((None, d, MOE_TF), lambda t, f, te, nu: (te[t], 0, f_idx(t, f, nu))),
            pl.BlockSpec((None, d, MOE_TF), lambda t, f, te, nu: (te[t], 0, f_idx(t, f, nu))),
            pl.BlockSpec((None, MOE_TF, d), lambda t, f, te, nu: (te[t], f_idx(t, f, nu), 0)),
        ],
        out_specs=pl.BlockSpec((MOE_TM, d), lambda t, f, te, nu: (t, 0)),
        scratch_shapes=[pltpu.VMEM((MOE_TM, d), F32)],
    )
    return pl.pallas_call(
        _moe_kernel,
        grid_spec=grid_spec,
        out_shape=jax.ShapeDtypeStruct((p, d), BF16),
        compiler_params=_params(("arbitrary", "arbitrary")),
        name="moe_experts",
    )(tile_expert, n_used, xs, gate_rows, w1, w3, w2)


def _combine_kernel(x_ref, y_ref, o_ref):
    o_ref[...] = x_ref[...] + y_ref[0].astype(F32) + y_ref[1].astype(F32)


def _combine(x, y2, tm=1024):
    n, d = x.shape
    return pl.pallas_call(
        _combine_kernel,
        grid=(n // tm,),
        in_specs=[pl.BlockSpec((tm, d), lambda i: (i, 0)), pl.BlockSpec((2, tm, d), lambda i: (0, i, 0))],
        out_specs=pl.BlockSpec((tm, d), lambda i: (i, 0)),
        out_shape=jax.ShapeDtypeStruct((n, d), F32),
        compiler_params=_params(("parallel",)),
        name="moe_combine",
    )(x, y2)


def _route(idx, gates, n):
    e_flat = idx.reshape(-1)
    n_pairs = e_flat.shape[0]
    pair_id = jnp.arange(n_pairs, dtype=jnp.int32)
    order = (jnp.sort(e_flat * n_pairs + pair_id) % n_pairs).astype(jnp.int32)
    rank = jnp.argsort(order).astype(jnp.int32)
    counts = jnp.sum(e_flat[None, :] == jnp.arange(N_EXPERTS, dtype=jnp.int32)[:, None], axis=1).astype(jnp.int32)
    starts = jnp.cumsum(counts) - counts
    tiles_per_expert = (counts + MOE_TM - 1) // MOE_TM
    tile_ends = jnp.cumsum(tiles_per_expert)
    padded_starts = (tile_ends - tiles_per_expert) * MOE_TM
    dest = rank + (padded_starts - starts)[e_flat]
    n_rows = n_pairs + N_EXPERTS * MOE_TM
    n_tiles = n_rows // MOE_TM
    tile_expert = jnp.minimum(
        jnp.sum(jnp.arange(n_tiles, dtype=jnp.int32)[:, None] >= tile_ends[None, :], axis=1), N_EXPERTS - 1
    ).astype(jnp.int32)
    row_expert = jnp.repeat(tile_expert, MOE_TM)
    row_in_expert = jnp.arange(n_rows, dtype=jnp.int32) - padded_starts[row_expert]
    row_valid = row_in_expert < counts[row_expert]
    row_pair = order[jnp.clip(starts[row_expert] + row_in_expert, 0, n_pairs - 1)]
    src_token = jnp.where(row_valid, row_pair % n, 0).astype(jnp.int32)
    row_gate = jnp.where(row_valid, gates.reshape(-1)[row_pair], 0.0)
    n_used = tile_ends[-1:].astype(jnp.int32)
    return src_token, row_gate, dest, tile_expert, n_used


CAST_BLOCK_BYTES = 8 * 2**20


def _cast_kernel(x_ref, o_ref):
    o_ref[...] = x_ref[...].astype(BF16)


def _to_bf16(w):
    shape = w.shape
    w2 = w.reshape(-1, shape[-1])
    n, c = w2.shape
    limit = CAST_BLOCK_BYTES // (4 * c)
    rows = max(r for r in range(16, limit + 1, 16) if n % r == 0)
    out = pl.pallas_call(
        _cast_kernel,
        grid=(n // rows,),
        in_specs=[pl.BlockSpec((rows, c), lambda i: (i, 0))],
        out_specs=pl.BlockSpec((rows, c), lambda i: (i, 0)),
        out_shape=jax.ShapeDtypeStruct((n, c), BF16),
        compiler_params=_params(("parallel",)),
        name="cast_bf16",
    )(w2)
    return out.reshape(shape)


def _channel_dft():
    c = jnp.arange(FNET_GROUP_DIM, dtype=jnp.int32)
    ang = ((c[:, None] * c[None, :]) % FNET_GROUP_DIM).astype(F32) * (2.0 * math.pi / FNET_GROUP_DIM)
    eye = jnp.eye(FNET_GROUPS, dtype=F32)
    scale = FNET_GROUP_DIM ** -0.5
    return jnp.concatenate([jnp.kron(eye, jnp.cos(ang)) * scale, -jnp.kron(eye, jnp.sin(ang)) * scale], axis=1).astype(BF16)


def _sequence_dft(s):
    hi = s // 64
    k = jnp.arange(s, dtype=jnp.int32)
    j_hi = jnp.arange(hi, dtype=jnp.int32)
    j_lo = jnp.arange(64, dtype=jnp.int32)
    w = 2.0 * math.pi / s
    ang_a = (((j_hi[:, None] * k[None, :]) % hi) * 64).astype(F32) * w
    ang_b = ((j_lo[:, None] * k[None, :]) % s).astype(F32) * w
    scale = s ** -0.5
    ca, sa = jnp.cos(ang_a) * scale, jnp.sin(ang_a) * scale
    cb, sb = jnp.cos(ang_b), jnp.sin(ang_b)
    ca2 = jnp.concatenate([ca, ca], axis=1)[:, None, :]
    sa2 = jnp.concatenate([sa, sa], axis=1)[:, None, :]
    p = jnp.concatenate([cb, sb], axis=1)[None, :, :]
    q = jnp.concatenate([sb, -cb], axis=1)[None, :, :]
    return (ca2 * p - sa2 * q).reshape(s, 2 * s).astype(BF16)


def _trunk_to_router(x, p):
    b, s, d = x.shape
    n = b * s
    y, qkv = _even_in(x, p["ln_mix_e"], p["w_in_e"], p["chan_dft"], p["head_sum"], p["qn"], p["kn"])
    fmix = _matmul(_sequence_dft(s), y.reshape(2 * s, b * FNET_WIDTH))
    att = _attention(qkv, p["slopes"])
    x = _even_out(x, fmix, att, p["w_out_e"])
    x = _swiglu(x.reshape(n, d), p["ln_ffn_e"], p["w_gate_e"], p["w_up_e"], p["w_down_e"])
    z = _odd_in(x, p["ln_mix_o"], p["w_in_o"]).reshape(b, s, ODD_IN)
    sb = _ret_bwd_states(z, p["lgb"])
    yr = _ret_main(z, sb, p["lgf"], p["lgb"], p["ret_gn"])
    x, h, idx, gates = _odd_out_router(yr.reshape(n, RET_V_WIDTH), x, p["w_out_o"], p["ln_ffn_o"], p["w_router_t"])
    return x, h, _route(idx, gates, n)


def _dispatch(h, route):
    src_token, row_gate = route[0], route[1]
    xs = h.at[src_token].get(mode="promise_in_bounds")
    return xs, jnp.broadcast_to(row_gate[:, None], (row_gate.shape[0], LANES))


def _experts(xs, gate_rows, route, p):
    return _moe_experts(xs, gate_rows, route[3], route[4], p["moe_w1"], p["moe_w3"], p["moe_w2"])


def _collect(ye, route, n):
    return ye.at[route[2]].get(mode="promise_in_bounds").reshape(2, n, D_MODEL)


def _both_trunks(x_a, x_b, p):
    n_a = x_a.shape[0] * x_a.shape[1]
    n_b = x_b.shape[0] * x_b.shape[1]
    xa, ha, ra = _trunk_to_router(x_a, p)
    x_b, ra = lax.optimization_barrier((x_b, ra))
    xs_a, gr_a = _dispatch(ha, ra)
    xb, hb, rb = _trunk_to_router(x_b, p)
    xs_a, rb = lax.optimization_barrier((xs_a, rb))
    xs_b, gr_b = _dispatch(hb, rb)
    ye_a = _experts(xs_a, gr_a, ra, p)
    ye_a, xs_b = lax.optimization_barrier((ye_a, xs_b))
    y2_a = _collect(ye_a, ra, n_a)
    ye_b = _experts(xs_b, gr_b, rb, p)
    xa, ye_b = lax.optimization_barrier((xa, ye_b))
    out_a = _combine(xa, y2_a).reshape(x_a.shape)
    out_b = _combine(xb, _collect(ye_b, rb, n_b)).reshape(x_b.shape)
    return out_a, out_b


def kernel(x_prompt, x_sample, ln_mix_e, w_in_e, w_out_e, qn_e, kn_e, ln_ffn_e, w_gate_e, w_up_e, w_down_e, ln_mix_o, w_in_o, w_out_o, logdecay_fwd, logdecay_bwd, ret_gn, ln_ffn_o, w_router, moe_w1, moe_w3, moe_w2):
    row = lambda w: w.reshape(1, -1).astype(F32)
    k_scale = jnp.concatenate([
        jnp.ones((RET_QK_WIDTH,), F32), jnp.full((RET_QK_WIDTH,), RET_DK ** -0.5, F32),
        jnp.ones((2 * RET_V_WIDTH,), F32)])
    head_id = jnp.arange(LANES, dtype=jnp.int32) // ATT_HEAD_DIM
    p = {
        "ln_mix_e": row(ln_mix_e[0]),
        "w_in_e": w_in_e[0].astype(BF16),
        "w_out_e": w_out_e[0].astype(BF16),
        "qn": row(jnp.tile(qn_e[0], 2) * (ATT_HEAD_DIM ** -0.5)),
        "kn": row(jnp.tile(kn_e[0], 2)),
        "ln_ffn_e": row(ln_ffn_e[0]),
        "w_gate_e": w_gate_e[0].astype(BF16),
        "w_up_e": w_up_e[0].astype(BF16),
        "w_down_e": w_down_e[0].astype(BF16),
        "ln_mix_o": row(ln_mix_o[0]),
        "w_in_o": (w_in_o[0] * k_scale[None, :]).astype(BF16),
        "w_out_o": w_out_o[0].astype(BF16),
        "lgf": logdecay_fwd[0].astype(F32),
        "lgb": logdecay_bwd[0].astype(F32),
        "ret_gn": row(ret_gn[0]),
        "ln_ffn_o": row(ln_ffn_o[0]),
        "w_router_t": w_router[0].T.astype(F32),
        "moe_w1": _to_bf16(moe_w1[0]),
        "moe_w3": _to_bf16(moe_w3[0]),
        "moe_w2": _to_bf16(moe_w2[0]),
        "chan_dft": _channel_dft(),
        "head_sum": (head_id[:, None] == head_id[None, :]).astype(BF16),
        "slopes": jnp.exp2(-8.0 * (jnp.arange(ATT_HEADS, dtype=F32) + 1.0) / ATT_HEADS),
    }
    return _both_trunks(x_prompt, x_sample, p)
```

```python
import functools
import math

import jax
import jax.numpy as jnp
from jax import lax
from jax.experimental import pallas as pl
from jax.experimental.pallas import tpu as pltpu

D_MODEL = 1024
FNET_GROUPS = 4
FNET_GROUP_DIM = 64
FNET_WIDTH = 256
ATT_HEAD_DIM = 64
ATT_HEADS = 12
ATT_WIDTH = 768
ATT_HALF_WINDOW = 64
DILATIONS = (1, 4, 16)
EVEN_IN = FNET_WIDTH + 3 * ATT_WIDTH
RET_HEADS = 4
RET_DK = 256
RET_DV = 512
RET_QK_WIDTH = 1024
RET_V_WIDTH = 2048
ODD_IN = 6144
D_FF = 2816
N_EXPERTS = 8
D_FF_EXPERT = 3584
EPS = 1e-6
NEG = -1e30

LANES = 128
VMEM_LIMIT_BYTES = 56 * 2**20

F32 = jnp.float32
BF16 = jnp.bfloat16


def _params(semantics):
    return pltpu.CompilerParams(dimension_semantics=semantics, vmem_limit_bytes=VMEM_LIMIT_BYTES)


def _rms(x, w):
    return x * lax.rsqrt(jnp.mean(x * x, axis=-1, keepdims=True) + EPS) * w


def _silu(x):
    return x / (1.0 + jnp.exp(-x))


def _even_in_kernel(x_ref, ln_ref, w_ref, cd_ref, hs_ref, qn_ref, kn_ref, y_ref, qkv_ref):
    h = _rms(x_ref[0], ln_ref[...]).astype(BF16)
    u = jnp.dot(h, w_ref[:, 0:FNET_WIDTH], preferred_element_type=F32).astype(BF16)
    yy = jnp.dot(u, cd_ref[...], preferred_element_type=F32)
    y_ref[0] = yy[:, :FNET_WIDTH].astype(BF16)
    y_ref[1] = yy[:, FNET_WIDTH:].astype(BF16)
    for part, n_ref in ((0, qn_ref), (1, kn_ref)):
        c0 = FNET_WIDTH + part * ATT_WIDTH
        z = jnp.dot(h, w_ref[:, c0:c0 + ATT_WIDTH], preferred_element_type=F32)
        for j in range(ATT_WIDTH // LANES):
            zj = z[:, j * LANES:(j + 1) * LANES]
            ss = jnp.dot((zj * zj).astype(BF16), hs_ref[...], preferred_element_type=F32)
            zn = zj * lax.rsqrt(ss * (1.0 / ATT_HEAD_DIM) + EPS) * n_ref[...]
            qkv_ref[0, :, part * ATT_WIDTH + j * LANES:part * ATT_WIDTH + (j + 1) * LANES] = zn.astype(BF16)
    c0 = FNET_WIDTH + 2 * ATT_WIDTH
    v = jnp.dot(h, w_ref[:, c0:c0 + ATT_WIDTH], preferred_element_type=F32)
    qkv_ref[0, :, 2 * ATT_WIDTH:3 * ATT_WIDTH] = v.astype(BF16)


def _even_in(x, ln, w_in, cd, hs, qn, kn, tm=512):
    b, s, d = x.shape
    full = lambda shp: pl.BlockSpec(shp, lambda bi, i: (0,) * len(shp))
    return pl.pallas_call(
        _even_in_kernel,
        grid=(b, s // tm),
        in_specs=[
            pl.BlockSpec((1, tm, d), lambda bi, i: (bi, i, 0)),
            full((1, d)), full((d, EVEN_IN)), full((FNET_WIDTH, 2 * FNET_WIDTH)), full((LANES, LANES)),
            full((1, LANES)), full((1, LANES)),
        ],
        out_specs=[
            pl.BlockSpec((2, tm, FNET_WIDTH), lambda bi, i: (0, i, bi)),
            pl.BlockSpec((1, tm, 3 * ATT_WIDTH), lambda bi, i: (bi, i, 0)),
        ],
        out_shape=[
            jax.ShapeDtypeStruct((2, s, b * FNET_WIDTH), BF16),
            jax.ShapeDtypeStruct((b, s, 3 * ATT_WIDTH), BF16),
        ],
        compiler_params=_params(("parallel", "parallel")),
        name="even_in_proj",
    )(x, ln, w_in, cd, hs, qn, kn)


def _matmul_kernel(a_ref, b_ref, o_ref, acc_ref):
    @pl.when(pl.program_id(2) == 0)
    def _():
        acc_ref[...] = jnp.zeros_like(acc_ref)

    acc_ref[...] += jnp.dot(a_ref[...], b_ref[...], preferred_element_type=F32)

    @pl.when(pl.program_id(2) == pl.num_programs(2) - 1)
    def _():
        o_ref[...] = acc_ref[...].astype(o_ref.dtype)


def _matmul(a, b, tm=1024, tn=2048, tk=1024):
    m, k = a.shape
    _, n = b.shape
    tm, tn, tk = min(tm, m), min(tn, n), min(tk, k)
    return pl.pallas_call(
        _matmul_kernel,
        grid=(m // tm, n // tn, k // tk),
        in_specs=[pl.BlockSpec((tm, tk), lambda i, j, kk: (i, kk)),
                  pl.BlockSpec((tk, tn), lambda i, j, kk: (kk, j))],
        out_specs=pl.BlockSpec((tm, tn), lambda i, j, kk: (i, j)),
        out_shape=jax.ShapeDtypeStruct((m, n), BF16),
        scratch_shapes=[pltpu.VMEM((tm, tn), F32)],
        compiler_params=_params(("parallel", "parallel", "arbitrary")),
        name="seq_dft_matmul",
    )(a, b)


ATT_QBLK = 128
ATT_CONV_ROWS = 512
ATT_GROUP = 16
ATT_MAX_DILATION = 16
ATT_DOUBLE_BUFFER_BYTES = 16 * 2**20


def _attn_kernel(slopes_ref, k_ref, v_ref, *refs, seq):
    nc = ATT_MAX_DILATION
    q_cls, k_cls, v_cls = refs[0:nc], refs[nc:2 * nc], refs[2 * nc:3 * nc]
    o_ref, xf, qs, kd, vd, bias, sc_buf, p_buf, m_buf, m_s, l_s, a_s = refs[3 * nc:]
    pair = pl.program_id(1)
    lane = lax.broadcasted_iota(jnp.int32, (1, LANES), 1)
    first_head = lane < ATT_HEAD_DIM
    slopes = (slopes_ref[2 * pair], slopes_ref[2 * pair + 1])
    base_len = seq // ATT_MAX_DILATION

    def conv_rows(c):
        return pl.ds(pl.multiple_of(c * ATT_CONV_ROWS, ATT_CONV_ROWS), ATT_CONV_ROWS)

    def store_split(dst_ref, rows, x, fill):
        other = jnp.full_like(x, fill)
        dst_ref[0, rows, :] = jnp.where(first_head, x, other)
        dst_ref[1, rows, :] = jnp.where(first_head, other, x)

    def class_major_rows(t, r):
        blocks_per_class = seq // r // ATT_QBLK
        rho = t // blocks_per_class
        l0 = (t % blocks_per_class) * ATT_QBLK
        return pl.ds(rho + r * l0, ATT_QBLK, stride=r)

    def class_rows(rho):
        return slice(rho * base_len, (rho + 1) * base_len)

    for rho in range(nc):
        qs[class_rows(rho), :] = q_cls[rho][0].astype(F32)

    for pattern, r in enumerate(DILATIONS):
        sub_len = seq // r
        n_keys = min(2 * ATT_QBLK, sub_len)
        blocks_per_class = sub_len // ATT_QBLK
        n_runs = ATT_MAX_DILATION // r
        run_len = ATT_QBLK // n_runs

        n_kruns = 1 if r == 1 else n_runs
        krun_len = n_keys // n_kruns
        if r == 1:
            def copy(c, carry):
                store_split(kd, conv_rows(c), k_ref[0, conv_rows(c), :], 0.0)
                store_split(vd, conv_rows(c), v_ref[0, conv_rows(c), :], 1.0)
                return carry

            lax.fori_loop(0, seq // ATT_CONV_ROWS, copy, 0)
        elif pattern == 1:
            for rho in range(nc):
                store_split(kd, class_rows(rho), k_cls[rho][0], 0.0)
                store_split(vd, class_rows(rho), v_cls[rho][0], 1.0)

        n_idx = lax.broadcasted_iota(jnp.int32, (ATT_QBLK, n_keys), 0)
        q_rel = n_runs * (n_idx % run_len) + n_idx // run_len
        c_idx = lax.broadcasted_iota(jnp.int32, (ATT_QBLK, n_keys), 1)
        k_rel = n_kruns * (c_idx % krun_len) + c_idx // krun_len
        offsets = (0, ATT_HALF_WINDOW, 2 * ATT_HALF_WINDOW) if blocks_per_class > 1 else (0,)
        for variant, off in enumerate(offsets):
            dist = jnp.abs(k_rel - off - q_rel)
            for h in range(2):
                scaled = (-slopes[h] * float(r)) * dist.astype(F32)
                bias[h, variant, :, 0:n_keys] = jnp.where(dist <= ATT_HALF_WINDOW, scaled, NEG)

        def geometry(t, r=r, blocks_per_class=blocks_per_class, n_runs=n_runs, run_len=run_len,
                     n_kruns=n_kruns, krun_len=krun_len, n_keys=n_keys):
            rho = t // blocks_per_class
            lb = t % blocks_per_class
            if blocks_per_class > 1:
                variant = jnp.where(lb == 0, 0, jnp.where(lb == blocks_per_class - 1, 2, 1))
            else:
                variant = 0
            runs = [pl.ds(pl.multiple_of((rho + r * j) * base_len + lb * run_len, 8), run_len)
                    for j in range(n_runs)]
            if r == 1:
                k_runs = [pl.ds(pl.multiple_of(t * ATT_QBLK - variant * ATT_HALF_WINDOW, ATT_HALF_WINDOW), n_keys)]
            else:
                start = lb * run_len - variant * (ATT_HALF_WINDOW // n_kruns)
                k_runs = [pl.ds(pl.multiple_of((rho + r * j) * base_len + start, 16), krun_len)
                          for j in range(n_kruns)]
            return variant, k_runs, runs

        def load_runs(ref, runs):
            return jnp.concatenate([ref[rows, :] for rows in runs], axis=0) if len(runs) > 1 else ref[runs[0], :]

        def store_runs(ref, runs, x, run_len=run_len):
            for j, rows in enumerate(runs):
                ref[rows, :] = x[j * run_len:(j + 1) * run_len, :]

        def group(g, carry, n_keys=n_keys, pattern=pattern, geometry=geometry, load_runs=load_runs,
                  store_runs=store_runs):
            for i in range(ATT_GROUP):
                variant, k_runs, runs = geometry(g * ATT_GROUP + i)
                q = load_runs(qs, runs).astype(BF16)
                for h in range(2):
                    sc = lax.dot_general(q, load_runs(kd.at[h], k_runs), (((1,), (1,)), ((), ())),
                                         preferred_element_type=F32)
                    sc_buf[i, h, :, 0:n_keys] = sc + bias[h, variant, :, 0:n_keys]
            for i in range(ATT_GROUP):
                ms = []
                for h in range(2):
                    sc = sc_buf[i, h, :, 0:n_keys]
                    m = jnp.max(sc, axis=-1, keepdims=True)
                    p_buf[i, h, :, 0:n_keys] = jnp.exp(sc - m).astype(BF16)
                    ms.append(m)
                m_buf[i] = jnp.where(first_head, ms[0], ms[1])
            for i in range(ATT_GROUP):
                _, k_runs, runs = geometry(g * ATT_GROUP + i)
                pv0 = jnp.dot(p_buf[i, 0, :, 0:n_keys], load_runs(vd.at[0], k_runs), preferred_element_type=F32)
                pv1 = jnp.dot(p_buf[i, 1, :, 0:n_keys], load_runs(vd.at[1], k_runs), preferred_element_type=F32)
                a_new = jnp.where(first_head, pv0, pv1)
                l_new = pltpu.roll(jnp.where(first_head, pv1, pv0), ATT_HEAD_DIM, 1)
                m_new = m_buf[i]
                if pattern == 0:
                    store_runs(m_s, runs, m_new)
                    store_runs(l_s, runs, l_new)
                    store_runs(a_s, runs, a_new)
                else:
                    m_old = load_runs(m_s, runs)
                    m_tot = jnp.maximum(m_old, m_new)
                    w_old = jnp.exp(m_old - m_tot)
                    w_new = jnp.exp(m_new - m_tot)
                    store_runs(m_s, runs, m_tot)
                    store_runs(l_s, runs, w_old * load_runs(l_s, runs) + w_new * l_new)
                    store_runs(a_s, runs, w_old * load_runs(a_s, runs) + w_new * a_new)
            return carry

        lax.fori_loop(0, seq // (ATT_QBLK * ATT_GROUP), group, 0)

    def scatter(t, carry):
        rows = pl.ds(pl.multiple_of(t * ATT_QBLK, ATT_QBLK), ATT_QBLK)
        xf[class_major_rows(t, ATT_MAX_DILATION), :] = a_s[rows, :] / l_s[rows, :]
        return carry

    lax.fori_loop(0, seq // ATT_QBLK, scatter, 0)

    def finish(c, carry):
        o_ref[0, conv_rows(c), :] = xf[conv_rows(c), :].astype(BF16)
        return carry

    lax.fori_loop(0, seq // ATT_CONV_ROWS, finish, 0)


def _attention(qkv, slopes):
    b, s, _ = qkv.shape
    n_pairs = ATT_WIDTH // LANES
    nc = ATT_MAX_DILATION
    col_blocks = qkv.shape[2] // LANES
    by_class = qkv.reshape(b, s // nc, nc * qkv.shape[2])
    mode = {} if s * LANES * 2 * 5 * 2 <= ATT_DOUBLE_BUFFER_BYTES else {"pipeline_mode": pl.Buffered(1)}
    blk = lambda off: pl.BlockSpec((1, s, LANES), lambda bi, j, off=off: (bi, 0, off + j), **mode)
    cls = lambda rho, off: pl.BlockSpec((1, s // nc, LANES),
                                        lambda bi, j, rho=rho, off=off: (bi, 0, rho * col_blocks + off + j), **mode)
    class_specs = [cls(rho, off) for off in (0, n_pairs, 2 * n_pairs) for rho in range(nc)]
    return pl.pallas_call(
        functools.partial(_attn_kernel, seq=s),
        grid=(b, n_pairs),
        in_specs=[pl.BlockSpec(memory_space=pltpu.SMEM), blk(n_pairs), blk(2 * n_pairs)] + class_specs,
        out_specs=pl.BlockSpec((1, s, LANES), lambda bi, j: (bi, 0, j)),
        out_shape=jax.ShapeDtypeStruct((b, s, ATT_WIDTH), BF16),
        scratch_shapes=[
            pltpu.VMEM((s, LANES), F32),
            pltpu.VMEM((s, LANES), F32),
            pltpu.VMEM((2, s, LANES), BF16),
            pltpu.VMEM((2, s, LANES), BF16),
            pltpu.VMEM((2, 3, ATT_QBLK, 2 * ATT_QBLK), F32),
            pltpu.VMEM((ATT_GROUP, 2, ATT_QBLK, 2 * ATT_QBLK), F32),
            pltpu.VMEM((ATT_GROUP, 2, ATT_QBLK, 2 * ATT_QBLK), BF16),
            pltpu.VMEM((ATT_GROUP, ATT_QBLK, LANES), F32),
            pltpu.VMEM((s, LANES), F32),
            pltpu.VMEM((s, LANES), F32),
            pltpu.VMEM((s, LANES), F32),
        ],
        compiler_params=_params(("parallel", "parallel")),
        name="dilated_attention",
    )(slopes, qkv, qkv, *([by_class] * (3 * nc)))


def _even_out_kernel(x_ref, f_ref, a_ref, w_ref, o_ref):
    acc = x_ref[0]
    acc = acc + jnp.dot(f_ref[...], w_ref[0:FNET_WIDTH, :], preferred_element_type=F32)
    acc = acc + jnp.dot(a_ref[0], w_ref[FNET_WIDTH:, :], preferred_element_type=F32)
    o_ref[0] = acc


def _even_out(x, fmix, att, w_out, tm=512):
    b, s, d = x.shape
    return pl.pallas_call(
        _even_out_kernel,
        grid=(b, s // tm),
        in_specs=[
            pl.BlockSpec((1, tm, d), lambda bi, i: (bi, i, 0)),
            pl.BlockSpec((tm, FNET_WIDTH), lambda bi, i: (i, bi)),
            pl.BlockSpec((1, tm, ATT_WIDTH), lambda bi, i: (bi, i, 0)),
            pl.BlockSpec((d, d), lambda bi, i: (0, 0)),
        ],
        out_specs=pl.BlockSpec((1, tm, d), lambda bi, i: (bi, i, 0)),
        out_shape=jax.ShapeDtypeStruct((b, s, d), F32),
        compiler_params=_params(("parallel", "parallel")),
        name="even_out_proj",
    )(x, fmix, att, w_out)


def _resident(shape):
    return pl.BlockSpec(shape, lambda *_: (0,) * len(shape), pipeline_mode=pl.Buffered(1))


def _swiglu_kernel(x_ref, ln_ref, wg_ref, wu_ref, wd_ref, o_ref):
    x = x_ref[...]
    h = _rms(x, ln_ref[...]).astype(BF16)
    g = jnp.dot(h, wg_ref[...], preferred_element_type=F32)
    u = jnp.dot(h, wu_ref[...], preferred_element_type=F32)
    act = (_silu(g) * u).astype(BF16)
    o_ref[...] = x + jnp.dot(act, wd_ref[...], preferred_element_type=F32)


def _swiglu(x, ln, wg, wu, wd, tm=512):
    n, d = x.shape
    dff = wg.shape[1]
    return pl.pallas_call(
        _swiglu_kernel,
        grid=(n // tm,),
        in_specs=[
            pl.BlockSpec((tm, d), lambda i: (i, 0)),
            _resident((1, d)), _resident((d, dff)), _resident((d, dff)), _resident((dff, d)),
        ],
        out_specs=pl.BlockSpec((tm, d), lambda i: (i, 0)),
        out_shape=jax.ShapeDtypeStruct((n, d), F32),
        compiler_params=_params(("parallel",)),
        name="swiglu_ffn",
    )(x, ln, wg, wu, wd)


ODD_IN_COLS = 2048


def _odd_in_kernel(x_ref, ln_ref, w_ref, o_ref):
    h = _rms(x_ref[...], ln_ref[...]).astype(BF16)
    for c0 in range(0, ODD_IN, ODD_IN_COLS):
        cols = slice(c0, c0 + ODD_IN_COLS)
        o_ref[:, cols] = jnp.dot(h, w_ref[:, cols], preferred_element_type=F32).astype(BF16)


def _odd_in(x, ln, w_in, tm=512):
    n, d = x.shape
    return pl.pallas_call(
        _odd_in_kernel,
        grid=(n // tm,),
        in_specs=[pl.BlockSpec((tm, d), lambda i: (i, 0)), _resident((1, d)), _resident((d, ODD_IN))],
        out_specs=pl.BlockSpec((tm, ODD_IN), lambda i: (i, 0)),
        out_shape=jax.ShapeDtypeStruct((n, ODD_IN), BF16),
        compiler_params=_params(("parallel",)),
        name="odd_in_proj",
    )(x, ln, w_in)


RET_CHUNK = 256
RET_TILE = 1024


def _chunk_positions():
    return lax.broadcasted_iota(jnp.int32, (RET_CHUNK, 1), 0).astype(F32)


def _kt_v(k_scaled, v):
    return lax.dot_general(k_scaled, v, (((0,), (0,)), ((), ())), preferred_element_type=F32)


def _ret_bwd_state_kernel(lgb_ref, k_ref, v_ref, sb_ref, state):
    head = pl.program_id(1)

    @pl.when(pl.program_id(2) == 0)
    def _():
        state[...] = jnp.zeros_like(state)

    lg = lgb_ref[head]
    key_decay = jnp.exp(lg * _chunk_positions())
    chunk_decay = jnp.exp(jnp.full((1, RET_DV), lg * RET_CHUNK, F32))
    for c in reversed(range(RET_TILE // RET_CHUNK)):
        rows = slice(c * RET_CHUNK, (c + 1) * RET_CHUNK)
        sb_ref[0, 0, c] = state[...].astype(BF16)
        kd = (k_ref[0, rows, :].astype(F32) * key_decay).astype(BF16)
        state[...] = state[...] * chunk_decay + _kt_v(kd, v_ref[0, rows, :])


def _ret_bwd_states(z, lgb):
    b, s, _ = z.shape
    n = s // RET_CHUNK
    n_tiles = s // RET_TILE
    per_tile = RET_TILE // RET_CHUNK
    k_off = RET_QK_WIDTH // RET_DK
    v_off = 2 * RET_QK_WIDTH // RET_DV
    return pl.pallas_call(
        _ret_bwd_state_kernel,
        grid=(b, RET_HEADS, n_tiles),
        in_specs=[
            pl.BlockSpec(memory_space=pltpu.SMEM),
            pl.BlockSpec((1, RET_TILE, RET_DK), lambda bi, h, t: (bi, n_tiles - 1 - t, k_off + h)),
            pl.BlockSpec((1, RET_TILE, RET_DV), lambda bi, h, t: (bi, n_tiles - 1 - t, v_off + h)),
        ],
        out_specs=pl.BlockSpec((1, 1, per_tile, RET_DK, RET_DV), lambda bi, h, t: (bi, h, n_tiles - 1 - t, 0, 0)),
        out_shape=jax.ShapeDtypeStruct((b, RET_HEADS, n, RET_DK, RET_DV), BF16),
        scratch_shapes=[pltpu.VMEM((RET_DK, RET_DV), F32)],
        compiler_params=_params(("parallel", "parallel", "arbitrary")),
        name="retention_bwd_states",
    )(lgb, z, z)


def _ret_main_kernel(lgf_ref, lgb_ref, q_ref, k_ref, v_ref, g_ref, sb_ref, gn_ref, o_ref, state):
    head = pl.program_id(1)

    @pl.when(pl.program_id(2) == 0)
    def _():
        state[...] = jnp.zeros_like(state)

    lgf = lgf_ref[head]
    lgb = lgb_ref[head]
    c = RET_CHUNK
    pos = _chunk_positions()
    rel = (lax.broadcasted_iota(jnp.int32, (c, c), 0) - lax.broadcasted_iota(jnp.int32, (c, c), 1)).astype(F32)
    decay = jnp.exp(jnp.where(rel >= 0, lgf * rel, -lgb * rel))
    q_decay_f = jnp.exp(lgf * (pos + 1.0))
    q_decay_b = jnp.exp(lgb * (c - pos))
    k_decay_f = jnp.exp(lgf * (c - 1.0 - pos))
    chunk_decay = jnp.exp(jnp.full((1, RET_DV), lgf * c, F32))
    gn = gn_ref[...]

    for ci in range(RET_TILE // RET_CHUNK):
        rows = slice(ci * c, (ci + 1) * c)
        q = q_ref[0, rows, :]
        k = k_ref[0, rows, :]
        v = v_ref[0, rows, :]
        qf32 = q.astype(F32)
        sc = lax.dot_general(q, k, (((1,), (1,)), ((), ())), preferred_element_type=F32)
        y = jnp.dot((sc * decay).astype(BF16), v, preferred_element_type=F32)
        y = y + jnp.dot((qf32 * q_decay_f).astype(BF16), state[...].astype(BF16), preferred_element_type=F32)
        y = y + jnp.dot((qf32 * q_decay_b).astype(BF16), sb_ref[0, 0, ci], preferred_element_type=F32)
        kd = (k.astype(F32) * k_decay_f).astype(BF16)
        state[...] = state[...] * chunk_decay + _kt_v(kd, v)
        yn = _rms(y, gn)
        o_ref[0, rows, :] = (_silu(g_ref[0, rows, :].astype(F32)) * yn).astype(BF16)


def _ret_main(z, sb, lgf, lgb, gn):
    b, s, _ = z.shape
    n_tiles = s // RET_TILE
    per_tile = RET_TILE // RET_CHUNK
    k_off = RET_QK_WIDTH // RET_DK
    v_off = 2 * RET_QK_WIDTH // RET_DV
    g_off = (2 * RET_QK_WIDTH + RET_V_WIDTH) // RET_DV
    smem = pl.BlockSpec(memory_space=pltpu.SMEM)
    return pl.pallas_call(
        _ret_main_kernel,
        grid=(b, RET_HEADS, n_tiles),
        in_specs=[
            smem, smem,
            pl.BlockSpec((1, RET_TILE, RET_DK), lambda bi, h, t: (bi, t, h)),
            pl.BlockSpec((1, RET_TILE, RET_DK), lambda bi, h, t: (bi, t, k_off + h)),
            pl.BlockSpec((1, RET_TILE, RET_DV), lambda bi, h, t: (bi, t, v_off + h)),
            pl.BlockSpec((1, RET_TILE, RET_DV), lambda bi, h, t: (bi, t, g_off + h)),
            pl.BlockSpec((1, 1, per_tile, RET_DK, RET_DV), lambda bi, h, t: (bi, h, t, 0, 0)),
            pl.BlockSpec((1, RET_DV), lambda bi, h, t: (0, h)),
        ],
        out_specs=pl.BlockSpec((1, RET_TILE, RET_DV), lambda bi, h, t: (bi, t, h)),
        out_shape=jax.ShapeDtypeStruct((b, s, RET_V_WIDTH), BF16),
        scratch_shapes=[pltpu.VMEM((RET_DK, RET_DV), F32)],
        compiler_params=_params(("parallel", "parallel", "arbitrary")),
        name="retention_main",
    )(lgf, lgb, z, z, z, z, sb, gn)


def _odd_out_router_kernel(y_ref, x_ref, w_ref, ln_ref, wr_ref, xo_ref, h_ref, idx_ref, gate_ref):
    x = x_ref[...] + jnp.dot(y_ref[...], w_ref[...], preferred_element_type=F32)
    xo_ref[...] = x
    hn = _rms(x, ln_ref[...])
    h_ref[...] = hn.astype(BF16)
    logits = lax.dot_general(wr_ref[...], hn, (((1,), (1,)), ((), ())),
                             precision=lax.Precision.HIGHEST, preferred_element_type=F32)
    row = lax.broadcasted_iota(jnp.int32, logits.shape, 0).astype(F32)
    none = float(N_EXPERTS)
    m1 = jnp.max(logits, axis=0, keepdims=True)
    i1 = jnp.min(jnp.where(logits == m1, row, none), axis=0, keepdims=True)
    rest = jnp.where(row == i1, -jnp.inf, logits)
    m2 = jnp.max(rest, axis=0, keepdims=True)
    i2 = jnp.min(jnp.where(rest == m2, row, none), axis=0, keepdims=True)
    e2 = jnp.exp(m2 - m1)
    idx_ref[0:1, :] = i1.astype(jnp.int32)
    idx_ref[1:2, :] = i2.astype(jnp.int32)
    gate_ref[0:1, :] = 1.0 / (1.0 + e2)
    gate_ref[1:2, :] = e2 / (1.0 + e2)


def _odd_out_router(y, x, w_out, ln, wr_t, tm=512):
    n, d = x.shape
    full = lambda shp: pl.BlockSpec(shp, lambda i: (0,) * len(shp))
    return pl.pallas_call(
        _odd_out_router_kernel,
        grid=(n // tm,),
        in_specs=[
            pl.BlockSpec((tm, RET_V_WIDTH), lambda i: (i, 0)),
            pl.BlockSpec((tm, d), lambda i: (i, 0)),
            full((RET_V_WIDTH, d)), full((1, d)), full((N_EXPERTS, d)),
        ],
        out_specs=[
            pl.BlockSpec((tm, d), lambda i: (i, 0)),
            pl.BlockSpec((tm, d), lambda i: (i, 0)),
            pl.BlockSpec((2, tm), lambda i: (0, i)),
            pl.BlockSpec((2, tm), lambda i: (0, i)),
        ],
        out_shape=[
            jax.ShapeDtypeStruct((n, d), F32),
            jax.ShapeDtypeStruct((n, d), BF16),
            jax.ShapeDtypeStruct((2, n), jnp.int32),
            jax.ShapeDtypeStruct((2, n), F32),
        ],
        compiler_params=_params(("parallel",)),
        name="odd_out_proj_router",
    )(y, x, w_out, ln, wr_t)


MOE_TM = 1024
MOE_TF = 1792
MOE_SUB = 512


def _moe_kernel(tile_expert_ref, n_used_ref, xs_ref, gate_ref, w1_ref, w3_ref, w2_ref, o_ref, acc_ref):
    t = pl.program_id(0)
    f = pl.program_id(1)
    last = pl.num_programs(1) - 1
    used = t < n_used_ref[0]

    @pl.when(used)
    def _():
        for r0 in range(0, MOE_TM, MOE_SUB):
            rows = slice(r0, r0 + MOE_SUB)
            xs = xs_ref[rows, :]
            g = jnp.dot(xs, w1_ref[...], preferred_element_type=F32)
            u = jnp.dot(xs, w3_ref[...], preferred_element_type=F32)
            act = (_silu(g) * u).astype(BF16)
            part = jnp.dot(act, w2_ref[...], preferred_element_type=F32)

            @pl.when(f == 0)
            def _():
                acc_ref[rows, :] = part

            @pl.when(jnp.logical_and(f > 0, f < last))
            def _():
                acc_ref[rows, :] += part

            @pl.when(f == last)
            def _():
                o_ref[rows, :] = ((acc_ref[rows, :] + part) * gate_ref[rows, 0:1]).astype(BF16)

    @pl.when(jnp.logical_and(jnp.logical_not(used), f == last))
    def _():
        o_ref[...] = jnp.zeros_like(o_ref)


def _moe_experts(xs, gate_rows, tile_expert, n_used, w1, w3, w2):
    p, d = xs.shape
    n_tiles = p // MOE_TM
    n_f = D_FF_EXPERT // MOE_TF
    assert n_f >= 2 and n_f * MOE_TF == D_FF_EXPERT

    def row_tile(t, f, te, nu):
        return (jnp.minimum(t, nu[0] - 1), 0)

    def f_idx(t, f, nu):
        return jnp.where(t < nu[0], f, n_f - 1)

    grid_spec = pltpu.PrefetchScalarGridSpec(
        num_scalar_prefetch=2,
        grid=(n_tiles, n_f),
        in_specs=[
            pl.BlockSpec((MOE_TM, d), row_tile),
            pl.BlockSpec((MOE_TM, LANES), row_tile),
            pl.BlockSpec((None, d, MOE_TF), lambda t, f, te, nu: (te[t], 0, f_idx(t, f, nu))),
            pl.BlockSpec((None, d, MOE_TF), lambda t, f, te, nu: (te[t], 0, f_idx(t, f, nu))),
            pl.BlockSpec((None, MOE_TF, d), lambda t, f, te, nu: (te[t], f_idx(t, f, nu), 0)),
        ],
        out_specs=pl.BlockSpec((MOE_TM, d), lambda t, f, te, nu: (t, 0)),
        scratch_shapes=[pltpu.VMEM((MOE_TM, d), F32)],
    )
    return pl.pallas_call(
        _moe_kernel,
        grid_spec=grid_spec,
        out_shape=jax.ShapeDtypeStruct((p, d), BF16),
        compiler_params=_params(("arbitrary", "arbitrary")),
        name="moe_experts",
    )(tile_expert, n_used, xs, gate_rows, w1, w3, w2)


def _combine_kernel(x_ref, y_ref, o_ref):
    o_ref[...] = x_ref[...] + y_ref[0].astype(F32) + y_ref[1].astype(F32)


def _combine(x, y2, tm=1024):
    n, d = x.shape
    return pl.pallas_call(
        _combine_kernel,
        grid=(n // tm,),
        in_specs=[pl.BlockSpec((tm, d), lambda i: (i, 0)), pl.BlockSpec((2, tm, d), lambda i: (0, i, 0))],
        out_specs=pl.BlockSpec((tm, d), lambda i: (i, 0)),
        out_shape=jax.ShapeDtypeStruct((n, d), F32),
        compiler_params=_params(("parallel",)),
        name="moe_combine",
    )(x, y2)


def _route(idx, gates, n):
    e_flat = idx.reshape(-1)
    n_pairs = e_flat.shape[0]
    pair_id = jnp.arange(n_pairs, dtype=jnp.int32)
    order = (jnp.sort(e_flat * n_pairs + pair_id) % n_pairs).astype(jnp.int32)
    rank = jnp.argsort(order).astype(jnp.int32)
    counts = jnp.sum(e_flat[None, :] == jnp.arange(N_EXPERTS, dtype=jnp.int32)[:, None], axis=1).astype(jnp.int32)
    starts = jnp.cumsum(counts) - counts
    tiles_per_expert = (counts + MOE_TM - 1) // MOE_TM
    tile_ends = jnp.cumsum(tiles_per_expert)
    padded_starts = (tile_ends - tiles_per_expert) * MOE_TM
    dest = rank + (padded_starts - starts)[e_flat]
    n_rows = n_pairs + N_EXPERTS * MOE_TM
    n_tiles = n_rows // MOE_TM
    tile_expert = jnp.minimum(
        jnp.sum(jnp.arange(n_tiles, dtype=jnp.int32)[:, None] >= tile_ends[None, :], axis=1), N_EXPERTS - 1
    ).astype(jnp.int32)
    row_expert = jnp.repeat(tile_expert, MOE_TM)
    row_in_expert = jnp.arange(n_rows, dtype=jnp.int32) - padded_starts[row_expert]
    row_valid = row_in_expert < counts[row_expert]
    row_pair = order[jnp.clip(starts[row_expert] + row_in_expert, 0, n_pairs - 1)]
    src_token = jnp.where(row_valid, row_pair % n, 0).astype(jnp.int32)
    row_gate = jnp.where(row_valid, gates.reshape(-1)[row_pair], 0.0)
    n_used = tile_ends[-1:].astype(jnp.int32)
    return src_token, row_gate, dest, tile_expert, n_used


def _channel_dft():
    c = jnp.arange(FNET_GROUP_DIM, dtype=jnp.int32)
    ang = ((c[:, None] * c[None, :]) % FNET_GROUP_DIM).astype(F32) * (2.0 * math.pi / FNET_GROUP_DIM)
    eye = jnp.eye(FNET_GROUPS, dtype=F32)
    scale = FNET_GROUP_DIM ** -0.5
    return jnp.concatenate([jnp.kron(eye, jnp.cos(ang)) * scale, -jnp.kron(eye, jnp.sin(ang)) * scale], axis=1).astype(BF16)


def _sequence_dft(s):
    hi = s // 64
    k = jnp.arange(s, dtype=jnp.int32)
    j_hi = jnp.arange(hi, dtype=jnp.int32)
    j_lo = jnp.arange(64, dtype=jnp.int32)
    w = 2.0 * math.pi / s
    ang_a = (((j_hi[:, None] * k[None, :]) % hi) * 64).astype(F32) * w
    ang_b = ((j_lo[:, None] * k[None, :]) % s).astype(F32) * w
    scale = s ** -0.5
    ca, sa = jnp.cos(ang_a) * scale, jnp.sin(ang_a) * scale
    cb, sb = jnp.cos(ang_b), jnp.sin(ang_b)
    ca2 = jnp.concatenate([ca, ca], axis=1)[:, None, :]
    sa2 = jnp.concatenate([sa, sa], axis=1)[:, None, :]
    p = jnp.concatenate([cb, sb], axis=1)[None, :, :]
    q = jnp.concatenate([sb, -cb], axis=1)[None, :, :]
    return (ca2 * p - sa2 * q).reshape(s, 2 * s).astype(BF16)


def _trunk_to_router(x, p):
    b, s, d = x.shape
    n = b * s
    y, qkv = _even_in(x, p["ln_mix_e"], p["w_in_e"], p["chan_dft"], p["head_sum"], p["qn"], p["kn"])
    fmix = _matmul(_sequence_dft(s), y.reshape(2 * s, b * FNET_WIDTH))
    att = _attention(qkv, p["slopes"])
    x = _even_out(x, fmix, att, p["w_out_e"])
    x = _swiglu(x.reshape(n, d), p["ln_ffn_e"], p["w_gate_e"], p["w_up_e"], p["w_down_e"])
    z = _odd_in(x, p["ln_mix_o"], p["w_in_o"]).reshape(b, s, ODD_IN)
    sb = _ret_bwd_states(z, p["lgb"])
    yr = _ret_main(z, sb, p["lgf"], p["lgb"], p["ret_gn"])
    x, h, idx, gates = _odd_out_router(yr.reshape(n, RET_V_WIDTH), x, p["w_out_o"], p["ln_ffn_o"], p["w_router_t"])
    return x, h, _route(idx, gates, n)


def _dispatch(h, route):
    src_token, row_gate = route[0], route[1]
    xs = h.at[src_token].get(mode="promise_in_bounds")
    return xs, jnp.broadcast_to(row_gate[:, None], (row_gate.shape[0], LANES))


def _experts(xs, gate_rows, route, p):
    return _moe_experts(xs, gate_rows, route[3], route[4], p["moe_w1"], p["moe_w3"], p["moe_w2"])


def _collect(ye, route, n):
    return ye.at[route[2]].get(mode="promise_in_bounds").reshape(2, n, D_MODEL)


def _both_trunks(x_a, x_b, p):
    outs = []
    for x in (x_a, x_b):
        n = x.shape[0] * x.shape[1]
        xr, h, route = _trunk_to_router(x, p)
        xs, gate_rows = _dispatch(h, route)
        ye = _experts(xs, gate_rows, route, p)
        outs.append(_combine(xr, _collect(ye, route, n)).reshape(x.shape))
    return tuple(outs)


def kernel(x_prompt, x_sample, ln_mix_e, w_in_e, w_out_e, qn_e, kn_e, ln_ffn_e, w_gate_e, w_up_e, w_down_e, ln_mix_o, w_in_o, w_out_o, logdecay_fwd, logdecay_bwd, ret_gn, ln_ffn_o, w_router, moe_w1, moe_w3, moe_w2):
    row = lambda w: w.reshape(1, -1).astype(F32)
    k_scale = jnp.concatenate([
        jnp.ones((RET_QK_WIDTH,), F32), jnp.full((RET_QK_WIDTH,), RET_DK ** -0.5, F32),
        jnp.ones((2 * RET_V_WIDTH,), F32)])
    head_id = jnp.arange(LANES, dtype=jnp.int32) // ATT_HEAD_DIM
    p = {
        "ln_mix_e": row(ln_mix_e[0]),
        "w_in_e": w_in_e[0].astype(BF16),
        "w_out_e": w_out_e[0].astype(BF16),
        "qn": row(jnp.tile(qn_e[0], 2) * (ATT_HEAD_DIM ** -0.5)),
        "kn": row(jnp.tile(kn_e[0], 2)),
        "ln_ffn_e": row(ln_ffn_e[0]),
        "w_gate_e": w_gate_e[0].astype(BF16),
        "w_up_e": w_up_e[0].astype(BF16),
        "w_down_e": w_down_e[0].astype(BF16),
        "ln_mix_o": row(ln_mix_o[0]),
        "w_in_o": (w_in_o[0] * k_scale[None, :]).astype(BF16),
        "w_out_o": w_out_o[0].astype(BF16),
        "lgf": logdecay_fwd[0].astype(F32),
        "lgb": logdecay_bwd[0].astype(F32),
        "ret_gn": row(ret_gn[0]),
        "ln_ffn_o": row(ln_ffn_o[0]),
        "w_router_t": w_router[0].T.astype(F32),
        "moe_w1": moe_w1[0].astype(BF16),
        "moe_w3": moe_w3[0].astype(BF16),
        "moe_w2": moe_w2[0].astype(BF16),
        "chan_dft": _channel_dft(),
        "head_sum": (head_id[:, None] == head_id[None, :]).astype(BF16),
        "slopes": jnp.exp2(-8.0 * (jnp.arange(ATT_HEADS, dtype=F32) + 1.0) / ATT_HEADS),
    }
    return _both_trunks(x_prompt, x_sample, p)
```

```python
import functools
import math

import jax
import jax.numpy as jnp
from jax import lax
from jax.experimental import pallas as pl
from jax.experimental.pallas import tpu as pltpu

D_MODEL = 1024
FNET_GROUPS = 4
FNET_GROUP_DIM = 64
FNET_WIDTH = 256
ATT_HEAD_DIM = 64
ATT_HEADS = 12
ATT_WIDTH = 768
ATT_HALF_WINDOW = 64
DILATIONS = (1, 4, 16)
EVEN_IN = FNET_WIDTH + 3 * ATT_WIDTH
RET_HEADS = 4
RET_DK = 256
RET_DV = 512
RET_QK_WIDTH = 1024
RET_V_WIDTH = 2048
ODD_IN = 6144
D_FF = 2816
N_EXPERTS = 8
D_FF_EXPERT = 3584
EPS = 1e-6
NEG = -1e30

LANES = 128
VMEM_LIMIT_BYTES = 56 * 2**20

F32 = jnp.float32
BF16 = jnp.bfloat16


def _params(semantics):
    return pltpu.CompilerParams(dimension_semantics=semantics, vmem_limit_bytes=VMEM_LIMIT_BYTES)


def _rms(x, w):
    return x * lax.rsqrt(jnp.mean(x * x, axis=-1, keepdims=True) + EPS) * w


def _silu(x):
    return x / (1.0 + jnp.exp(-x))


QKV_CLASSES = 16


def _even_in_kernel(x_ref, ln_ref, w_ref, cd_ref, hs_ref, qn_ref, kn_ref, y_ref, qkv_ref, base_ref, zs_ref):
    h = _rms(x_ref[0], ln_ref[...]).astype(BF16)
    u = jnp.dot(h, w_ref[:, 0:FNET_WIDTH], preferred_element_type=F32).astype(BF16)
    yy = jnp.dot(u, cd_ref[...], preferred_element_type=F32)
    y_ref[0] = yy[:, :FNET_WIDTH].astype(BF16)
    y_ref[1] = yy[:, FNET_WIDTH:].astype(BF16)
    for part, n_ref in ((0, qn_ref), (1, kn_ref)):
        c0 = FNET_WIDTH + part * ATT_WIDTH
        z = jnp.dot(h, w_ref[:, c0:c0 + ATT_WIDTH], preferred_element_type=F32)
        for j in range(ATT_WIDTH // LANES):
            zj = z[:, j * LANES:(j + 1) * LANES]
            ss = jnp.dot((zj * zj).astype(BF16), hs_ref[...], preferred_element_type=F32)
            zn = zj * lax.rsqrt(ss * (1.0 / ATT_HEAD_DIM) + EPS) * n_ref[...]
            cols = slice(part * ATT_WIDTH + j * LANES, part * ATT_WIDTH + (j + 1) * LANES)
            qkv_ref[0, :, cols] = zn.astype(BF16)
            zs_ref[part * (ATT_WIDTH // LANES) + j] = zn
    c0 = FNET_WIDTH + 2 * ATT_WIDTH
    v = jnp.dot(h, w_ref[:, c0:c0 + ATT_WIDTH], preferred_element_type=F32)
    qkv_ref[0, :, 2 * ATT_WIDTH:3 * ATT_WIDTH] = v.astype(BF16)
    for j in range(ATT_WIDTH // LANES):
        zs_ref[2 * (ATT_WIDTH // LANES) + j] = v[:, j * LANES:(j + 1) * LANES]
    rows_per_class = zs_ref.shape[1] // QKV_CLASSES
    for rho in range(QKV_CLASSES):
        for c in range(zs_ref.shape[0]):
            base_ref[0, rho, :, c * LANES:(c + 1) * LANES] = (
                zs_ref[c, pl.ds(rho, rows_per_class, stride=QKV_CLASSES), :].astype(BF16))


def _even_in(x, ln, w_in, cd, hs, qn, kn, tm=512):
    b, s, d = x.shape
    full = lambda shp: pl.BlockSpec(shp, lambda bi, i: (0,) * len(shp))
    return pl.pallas_call(
        _even_in_kernel,
        grid=(b, s // tm),
        in_specs=[
            pl.BlockSpec((1, tm, d), lambda bi, i: (bi, i, 0)),
            full((1, d)), full((d, EVEN_IN)), full((FNET_WIDTH, 2 * FNET_WIDTH)), full((LANES, LANES)),
            full((1, LANES)), full((1, LANES)),
        ],
        out_specs=[
            pl.BlockSpec((2, tm, FNET_WIDTH), lambda bi, i: (0, i, bi)),
            pl.BlockSpec((1, tm, 3 * ATT_WIDTH), lambda bi, i: (bi, i, 0)),
            pl.BlockSpec((1, QKV_CLASSES, tm // QKV_CLASSES, 3 * ATT_WIDTH), lambda bi, i: (bi, 0, i, 0)),
        ],
        out_shape=[
            jax.ShapeDtypeStruct((2, s, b * FNET_WIDTH), BF16),
            jax.ShapeDtypeStruct((b, s, 3 * ATT_WIDTH), BF16),
            jax.ShapeDtypeStruct((b, QKV_CLASSES, s // QKV_CLASSES, 3 * ATT_WIDTH), BF16),
        ],
        scratch_shapes=[pltpu.VMEM((3 * ATT_WIDTH // LANES, tm, LANES), F32)],
        compiler_params=_params(("parallel", "parallel")),
        name="even_in_proj",
    )(x, ln, w_in, cd, hs, qn, kn)


def _matmul_kernel(a_ref, b_ref, o_ref, acc_ref):
    @pl.when(pl.program_id(2) == 0)
    def _():
        acc_ref[...] = jnp.zeros_like(acc_ref)

    acc_ref[...] += jnp.dot(a_ref[...], b_ref[...], preferred_element_type=F32)

    @pl.when(pl.program_id(2) == pl.num_programs(2) - 1)
    def _():
        o_ref[...] = acc_ref[...].astype(o_ref.dtype)


def _matmul(a, b, tm=1024, tn=2048, tk=1024):
    m, k = a.shape
    _, n = b.shape
    tm, tn, tk = min(tm, m), min(tn, n), min(tk, k)
    return pl.pallas_call(
        _matmul_kernel,
        grid=(m // tm, n // tn, k // tk),
        in_specs=[pl.BlockSpec((tm, tk), lambda i, j, kk: (i, kk)),
                  pl.BlockSpec((tk, tn), lambda i, j, kk: (kk, j))],
        out_specs=pl.BlockSpec((tm, tn), lambda i, j, kk: (i, j)),
        out_shape=jax.ShapeDtypeStruct((m, n), BF16),
        scratch_shapes=[pltpu.VMEM((tm, tn), F32)],
        compiler_params=_params(("parallel", "parallel", "arbitrary")),
        name="seq_dft_matmul",
    )(a, b)


ATT_QBLK = 128
ATT_CONV_ROWS = 512
ATT_GROUP = 16
ATT_MAX_DILATION = 16
ATT_DOUBLE_BUFFER_BYTES = 16 * 2**20


def _attn_kernel(slopes_ref, k_ref, v_ref, qb_ref, kb_ref, vb_ref, o_ref, xf, qs, kd, vd, bias, sc_buf, p_buf, m_buf,
                 m_s, l_s, a_s, *, seq):
    pair = pl.program_id(1)
    lane = lax.broadcasted_iota(jnp.int32, (1, LANES), 1)
    first_head = lane < ATT_HEAD_DIM
    slopes = (slopes_ref[2 * pair], slopes_ref[2 * pair + 1])
    base_len = seq // ATT_MAX_DILATION

    def conv_rows(c):
        return pl.ds(pl.multiple_of(c * ATT_CONV_ROWS, ATT_CONV_ROWS), ATT_CONV_ROWS)

    def store_split(dst_ref, rows, x, fill):
        other = jnp.full_like(x, fill)
        dst_ref[0, rows, :] = jnp.where(first_head, x, other)
        dst_ref[1, rows, :] = jnp.where(first_head, other, x)

    def class_major_rows(t, r):
        blocks_per_class = seq // r // ATT_QBLK
        rho = t // blocks_per_class
        l0 = (t % blocks_per_class) * ATT_QBLK
        return pl.ds(rho + r * l0, ATT_QBLK, stride=r)

    def widen_q(c, carry):
        qs[conv_rows(c), :] = qb_ref[0, conv_rows(c), :].astype(F32)
        return carry

    lax.fori_loop(0, seq // ATT_CONV_ROWS, widen_q, 0)

    for pattern, r in enumerate(DILATIONS):
        sub_len = seq // r
        n_keys = min(2 * ATT_QBLK, sub_len)
        blocks_per_class = sub_len // ATT_QBLK
        n_runs = ATT_MAX_DILATION // r
        run_len = ATT_QBLK // n_runs

        n_kruns = 1 if r == 1 else n_runs
        krun_len = n_keys // n_kruns
        if r == 1:
            def copy(c, carry):
                store_split(kd, conv_rows(c), k_ref[0, conv_rows(c), :], 0.0)
                store_split(vd, conv_rows(c), v_ref[0, conv_rows(c), :], 1.0)
                return carry

            lax.fori_loop(0, seq // ATT_CONV_ROWS, copy, 0)
        elif pattern == 1:
            def copy_base(c, carry):
                store_split(kd, conv_rows(c), kb_ref[0, conv_rows(c), :], 0.0)
                store_split(vd, conv_rows(c), vb_ref[0, conv_rows(c), :], 1.0)
                return carry

            lax.fori_loop(0, seq // ATT_CONV_ROWS, copy_base, 0)

        n_idx = lax.broadcasted_iota(jnp.int32, (ATT_QBLK, n_keys), 0)
        q_rel = n_runs * (n_idx % run_len) + n_idx // run_len
        c_idx = lax.broadcasted_iota(jnp.int32, (ATT_QBLK, n_keys), 1)
        k_rel = n_kruns * (c_idx % krun_len) + c_idx // krun_len
        offsets = (0, ATT_HALF_WINDOW, 2 * ATT_HALF_WINDOW) if blocks_per_class > 1 else (0,)
        for variant, off in enumerate(offsets):
            dist = jnp.abs(k_rel - off - q_rel)
            for h in range(2):
                scaled = (-slopes[h] * float(r)) * dist.astype(F32)
                bias[h, variant, :, 0:n_keys] = jnp.where(dist <= ATT_HALF_WINDOW, scaled, NEG)

        def geometry(t, r=r, blocks_per_class=blocks_per_class, n_runs=n_runs, run_len=run_len,
                     n_kruns=n_kruns, krun_len=krun_len, n_keys=n_keys):
            rho = t // blocks_per_class
            lb = t % blocks_per_class
            if blocks_per_class > 1:
                variant = jnp.where(lb == 0, 0, jnp.where(lb == blocks_per_class - 1, 2, 1))
            else:
                variant = 0
            runs = [pl.ds(pl.multiple_of((rho + r * j) * base_len + lb * run_len, 8), run_len)
                    for j in range(n_runs)]
            if r == 1:
                k_runs = [pl.ds(pl.multiple_of(t * ATT_QBLK - variant * ATT_HALF_WINDOW, ATT_HALF_WINDOW), n_keys)]
            else:
                start = lb * run_len - variant * (ATT_HALF_WINDOW // n_kruns)
                k_runs = [pl.ds(pl.multiple_of((rho + r * j) * base_len + start, 16), krun_len)
                          for j in range(n_kruns)]
            return variant, k_runs, runs

        def load_runs(ref, runs):
            return jnp.concatenate([ref[rows, :] for rows in runs], axis=0) if len(runs) > 1 else ref[runs[0], :]

        def store_runs(ref, runs, x, run_len=run_len):
            for j, rows in enumerate(runs):
                ref[rows, :] = x[j * run_len:(j + 1) * run_len, :]

        def group(g, carry, n_keys=n_keys, pattern=pattern, geometry=geometry, load_runs=load_runs,
                  store_runs=store_runs):
            for i in range(ATT_GROUP):
                variant, k_runs, runs = geometry(g * ATT_GROUP + i)
                q = load_runs(qs, runs).astype(BF16)
                for h in range(2):
                    sc = lax.dot_general(q, load_runs(kd.at[h], k_runs), (((1,), (1,)), ((), ())),
                                         preferred_element_type=F32)
                    sc_buf[i, h, :, 0:n_keys] = sc + bias[h, variant, :, 0:n_keys]
            for i in range(ATT_GROUP):
                ms = []
                for h in range(2):
                    sc = sc_buf[i, h, :, 0:n_keys]
                    m = jnp.max(sc, axis=-1, keepdims=True)
                    p_buf[i, h, :, 0:n_keys] = jnp.exp(sc - m).astype(BF16)
                    ms.append(m)
                m_buf[i] = jnp.where(first_head, ms[0], ms[1])
            for i in range(ATT_GROUP):
                _, k_runs, runs = geometry(g * ATT_GROUP + i)
                pv0 = jnp.dot(p_buf[i, 0, :, 0:n_keys], load_runs(vd.at[0], k_runs), preferred_element_type=F32)
                pv1 = jnp.dot(p_buf[i, 1, :, 0:n_keys], load_runs(vd.at[1], k_runs), preferred_element_type=F32)
                a_new = jnp.where(first_head, pv0, pv1)
                l_new = pltpu.roll(jnp.where(first_head, pv1, pv0), ATT_HEAD_DIM, 1)
                m_new = m_buf[i]
                if pattern == 0:
                    store_runs(m_s, runs, m_new)
                    store_runs(l_s, runs, l_new)
                    store_runs(a_s, runs, a_new)
                else:
                    m_old = load_runs(m_s, runs)
                    m_tot = jnp.maximum(m_old, m_new)
                    w_old = jnp.exp(m_old - m_tot)
                    w_new = jnp.exp(m_new - m_tot)
                    store_runs(m_s, runs, m_tot)
                    store_runs(l_s, runs, w_old * load_runs(l_s, runs) + w_new * l_new)
                    store_runs(a_s, runs, w_old * load_runs(a_s, runs) + w_new * a_new)
            return carry

        lax.fori_loop(0, seq // (ATT_QBLK * ATT_GROUP), group, 0)

    def scatter(t, carry):
        rows = pl.ds(pl.multiple_of(t * ATT_QBLK, ATT_QBLK), ATT_QBLK)
        xf[class_major_rows(t, ATT_MAX_DILATION), :] = a_s[rows, :] / l_s[rows, :]
        return carry

    lax.fori_loop(0, seq // ATT_QBLK, scatter, 0)

    def finish(c, carry):
        o_ref[0, conv_rows(c), :] = xf[conv_rows(c), :].astype(BF16)
        return carry

    lax.fori_loop(0, seq // ATT_CONV_ROWS, finish, 0)


def _attention(qkv, qkv_base, slopes):
    b, s, _ = qkv.shape
    n_pairs = ATT_WIDTH // LANES
    mode = {} if s * LANES * 2 * 5 * 2 <= ATT_DOUBLE_BUFFER_BYTES else {"pipeline_mode": pl.Buffered(1)}
    blk = lambda off: pl.BlockSpec((1, s, LANES), lambda bi, j, off=off: (bi, 0, off + j), **mode)
    return pl.pallas_call(
        functools.partial(_attn_kernel, seq=s),
        grid=(b, n_pairs),
        in_specs=[pl.BlockSpec(memory_space=pltpu.SMEM), blk(n_pairs), blk(2 * n_pairs),
                  blk(0), blk(n_pairs), blk(2 * n_pairs)],
        out_specs=pl.BlockSpec((1, s, LANES), lambda bi, j: (bi, 0, j)),
        out_shape=jax.ShapeDtypeStruct((b, s, ATT_WIDTH), BF16),
        scratch_shapes=[
            pltpu.VMEM((s, LANES), F32),
            pltpu.VMEM((s, LANES), F32),
            pltpu.VMEM((2, s, LANES), BF16),
            pltpu.VMEM((2, s, LANES), BF16),
            pltpu.VMEM((2, 3, ATT_QBLK, 2 * ATT_QBLK), F32),
            pltpu.VMEM((ATT_GROUP, 2, ATT_QBLK, 2 * ATT_QBLK), F32),
            pltpu.VMEM((ATT_GROUP, 2, ATT_QBLK, 2 * ATT_QBLK), BF16),
            pltpu.VMEM((ATT_GROUP, ATT_QBLK, LANES), F32),
            pltpu.VMEM((s, LANES), F32),
            pltpu.VMEM((s, LANES), F32),
            pltpu.VMEM((s, LANES), F32),
        ],
        compiler_params=_params(("parallel", "parallel")),
        name="dilated_attention",
    )(slopes, qkv, qkv, qkv_base, qkv_base, qkv_base)


def _even_out_kernel(x_ref, f_ref, a_ref, w_ref, o_ref):
    acc = x_ref[0]
    acc = acc + jnp.dot(f_ref[...], w_ref[0:FNET_WIDTH, :], preferred_element_type=F32)
    acc = acc + jnp.dot(a_ref[0], w_ref[FNET_WIDTH:, :], preferred_element_type=F32)
    o_ref[0] = acc


def _even_out(x, fmix, att, w_out, tm=512):
    b, s, d = x.shape
    return pl.pallas_call(
        _even_out_kernel,
        grid=(b, s // tm),
        in_specs=[
            pl.BlockSpec((1, tm, d), lambda bi, i: (bi, i, 0)),
            pl.BlockSpec((tm, FNET_WIDTH), lambda bi, i: (i, bi)),
            pl.BlockSpec((1, tm, ATT_WIDTH), lambda bi, i: (bi, i, 0)),
            pl.BlockSpec((d, d), lambda bi, i: (0, 0)),
        ],
        out_specs=pl.BlockSpec((1, tm, d), lambda bi, i: (bi, i, 0)),
        out_shape=jax.ShapeDtypeStruct((b, s, d), F32),
        compiler_params=_params(("parallel", "parallel")),
        name="even_out_proj",
    )(x, fmix, att, w_out)


def _resident(shape):
    return pl.BlockSpec(shape, lambda *_: (0,) * len(shape), pipeline_mode=pl.Buffered(1))


def _swiglu_kernel(x_ref, ln_ref, wg_ref, wu_ref, wd_ref, o_ref):
    x = x_ref[...]
    h = _rms(x, ln_ref[...]).astype(BF16)
    g = jnp.dot(h, wg_ref[...], preferred_element_type=F32)
    u = jnp.dot(h, wu_ref[...], preferred_element_type=F32)
    act = (_silu(g) * u).astype(BF16)
    o_ref[...] = x + jnp.dot(act, wd_ref[...], preferred_element_type=F32)


def _swiglu(x, ln, wg, wu, wd, tm=512):
    n, d = x.shape
    dff = wg.shape[1]
    return pl.pallas_call(
        _swiglu_kernel,
        grid=(n // tm,),
        in_specs=[
            pl.BlockSpec((tm, d), lambda i: (i, 0)),
            _resident((1, d)), _resident((d, dff)), _resident((d, dff)), _resident((dff, d)),
        ],
        out_specs=pl.BlockSpec((tm, d), lambda i: (i, 0)),
        out_shape=jax.ShapeDtypeStruct((n, d), F32),
        compiler_params=_params(("parallel",)),
        name="swiglu_ffn",
    )(x, ln, wg, wu, wd)


ODD_IN_COLS = 2048


def _odd_in_kernel(x_ref, ln_ref, w_ref, o_ref):
    h = _rms(x_ref[...], ln_ref[...]).astype(BF16)
    for c0 in range(0, ODD_IN, ODD_IN_COLS):
        cols = slice(c0, c0 + ODD_IN_COLS)
        o_ref[:, cols] = jnp.dot(h, w_ref[:, cols], preferred_element_type=F32).astype(BF16)


def _odd_in(x, ln, w_in, tm=512):
    n, d = x.shape
    return pl.pallas_call(
        _odd_in_kernel,
        grid=(n // tm,),
        in_specs=[pl.BlockSpec((tm, d), lambda i: (i, 0)), _resident((1, d)), _resident((d, ODD_IN))],
        out_specs=pl.BlockSpec((tm, ODD_IN), lambda i: (i, 0)),
        out_shape=jax.ShapeDtypeStruct((n, ODD_IN), BF16),
        compiler_params=_params(("parallel",)),
        name="odd_in_proj",
    )(x, ln, w_in)


RET_CHUNK = 256
RET_TILE = 1024


def _chunk_positions():
    return lax.broadcasted_iota(jnp.int32, (RET_CHUNK, 1), 0).astype(F32)


def _kt_v(k_scaled, v):
    return lax.dot_general(k_scaled, v, (((0,), (0,)), ((), ())), preferred_element_type=F32)


def _ret_bwd_state_kernel(lgb_ref, k_ref, v_ref, sb_ref, state):
    head = pl.program_id(1)

    @pl.when(pl.program_id(2) == 0)
    def _():
        state[...] = jnp.zeros_like(state)

    lg = lgb_ref[head]
    key_decay = jnp.exp(lg * _chunk_positions())
    chunk_decay = jnp.exp(jnp.full((1, RET_DV), lg * RET_CHUNK, F32))
    for c in reversed(range(RET_TILE // RET_CHUNK)):
        rows = slice(c * RET_CHUNK, (c + 1) * RET_CHUNK)
        sb_ref[0, 0, c] = state[...].astype(BF16)
        kd = (k_ref[0, rows, :].astype(F32) * key_decay).astype(BF16)
        state[...] = state[...] * chunk_decay + _kt_v(kd, v_ref[0, rows, :])


def _ret_bwd_states(z, lgb):
    b, s, _ = z.shape
    n = s // RET_CHUNK
    n_tiles = s // RET_TILE
    per_tile = RET_TILE // RET_CHUNK
    k_off = RET_QK_WIDTH // RET_DK
    v_off = 2 * RET_QK_WIDTH // RET_DV
    return pl.pallas_call(
        _ret_bwd_state_kernel,
        grid=(b, RET_HEADS, n_tiles),
        in_specs=[
            pl.BlockSpec(memory_space=pltpu.SMEM),
            pl.BlockSpec((1, RET_TILE, RET_DK), lambda bi, h, t: (bi, n_tiles - 1 - t, k_off + h)),
            pl.BlockSpec((1, RET_TILE, RET_DV), lambda bi, h, t: (bi, n_tiles - 1 - t, v_off + h)),
        ],
        out_specs=pl.BlockSpec((1, 1, per_tile, RET_DK, RET_DV), lambda bi, h, t: (bi, h, n_tiles - 1 - t, 0, 0)),
        out_shape=jax.ShapeDtypeStruct((b, RET_HEADS, n, RET_DK, RET_DV), BF16),
        scratch_shapes=[pltpu.VMEM((RET_DK, RET_DV), F32)],
        compiler_params=_params(("parallel", "parallel", "arbitrary")),
        name="retention_bwd_states",
    )(lgb, z, z)


def _ret_main_kernel(lgf_ref, lgb_ref, q_ref, k_ref, v_ref, g_ref, sb_ref, gn_ref, o_ref, state):
    head = pl.program_id(1)

    @pl.when(pl.program_id(2) == 0)
    def _():
        state[...] = jnp.zeros_like(state)

    lgf = lgf_ref[head]
    lgb = lgb_ref[head]
    c = RET_CHUNK
    pos = _chunk_positions()
    rel = (lax.broadcasted_iota(jnp.int32, (c, c), 0) - lax.broadcasted_iota(jnp.int32, (c, c), 1)).astype(F32)
    decay = jnp.exp(jnp.where(rel >= 0, lgf * rel, -lgb * rel))
    q_decay_f = jnp.exp(lgf * (pos + 1.0))
    q_decay_b = jnp.exp(lgb * (c - pos))
    k_decay_f = jnp.exp(lgf * (c - 1.0 - pos))
    chunk_decay = jnp.exp(jnp.full((1, RET_DV), lgf * c, F32))
    gn = gn_ref[...]

    for ci in range(RET_TILE // RET_CHUNK):
        rows = slice(ci * c, (ci + 1) * c)
        q = q_ref[0, rows, :]
        k = k_ref[0, rows, :]
        v = v_ref[0, rows, :]
        qf32 = q.astype(F32)
        sc = lax.dot_general(q, k, (((1,), (1,)), ((), ())), preferred_element_type=F32)
        y = jnp.dot((sc * decay).astype(BF16), v, preferred_element_type=F32)
        y = y + jnp.dot((qf32 * q_decay_f).astype(BF16), state[...].astype(BF16), preferred_element_type=F32)
        y = y + jnp.dot((qf32 * q_decay_b).astype(BF16), sb_ref[0, 0, ci], preferred_element_type=F32)
        kd = (k.astype(F32) * k_decay_f).astype(BF16)
        state[...] = state[...] * chunk_decay + _kt_v(kd, v)
        yn = _rms(y, gn)
        o_ref[0, rows, :] = (_silu(g_ref[0, rows, :].astype(F32)) * yn).astype(BF16)


def _ret_main(z, sb, lgf, lgb, gn):
    b, s, _ = z.shape
    n_tiles = s // RET_TILE
    per_tile = RET_TILE // RET_CHUNK
    k_off = RET_QK_WIDTH // RET_DK
    v_off = 2 * RET_QK_WIDTH // RET_DV
    g_off = (2 * RET_QK_WIDTH + RET_V_WIDTH) // RET_DV
    smem = pl.BlockSpec(memory_space=pltpu.SMEM)
    return pl.pallas_call(
        _ret_main_kernel,
        grid=(b, RET_HEADS, n_tiles),
        in_specs=[
            smem, smem,
            pl.BlockSpec((1, RET_TILE, RET_DK), lambda bi, h, t: (bi, t, h)),
            pl.BlockSpec((1, RET_TILE, RET_DK), lambda bi, h, t: (bi, t, k_off + h)),
            pl.BlockSpec((1, RET_TILE, RET_DV), lambda bi, h, t: (bi, t, v_off + h)),
            pl.BlockSpec((1, RET_TILE, RET_DV), lambda bi, h, t: (bi, t, g_off + h)),
            pl.BlockSpec((1, 1, per_tile, RET_DK, RET_DV), lambda bi, h, t: (bi, h, t, 0, 0)),
            pl.BlockSpec((1, RET_DV), lambda bi, h, t: (0, h)),
        ],
        out_specs=pl.BlockSpec((1, RET_TILE, RET_DV), lambda bi, h, t: (bi, t, h)),
        out_shape=jax.ShapeDtypeStruct((b, s, RET_V_WIDTH), BF16),
        scratch_shapes=[pltpu.VMEM((RET_DK, RET_DV), F32)],
        compiler_params=_params(("parallel", "parallel", "arbitrary")),
        name="retention_main",
    )(lgf, lgb, z, z, z, z, sb, gn)


def _odd_out_router_kernel(y_ref, x_ref, w_ref, ln_ref, wr_ref, xo_ref, h_ref, idx_ref, gate_ref):
    x = x_ref[...] + jnp.dot(y_ref[...], w_ref[...], preferred_element_type=F32)
    xo_ref[...] = x
    hn = _rms(x, ln_ref[...])
    h_ref[...] = hn.astype(BF16)
    logits = lax.dot_general(wr_ref[...], hn, (((1,), (1,)), ((), ())),
                             precision=lax.Precision.HIGHEST, preferred_element_type=F32)
    row = lax.broadcasted_iota(jnp.int32, logits.shape, 0).astype(F32)
    none = float(N_EXPERTS)
    m1 = jnp.max(logits, axis=0, keepdims=True)
    i1 = jnp.min(jnp.where(logits == m1, row, none), axis=0, keepdims=True)
    rest = jnp.where(row == i1, -jnp.inf, logits)
    m2 = jnp.max(rest, axis=0, keepdims=True)
    i2 = jnp.min(jnp.where(rest == m2, row, none), axis=0, keepdims=True)
    e2 = jnp.exp(m2 - m1)
    idx_ref[0:1, :] = i1.astype(jnp.int32)
    idx_ref[1:2, :] = i2.astype(jnp.int32)
    gate_ref[0:1, :] = 1.0 / (1.0 + e2)
    gate_ref[1:2, :] = e2 / (1.0 + e2)


def _odd_out_router(y, x, w_out, ln, wr_t, tm=512):
    n, d = x.shape
    full = lambda shp: pl.BlockSpec(shp, lambda i: (0,) * len(shp))
    return pl.pallas_call(
        _odd_out_router_kernel,
        grid=(n // tm,),
        in_specs=[
            pl.BlockSpec((tm, RET_V_WIDTH), lambda i: (i, 0)),
            pl.BlockSpec((tm, d), lambda i: (i, 0)),
            full((RET_V_WIDTH, d)), full((1, d)), full((N_EXPERTS, d)),
        ],
        out_specs=[
            pl.BlockSpec((tm, d), lambda i: (i, 0)),
            pl.BlockSpec((tm, d), lambda i: (i, 0)),
            pl.BlockSpec((2, tm), lambda i: (0, i)),
            pl.BlockSpec((2, tm), lambda i: (0, i)),
        ],
        out_shape=[
            jax.ShapeDtypeStruct((n, d), F32),
            jax.ShapeDtypeStruct((n, d), BF16),
            jax.ShapeDtypeStruct((2, n), jnp.int32),
            jax.ShapeDtypeStruct((2, n), F32),
        ],
        compiler_params=_params(("parallel",)),
        name="odd_out_proj_router",
    )(y, x, w_out, ln, wr_t)


MOE_TM = 1024
MOE_TF = 1792
MOE_SUB = 512


def _moe_kernel(tile_expert_ref, n_used_ref, xs_ref, gate_ref, w1_ref, w3_ref, w2_ref, o_ref, acc_ref):
    t = pl.program_id(0)
    f = pl.program_id(1)
    last = pl.num_programs(1) - 1
    used = t < n_used_ref[0]

    @pl.when(used)
    def _():
        for r0 in range(0, MOE_TM, MOE_SUB):
            rows = slice(r0, r0 + MOE_SUB)
            xs = xs_ref[rows, :]
            g = jnp.dot(xs, w1_ref[...], preferred_element_type=F32)
            u = jnp.dot(xs, w3_ref[...], preferred_element_type=F32)
            act = (_silu(g) * u).astype(BF16)
            part = jnp.dot(act, w2_ref[...], preferred_element_type=F32)

            @pl.when(f == 0)
            def _():
                acc_ref[rows, :] = part

            @pl.when(jnp.logical_and(f > 0, f < last))
            def _():
                acc_ref[rows, :] += part

            @pl.when(f == last)
            def _():
                o_ref[rows, :] = ((acc_ref[rows, :] + part) * gate_ref[rows, 0:1]).astype(BF16)

    @pl.when(jnp.logical_and(jnp.logical_not(used), f == last))
    def _():
        o_ref[...] = jnp.zeros_like(o_ref)


def _moe_experts(xs, gate_rows, tile_expert, n_used, w1, w3, w2):
    p, d = xs.shape
    n_tiles = p // MOE_TM
    n_f = D_FF_EXPERT // MOE_TF
    assert n_f >= 2 and n_f * MOE_TF == D_FF_EXPERT

    def row_tile(t, f, te, nu):
        return (jnp.minimum(t, nu[0] - 1), 0)

    def f_idx(t, f, nu):
        return jnp.where(t < nu[0], f, n_f - 1)

    grid_spec = pltpu.PrefetchScalarGridSpec(
        num_scalar_prefetch=2,
        grid=(n_tiles, n_f),
        in_specs=[
            pl.BlockSpec((MOE_TM, d), row_tile),
            pl.BlockSpec((MOE_TM, LANES), row_tile),
            pl.BlockSpec((None, d, MOE_TF), lambda t, f, te, nu: (te[t], 0, f_idx(t, f, nu))),
            pl.BlockSpec((None, d, MOE_TF), lambda t, f, te, nu: (te[t], 0, f_idx(t, f, nu))),
            pl.BlockSpec((None, MOE_TF, d), lambda t, f, te, nu: (te[t], f_idx(t, f, nu), 0)),
        ],
        out_specs=pl.BlockSpec((MOE_TM, d), lambda t, f, te, nu: (t, 0)),
        scratch_shapes=[pltpu.VMEM((MOE_TM, d), F32)],
    )
    return pl.pallas_call(
        _moe_kernel,
        grid_spec=grid_spec,
        out_shape=jax.ShapeDtypeStruct((p, d), BF16),
        compiler_params=_params(("arbitrary", "arbitrary")),
        name="moe_experts",
    )(tile_expert, n_used, xs, gate_rows, w1, w3, w2)


def _combine_kernel(x_ref, y_ref, o_ref):
    o_ref[...] = x_ref[...] + y_ref[0].astype(F32) + y_ref[1].astype(F32)


def _combine(x, y2, tm=1024):
    n, d = x.shape
    return pl.pallas_call(
        _combine_kernel,
        grid=(n // tm,),
        in_specs=[pl.BlockSpec((tm, d), lambda i: (i, 0)), pl.BlockSpec((2, tm, d), lambda i: (0, i, 0))],
        out_specs=pl.BlockSpec((tm, d), lambda i: (i, 0)),
        out_shape=jax.ShapeDtypeStruct((n, d), F32),
        compiler_params=_params(("parallel",)),
        name="moe_combine",
    )(x, y2)


def _route(idx, gates, n):
    e_flat = idx.reshape(-1)
    n_pairs = e_flat.shape[0]
    pair_id = jnp.arange(n_pairs, dtype=jnp.int32)
    order = (jnp.sort(e_flat * n_pairs + pair_id) % n_pairs).astype(jnp.int32)
    rank = jnp.argsort(order).astype(jnp.int32)
    counts = jnp.sum(e_flat[None, :] == jnp.arange(N_EXPERTS, dtype=jnp.int32)[:, None], axis=1).astype(jnp.int32)
    starts = jnp.cumsum(counts) - counts
    tiles_per_expert = (counts + MOE_TM - 1) // MOE_TM
    tile_ends = jnp.cumsum(tiles_per_expert)
    padded_starts = (tile_ends - tiles_per_expert) * MOE_TM
    dest = rank + (padded_starts - starts)[e_flat]
    n_rows = n_pairs + N_EXPERTS * MOE_TM
    n_tiles = n_rows // MOE_TM
    tile_expert = jnp.minimum(
        jnp.sum(jnp.arange(n_tiles, dtype=jnp.int32)[:, None] >= tile_ends[None, :], axis=1), N_EXPERTS - 1
    ).astype(jnp.int32)
    row_expert = jnp.repeat(tile_expert, MOE_TM)
    row_in_expert = jnp.arange(n_rows, dtype=jnp.int32) - padded_starts[row_expert]
    row_valid = row_in_expert < counts[row_expert]
    row_pair = order[jnp.clip(starts[row_expert] + row_in_expert, 0, n_pairs - 1)]
    src_token = jnp.where(row_valid, row_pair % n, 0).astype(jnp.int32)
    row_gate = jnp.where(row_valid, gates.reshape(-1)[row_pair], 0.0)
    n_used = tile_ends[-1:].astype(jnp.int32)
    return src_token, row_gate, dest, tile_expert, n_used


def _channel_dft():
    c = jnp.arange(FNET_GROUP_DIM, dtype=jnp.int32)
    ang = ((c[:, None] * c[None, :]) % FNET_GROUP_DIM).astype(F32) * (2.0 * math.pi / FNET_GROUP_DIM)
    eye = jnp.eye(FNET_GROUPS, dtype=F32)
    scale = FNET_GROUP_DIM ** -0.5
    return jnp.concatenate([jnp.kron(eye, jnp.cos(ang)) * scale, -jnp.kron(eye, jnp.sin(ang)) * scale], axis=1).astype(BF16)


def _sequence_dft(s):
    hi = s // 64
    k = jnp.arange(s, dtype=jnp.int32)
    j_hi = jnp.arange(hi, dtype=jnp.int32)
    j_lo = jnp.arange(64, dtype=jnp.int32)
    w = 2.0 * math.pi / s
    ang_a = (((j_hi[:, None] * k[None, :]) % hi) * 64).astype(F32) * w
    ang_b = ((j_lo[:, None] * k[None, :]) % s).astype(F32) * w
    scale = s ** -0.5
    ca, sa = jnp.cos(ang_a) * scale, jnp.sin(ang_a) * scale
    cb, sb = jnp.cos(ang_b), jnp.sin(ang_b)
    ca2 = jnp.concatenate([ca, ca], axis=1)[:, None, :]
    sa2 = jnp.concatenate([sa, sa], axis=1)[:, None, :]
    p = jnp.concatenate([cb, sb], axis=1)[None, :, :]
    q = jnp.concatenate([sb, -cb], axis=1)[None, :, :]
    return (ca2 * p - sa2 * q).reshape(s, 2 * s).astype(BF16)


def _trunk_to_router(x, p):
    b, s, d = x.shape
    n = b * s
    y, qkv, qkv_base = _even_in(x, p["ln_mix_e"], p["w_in_e"], p["chan_dft"], p["head_sum"], p["qn"], p["kn"])
    fmix = _matmul(_sequence_dft(s), y.reshape(2 * s, b * FNET_WIDTH))
    att = _attention(qkv, qkv_base.reshape(b, s, 3 * ATT_WIDTH), p["slopes"])
    x = _even_out(x, fmix, att, p["w_out_e"])
    x = _swiglu(x.reshape(n, d), p["ln_ffn_e"], p["w_gate_e"], p["w_up_e"], p["w_down_e"])
    z = _odd_in(x, p["ln_mix_o"], p["w_in_o"]).reshape(b, s, ODD_IN)
    sb = _ret_bwd_states(z, p["lgb"])
    yr = _ret_main(z, sb, p["lgf"], p["lgb"], p["ret_gn"])
    x, h, idx, gates = _odd_out_router(yr.reshape(n, RET_V_WIDTH), x, p["w_out_o"], p["ln_ffn_o"], p["w_router_t"])
    return x, h, _route(idx, gates, n)


def _dispatch(h, route):
    src_token, row_gate = route[0], route[1]
    xs = h.at[src_token].get(mode="promise_in_bounds")
    return xs, jnp.broadcast_to(row_gate[:, None], (row_gate.shape[0], LANES))


def _experts(xs, gate_rows, route, p):
    return _moe_experts(xs, gate_rows, route[3], route[4], p["moe_w1"], p["moe_w3"], p["moe_w2"])


def _collect(ye, route, n):
    return ye.at[route[2]].get(mode="promise_in_bounds").reshape(2, n, D_MODEL)


def _both_trunks(x_a, x_b, p):
    outs = []
    for x in (x_a, x_b):
        n = x.shape[0] * x.shape[1]
        xr, h, route = _trunk_to_router(x, p)
        xs, gate_rows = _dispatch(h, route)
        ye = _experts(xs, gate_rows, route, p)
        outs.append(_combine(xr, _collect(ye, route, n)).reshape(x.shape))
    return tuple(outs)


def kernel(x_prompt, x_sample, ln_mix_e, w_in_e, w_out_e, qn_e, kn_e, ln_ffn_e, w_gate_e, w_up_e, w_down_e, ln_mix_o, w_in_o, w_out_o, logdecay_fwd, logdecay_bwd, ret_gn, ln_ffn_o, w_router, moe_w1, moe_w3, moe_w2):
    row = lambda w: w.reshape(1, -1).astype(F32)
    k_scale = jnp.concatenate([
        jnp.ones((RET_QK_WIDTH,), F32), jnp.full((RET_QK_WIDTH,), RET_DK ** -0.5, F32),
        jnp.ones((2 * RET_V_WIDTH,), F32)])
    head_id = jnp.arange(LANES, dtype=jnp.int32) // ATT_HEAD_DIM
    p = {
        "ln_mix_e": row(ln_mix_e[0]),
        "w_in_e": w_in_e[0].astype(BF16),
        "w_out_e": w_out_e[0].astype(BF16),
        "qn": row(jnp.tile(qn_e[0], 2) * (ATT_HEAD_DIM ** -0.5)),
        "kn": row(jnp.tile(kn_e[0], 2)),
        "ln_ffn_e": row(ln_ffn_e[0]),
        "w_gate_e": w_gate_e[0].astype(BF16),
        "w_up_e": w_up_e[0].astype(BF16),
        "w_down_e": w_down_e[0].astype(BF16),
        "ln_mix_o": row(ln_mix_o[0]),
        "w_in_o": (w_in_o[0] * k_scale[None, :]).astype(BF16),
        "w_out_o": w_out_o[0].astype(BF16),
        "lgf": logdecay_fwd[0].astype(F32),
        "lgb": logdecay_bwd[0].astype(F32),
        "ret_gn": row(ret_gn[0]),
        "ln_ffn_o": row(ln_ffn_o[0]),
        "w_router_t": w_router[0].T.astype(F32),
        "moe_w1": moe_w1[0].astype(BF16),
        "moe_w3": moe_w3[0].astype(BF16),
        "moe_w2": moe_w2[0].astype(BF16),
        "chan_dft": _channel_dft(),
        "head_sum": (head_id[:, None] == head_id[None, :]).astype(BF16),
        "slopes": jnp.exp2(-8.0 * (jnp.arange(ATT_HEADS, dtype=F32) + 1.0) / ATT_HEADS),
    }
    return _both_trunks(x_prompt, x_sample, p)
```

```python
import functools
import math

import jax
import jax.numpy as jnp
from jax import lax
from jax.experimental import pallas as pl
from jax.experimental.pallas import tpu as pltpu

D_MODEL = 1024
FNET_GROUPS = 4
FNET_GROUP_DIM = 64
FNET_WIDTH = 256
ATT_HEAD_DIM = 64
ATT_HEADS = 12
ATT_WIDTH = 768
ATT_HALF_WINDOW = 64
DILATIONS = (1, 4, 16)
EVEN_IN = FNET_WIDTH + 3 * ATT_WIDTH
RET_HEADS = 4
RET_DK = 256
RET_DV = 512
RET_QK_WIDTH = 1024
RET_V_WIDTH = 2048
ODD_IN = 6144
D_FF = 2816
N_EXPERTS = 8
D_FF_EXPERT = 3584
EPS = 1e-6
NEG = -1e30

LANES = 128
VMEM_LIMIT_BYTES = 56 * 2**20

F32 = jnp.float32
BF16 = jnp.bfloat16


def _params(semantics):
    return pltpu.CompilerParams(dimension_semantics=semantics, vmem_limit_bytes=VMEM_LIMIT_BYTES)


def _rms(x, w):
    return x * lax.rsqrt(jnp.mean(x * x, axis=-1, keepdims=True) + EPS) * w


def _silu(x):
    return x / (1.0 + jnp.exp(-x))


QKV_CLASSES = 16


def _even_in_kernel(x_ref, ln_ref, w_ref, cd_ref, hs_ref, qn_ref, kn_ref, y_ref, qkv_ref, base_ref, zs_ref):
    h = _rms(x_ref[0], ln_ref[...]).astype(BF16)
    u = jnp.dot(h, w_ref[:, 0:FNET_WIDTH], preferred_element_type=F32).astype(BF16)
    yy = jnp.dot(u, cd_ref[...], preferred_element_type=F32)
    y_ref[0] = yy[:, :FNET_WIDTH].astype(BF16)
    y_ref[1] = yy[:, FNET_WIDTH:].astype(BF16)
    for part, n_ref in ((0, qn_ref), (1, kn_ref)):
        c0 = FNET_WIDTH + part * ATT_WIDTH
        z = jnp.dot(h, w_ref[:, c0:c0 + ATT_WIDTH], preferred_element_type=F32)
        for j in range(ATT_WIDTH // LANES):
            zj = z[:, j * LANES:(j + 1) * LANES]
            ss = jnp.dot((zj * zj).astype(BF16), hs_ref[...], preferred_element_type=F32)
            zn = zj * lax.rsqrt(ss * (1.0 / ATT_HEAD_DIM) + EPS) * n_ref[...]
            cols = slice(part * ATT_WIDTH + j * LANES, part * ATT_WIDTH + (j + 1) * LANES)
            qkv_ref[0, :, cols] = zn.astype(BF16)
            zs_ref[part * (ATT_WIDTH // LANES) + j] = zn
    c0 = FNET_WIDTH + 2 * ATT_WIDTH
    v = jnp.dot(h, w_ref[:, c0:c0 + ATT_WIDTH], preferred_element_type=F32)
    qkv_ref[0, :, 2 * ATT_WIDTH:3 * ATT_WIDTH] = v.astype(BF16)
    for j in range(ATT_WIDTH // LANES):
        zs_ref[2 * (ATT_WIDTH // LANES) + j] = v[:, j * LANES:(j + 1) * LANES]
    rows_per_class = zs_ref.shape[1] // QKV_CLASSES
    for rho in range(QKV_CLASSES):
        for c in range(zs_ref.shape[0]):
            base_ref[0, rho, :, c * LANES:(c + 1) * LANES] = (
                zs_ref[c, pl.ds(rho, rows_per_class, stride=QKV_CLASSES), :].astype(BF16))


def _even_in(x, ln, w_in, cd, hs, qn, kn, tm=512):
    b, s, d = x.shape
    full = lambda shp: pl.BlockSpec(shp, lambda bi, i: (0,) * len(shp))
    return pl.pallas_call(
        _even_in_kernel,
        grid=(b, s // tm),
        in_specs=[
            pl.BlockSpec((1, tm, d), lambda bi, i: (bi, i, 0)),
            full((1, d)), full((d, EVEN_IN)), full((FNET_WIDTH, 2 * FNET_WIDTH)), full((LANES, LANES)),
            full((1, LANES)), full((1, LANES)),
        ],
        out_specs=[
            pl.BlockSpec((2, tm, FNET_WIDTH), lambda bi, i: (0, i, bi)),
            pl.BlockSpec((1, tm, 3 * ATT_WIDTH), lambda bi, i: (bi, i, 0)),
            pl.BlockSpec((1, QKV_CLASSES, tm // QKV_CLASSES, 3 * ATT_WIDTH), lambda bi, i: (bi, 0, i, 0)),
        ],
        out_shape=[
            jax.ShapeDtypeStruct((2, s, b * FNET_WIDTH), BF16),
            jax.ShapeDtypeStruct((b, s, 3 * ATT_WIDTH), BF16),
            jax.ShapeDtypeStruct((b, QKV_CLASSES, s // QKV_CLASSES, 3 * ATT_WIDTH), BF16),
        ],
        scratch_shapes=[pltpu.VMEM((3 * ATT_WIDTH // LANES, tm, LANES), F32)],
        compiler_params=_params(("parallel", "parallel")),
        name="even_in_proj",
    )(x, ln, w_in, cd, hs, qn, kn)


def _matmul_kernel(a_ref, b_ref, o_ref, acc_ref):
    @pl.when(pl.program_id(2) == 0)
    def _():
        acc_ref[...] = jnp.zeros_like(acc_ref)

    acc_ref[...] += jnp.dot(a_ref[...], b_ref[...], preferred_element_type=F32)

    @pl.when(pl.program_id(2) == pl.num_programs(2) - 1)
    def _():
        o_ref[...] = acc_ref[...].astype(o_ref.dtype)


def _matmul(a, b, tm=1024, tn=2048, tk=1024):
    m, k = a.shape
    _, n = b.shape
    tm, tn, tk = min(tm, m), min(tn, n), min(tk, k)
    return pl.pallas_call(
        _matmul_kernel,
        grid=(m // tm, n // tn, k // tk),
        in_specs=[pl.BlockSpec((tm, tk), lambda i, j, kk: (i, kk)),
                  pl.BlockSpec((tk, tn), lambda i, j, kk: (kk, j))],
        out_specs=pl.BlockSpec((tm, tn), lambda i, j, kk: (i, j)),
        out_shape=jax.ShapeDtypeStruct((m, n), BF16),
        scratch_shapes=[pltpu.VMEM((tm, tn), F32)],
        compiler_params=_params(("parallel", "parallel", "arbitrary")),
        name="seq_dft_matmul",
    )(a, b)


ATT_QBLK = 128
ATT_CONV_ROWS = 512
ATT_GROUP = 16
ATT_MAX_DILATION = 16
ATT_DOUBLE_BUFFER_BYTES = 16 * 2**20


def _attn_kernel(slopes_ref, k_ref, v_ref, qb_ref, kb_ref, vb_ref, o_ref, xf, qs, kd, vd, bias, sc_buf, p_buf, m_buf,
                 m_s, l_s, a_s, *, seq):
    pair = pl.program_id(1)
    lane = lax.broadcasted_iota(jnp.int32, (1, LANES), 1)
    first_head = lane < ATT_HEAD_DIM
    slopes = (slopes_ref[2 * pair], slopes_ref[2 * pair + 1])
    base_len = seq // ATT_MAX_DILATION

    def conv_rows(c):
        return pl.ds(pl.multiple_of(c * ATT_CONV_ROWS, ATT_CONV_ROWS), ATT_CONV_ROWS)

    def store_split(dst_ref, rows, x, fill):
        other = jnp.full_like(x, fill)
        dst_ref[0, rows, :] = jnp.where(first_head, x, other)
        dst_ref[1, rows, :] = jnp.where(first_head, other, x)

    def class_major_rows(t, r):
        blocks_per_class = seq // r // ATT_QBLK
        rho = t // blocks_per_class
        l0 = (t % blocks_per_class) * ATT_QBLK
        return pl.ds(rho + r * l0, ATT_QBLK, stride=r)

    def widen_q(c, carry):
        qs[conv_rows(c), :] = qb_ref[0, conv_rows(c), :].astype(F32)
        return carry

    lax.fori_loop(0, seq // ATT_CONV_ROWS, widen_q, 0)

    for pattern, r in enumerate(DILATIONS):
        sub_len = seq // r
        n_keys = min(2 * ATT_QBLK, sub_len)
        blocks_per_class = sub_len // ATT_QBLK
        n_runs = ATT_MAX_DILATION // r
        run_len = ATT_QBLK // n_runs

        n_kruns = 1 if r == 1 else n_runs
        krun_len = n_keys // n_kruns
        if r == 1:
            def copy(c, carry):
                store_split(kd, conv_rows(c), k_ref[0, conv_rows(c), :], 0.0)
                store_split(vd, conv_rows(c), v_ref[0, conv_rows(c), :], 1.0)
                return carry

            lax.fori_loop(0, seq // ATT_CONV_ROWS, copy, 0)
        elif pattern == 1:
            def copy_base(c, carry):
                store_split(kd, conv_rows(c), kb_ref[0, conv_rows(c), :], 0.0)
                store_split(vd, conv_rows(c), vb_ref[0, conv_rows(c), :], 1.0)
                return carry

            lax.fori_loop(0, seq // ATT_CONV_ROWS, copy_base, 0)

        n_idx = lax.broadcasted_iota(jnp.int32, (ATT_QBLK, n_keys), 0)
        q_rel = n_runs * (n_idx % run_len) + n_idx // run_len
        c_idx = lax.broadcasted_iota(jnp.int32, (ATT_QBLK, n_keys), 1)
        k_rel = n_kruns * (c_idx % krun_len) + c_idx // krun_len
        offsets = (0, ATT_HALF_WINDOW, 2 * ATT_HALF_WINDOW) if blocks_per_class > 1 else (0,)
        for variant, off in enumerate(offsets):
            dist = jnp.abs(k_rel - off - q_rel)
            for h in range(2):
                scaled = (-slopes[h] * float(r)) * dist.astype(F32)
                bias[h, variant, :, 0:n_keys] = jnp.where(dist <= ATT_HALF_WINDOW, scaled, NEG)

        def geometry(t, r=r, blocks_per_class=blocks_per_class, n_runs=n_runs, run_len=run_len,
                     n_kruns=n_kruns, krun_len=krun_len, n_keys=n_keys):
            rho = t // blocks_per_class
            lb = t % blocks_per_class
            if blocks_per_class > 1:
                variant = jnp.where(lb == 0, 0, jnp.where(lb == blocks_per_class - 1, 2, 1))
            else:
                variant = 0
            runs = [pl.ds(pl.multiple_of((rho + r * j) * base_len + lb * run_len, 8), run_len)
                    for j in range(n_runs)]
            if r == 1:
                k_runs = [pl.ds(pl.multiple_of(t * ATT_QBLK - variant * ATT_HALF_WINDOW, ATT_HALF_WINDOW), n_keys)]
            else:
                start = lb * run_len - variant * (ATT_HALF_WINDOW // n_kruns)
                k_runs = [pl.ds(pl.multiple_of((rho + r * j) * base_len + start, 16), krun_len)
                          for j in range(n_kruns)]
            return variant, k_runs, runs

        def load_runs(ref, runs):
            return jnp.concatenate([ref[rows, :] for rows in runs], axis=0) if len(runs) > 1 else ref[runs[0], :]

        def store_runs(ref, runs, x, run_len=run_len):
            for j, rows in enumerate(runs):
                ref[rows, :] = x[j * run_len:(j + 1) * run_len, :]

        def group(g, carry, n_keys=n_keys, pattern=pattern, geometry=geometry, load_runs=load_runs,
                  store_runs=store_runs):
            for i in range(ATT_GROUP):
                variant, k_runs, runs = geometry(g * ATT_GROUP + i)
                q = load_runs(qs, runs).astype(BF16)
                for h in range(2):
                    sc = lax.dot_general(q, load_runs(kd.at[h], k_runs), (((1,), (1,)), ((), ())),
                                         preferred_element_type=F32)
                    sc_buf[i, h, :, 0:n_keys] = sc + bias[h, variant, :, 0:n_keys]
            for i in range(ATT_GROUP):
                ms = []
                for h in range(2):
                    sc = sc_buf[i, h, :, 0:n_keys]
                    m = jnp.max(sc, axis=-1, keepdims=True)
                    p_buf[i, h, :, 0:n_keys] = jnp.exp(sc - m).astype(BF16)
                    ms.append(m)
                m_buf[i] = jnp.where(first_head, ms[0], ms[1])
            for i in range(ATT_GROUP):
                _, k_runs, runs = geometry(g * ATT_GROUP + i)
                pv0 = jnp.dot(p_buf[i, 0, :, 0:n_keys], load_runs(vd.at[0], k_runs), preferred_element_type=F32)
                pv1 = jnp.dot(p_buf[i, 1, :, 0:n_keys], load_runs(vd.at[1], k_runs), preferred_element_type=F32)
                a_new = jnp.where(first_head, pv0, pv1)
                l_new = pltpu.roll(jnp.where(first_head, pv1, pv0), ATT_HEAD_DIM, 1)
                m_new = m_buf[i]
                if pattern == 0:
                    store_runs(m_s, runs, m_new)
                    store_runs(l_s, runs, l_new)
                    store_runs(a_s, runs, a_new)
                else:
                    m_old = load_runs(m_s, runs)
                    m_tot = jnp.maximum(m_old, m_new)
                    w_old = jnp.exp(m_old - m_tot)
                    w_new = jnp.exp(m_new - m_tot)
                    store_runs(m_s, runs, m_tot)
                    store_runs(l_s, runs, w_old * load_runs(l_s, runs) + w_new * l_new)
                    store_runs(a_s, runs, w_old * load_runs(a_s, runs) + w_new * a_new)
            return carry

        lax.fori_loop(0, seq // (ATT_QBLK * ATT_GROUP), group, 0)

    def scatter(t, carry):
        rows = pl.ds(pl.multiple_of(t * ATT_QBLK, ATT_QBLK), ATT_QBLK)
        xf[class_major_rows(t, ATT_MAX_DILATION), :] = a_s[rows, :] / l_s[rows, :]
        return carry

    lax.fori_loop(0, seq // ATT_QBLK, scatter, 0)

    def finish(c, carry):
        o_ref[0, conv_rows(c), :] = xf[conv_rows(c), :].astype(BF16)
        return carry

    lax.fori_loop(0, seq // ATT_CONV_ROWS, finish, 0)


def _attention(qkv, qkv_base, slopes):
    b, s, _ = qkv.shape
    n_pairs = ATT_WIDTH // LANES
    mode = {} if s * LANES * 2 * 5 * 2 <= ATT_DOUBLE_BUFFER_BYTES else {"pipeline_mode": pl.Buffered(1)}
    blk = lambda off: pl.BlockSpec((1, s, LANES), lambda bi, j, off=off: (bi, 0, off + j), **mode)
    return pl.pallas_call(
        functools.partial(_attn_kernel, seq=s),
        grid=(b, n_pairs),
        in_specs=[pl.BlockSpec(memory_space=pltpu.SMEM), blk(n_pairs), blk(2 * n_pairs),
                  blk(0), blk(n_pairs), blk(2 * n_pairs)],
        out_specs=pl.BlockSpec((1, s, LANES), lambda bi, j: (bi, 0, j)),
        out_shape=jax.ShapeDtypeStruct((b, s, ATT_WIDTH), BF16),
        scratch_shapes=[
            pltpu.VMEM((s, LANES), F32),
            pltpu.VMEM((s, LANES), F32),
            pltpu.VMEM((2, s, LANES), BF16),
            pltpu.VMEM((2, s, LANES), BF16),
            pltpu.VMEM((2, 3, ATT_QBLK, 2 * ATT_QBLK), F32),
            pltpu.VMEM((ATT_GROUP, 2, ATT_QBLK, 2 * ATT_QBLK), F32),
            pltpu.VMEM((ATT_GROUP, 2, ATT_QBLK, 2 * ATT_QBLK), BF16),
            pltpu.VMEM((ATT_GROUP, ATT_QBLK, LANES), F32),
            pltpu.VMEM((s, LANES), F32),
            pltpu.VMEM((s, LANES), F32),
            pltpu.VMEM((s, LANES), F32),
        ],
        compiler_params=_params(("parallel", "parallel")),
        name="dilated_attention",
    )(slopes, qkv, qkv, qkv_base, qkv_base, qkv_base)


def _even_out_kernel(x_ref, f_ref, a_ref, w_ref, o_ref):
    acc = x_ref[0]
    acc = acc + jnp.dot(f_ref[...], w_ref[0:FNET_WIDTH, :], preferred_element_type=F32)
    acc = acc + jnp.dot(a_ref[0], w_ref[FNET_WIDTH:, :], preferred_element_type=F32)
    o_ref[0] = acc


def _even_out(x, fmix, att, w_out, tm=512):
    b, s, d = x.shape
    return pl.pallas_call(
        _even_out_kernel,
        grid=(b, s // tm),
        in_specs=[
            pl.BlockSpec((1, tm, d), lambda bi, i: (bi, i, 0)),
            pl.BlockSpec((tm, FNET_WIDTH), lambda bi, i: (i, bi)),
            pl.BlockSpec((1, tm, ATT_WIDTH), lambda bi, i: (bi, i, 0)),
            pl.BlockSpec((d, d), lambda bi, i: (0, 0)),
        ],
        out_specs=pl.BlockSpec((1, tm, d), lambda bi, i: (bi, i, 0)),
        out_shape=jax.ShapeDtypeStruct((b, s, d), F32),
        compiler_params=_params(("parallel", "parallel")),
        name="even_out_proj",
    )(x, fmix, att, w_out)


def _resident(shape):
    return pl.BlockSpec(shape, lambda *_: (0,) * len(shape), pipeline_mode=pl.Buffered(1))


def _swiglu_kernel(x_ref, ln_ref, wg_ref, wu_ref, wd_ref, o_ref):
    x = x_ref[...]
    h = _rms(x, ln_ref[...]).astype(BF16)
    g = jnp.dot(h, wg_ref[...], preferred_element_type=F32)
    u = jnp.dot(h, wu_ref[...], preferred_element_type=F32)
    act = (_silu(g) * u).astype(BF16)
    o_ref[...] = x + jnp.dot(act, wd_ref[...], preferred_element_type=F32)


def _swiglu(x, ln, wg, wu, wd, tm=512):
    n, d = x.shape
    dff = wg.shape[1]
    return pl.pallas_call(
        _swiglu_kernel,
        grid=(n // tm,),
        in_specs=[
            pl.BlockSpec((tm, d), lambda i: (i, 0)),
            _resident((1, d)), _resident((d, dff)), _resident((d, dff)), _resident((dff, d)),
        ],
        out_specs=pl.BlockSpec((tm, d), lambda i: (i, 0)),
        out_shape=jax.ShapeDtypeStruct((n, d), F32),
        compiler_params=_params(("parallel",)),
        name="swiglu_ffn",
    )(x, ln, wg, wu, wd)


ODD_IN_COLS = 2048


def _odd_in_kernel(x_ref, ln_ref, w_ref, o_ref):
    h = _rms(x_ref[...], ln_ref[...]).astype(BF16)
    for c0 in range(0, ODD_IN, ODD_IN_COLS):
        cols = slice(c0, c0 + ODD_IN_COLS)
        o_ref[:, cols] = jnp.dot(h, w_ref[:, cols], preferred_element_type=F32).astype(BF16)


def _odd_in(x, ln, w_in, tm=512):
    n, d = x.shape
    return pl.pallas_call(
        _odd_in_kernel,
        grid=(n // tm,),
        in_specs=[pl.BlockSpec((tm, d), lambda i: (i, 0)), _resident((1, d)), _resident((d, ODD_IN))],
        out_specs=pl.BlockSpec((tm, ODD_IN), lambda i: (i, 0)),
        out_shape=jax.ShapeDtypeStruct((n, ODD_IN), BF16),
        compiler_params=_params(("parallel",)),
        name="odd_in_proj",
    )(x, ln, w_in)


RET_CHUNK = 256
RET_TILE = 1024


def _chunk_positions():
    return lax.broadcasted_iota(jnp.int32, (RET_CHUNK, 1), 0).astype(F32)


def _kt_v(k_scaled, v):
    return lax.dot_general(k_scaled, v, (((0,), (0,)), ((), ())), preferred_element_type=F32)


def _ret_bwd_state_kernel(lgb_ref, k_ref, v_ref, sb_ref, state):
    head = pl.program_id(1)

    @pl.when(pl.program_id(2) == 0)
    def _():
        state[...] = jnp.zeros_like(state)

    lg = lgb_ref[head]
    key_decay = jnp.exp(lg * _chunk_positions())
    chunk_decay = jnp.exp(jnp.full((1, RET_DV), lg * RET_CHUNK, F32))
    for c in reversed(range(RET_TILE // RET_CHUNK)):
        rows = slice(c * RET_CHUNK, (c + 1) * RET_CHUNK)
        sb_ref[0, 0, c] = state[...].astype(BF16)
        kd = (k_ref[0, rows, :].astype(F32) * key_decay).astype(BF16)
        state[...] = state[...] * chunk_decay + _kt_v(kd, v_ref[0, rows, :])


def _ret_bwd_states(z, lgb):
    b, s, _ = z.shape
    n = s // RET_CHUNK
    n_tiles = s // RET_TILE
    per_tile = RET_TILE // RET_CHUNK
    k_off = RET_QK_WIDTH // RET_DK
    v_off = 2 * RET_QK_WIDTH // RET_DV
    return pl.pallas_call(
        _ret_bwd_state_kernel,
        grid=(b, RET_HEADS, n_tiles),
        in_specs=[
            pl.BlockSpec(memory_space=pltpu.SMEM),
            pl.BlockSpec((1, RET_TILE, RET_DK), lambda bi, h, t: (bi, n_tiles - 1 - t, k_off + h)),
            pl.BlockSpec((1, RET_TILE, RET_DV), lambda bi, h, t: (bi, n_tiles - 1 - t, v_off + h)),
        ],
        out_specs=pl.BlockSpec((1, 1, per_tile, RET_DK, RET_DV), lambda bi, h, t: (bi, h, n_tiles - 1 - t, 0, 0)),
        out_shape=jax.ShapeDtypeStruct((b, RET_HEADS, n, RET_DK, RET_DV), BF16),
        scratch_shapes=[pltpu.VMEM((RET_DK, RET_DV), F32)],
        compiler_params=_params(("parallel", "parallel", "arbitrary")),
        name="retention_bwd_states",
    )(lgb, z, z)


def _ret_main_kernel(lgf_ref, lgb_ref, q_ref, k_ref, v_ref, g_ref, sb_ref, gn_ref, o_ref, state):
    head = pl.program_id(1)

    @pl.when(pl.program_id(2) == 0)
    def _():
        state[...] = jnp.zeros_like(state)

    lgf = lgf_ref[head]
    lgb = lgb_ref[head]
    c = RET_CHUNK
    pos = _chunk_positions()
    rel = (lax.broadcasted_iota(jnp.int32, (c, c), 0) - lax.broadcasted_iota(jnp.int32, (c, c), 1)).astype(F32)
    decay = jnp.exp(jnp.where(rel >= 0, lgf * rel, -lgb * rel))
    q_decay_f = jnp.exp(lgf * (pos + 1.0))
    q_decay_b = jnp.exp(lgb * (c - pos))
    k_decay_f = jnp.exp(lgf * (c - 1.0 - pos))
    chunk_decay = jnp.exp(jnp.full((1, RET_DV), lgf * c, F32))
    gn = gn_ref[...]

    for ci in range(RET_TILE // RET_CHUNK):
        rows = slice(ci * c, (ci + 1) * c)
        q = q_ref[0, rows, :]
        k = k_ref[0, rows, :]
        v = v_ref[0, rows, :]
        qf32 = q.astype(F32)
        sc = lax.dot_general(q, k, (((1,), (1,)), ((), ())), preferred_element_type=F32)
        y = jnp.dot((sc * decay).astype(BF16), v, preferred_element_type=F32)
        y = y + jnp.dot((qf32 * q_decay_f).astype(BF16), state[...].astype(BF16), preferred_element_type=F32)
        y = y + jnp.dot((qf32 * q_decay_b).astype(BF16), sb_ref[0, 0, ci], preferred_element_type=F32)
        kd = (k.astype(F32) * k_decay_f).astype(BF16)
        state[...] = state[...] * chunk_decay + _kt_v(kd, v)
        yn = _rms(y, gn)
        o_ref[0, rows, :] = (_silu(g_ref[0, rows, :].astype(F32)) * yn).astype(BF16)


def _ret_main(z, sb, lgf, lgb, gn):
    b, s, _ = z.shape
    n_tiles = s // RET_TILE
    per_tile = RET_TILE // RET_CHUNK
    k_off = RET_QK_WIDTH // RET_DK
    v_off = 2 * RET_QK_WIDTH // RET_DV
    g_off = (2 * RET_QK_WIDTH + RET_V_WIDTH) // RET_DV
    smem = pl.BlockSpec(memory_space=pltpu.SMEM)
    return pl.pallas_call(
        _ret_main_kernel,
        grid=(b, RET_HEADS, n_tiles),
        in_specs=[
            smem, smem,
            pl.BlockSpec((1, RET_TILE, RET_DK), lambda bi, h, t: (bi, t, h)),
            pl.BlockSpec((1, RET_TILE, RET_DK), lambda bi, h, t: (bi, t, k_off + h)),
            pl.BlockSpec((1, RET_TILE, RET_DV), lambda bi, h, t: (bi, t, v_off + h)),
            pl.BlockSpec((1, RET_TILE, RET_DV), lambda bi, h, t: (bi, t, g_off + h)),
            pl.BlockSpec((1, 1, per_tile, RET_DK, RET_DV), lambda bi, h, t: (bi, h, t, 0, 0)),
            pl.BlockSpec((1, RET_DV), lambda bi, h, t: (0, h)),
        ],
        out_specs=pl.BlockSpec((1, RET_TILE, RET_DV), lambda bi, h, t: (bi, t, h)),
        out_shape=jax.ShapeDtypeStruct((b, s, RET_V_WIDTH), BF16),
        scratch_shapes=[pltpu.VMEM((RET_DK, RET_DV), F32)],
        compiler_params=_params(("parallel", "parallel", "arbitrary")),
        name="retention_main",
    )(lgf, lgb, z, z, z, z, sb, gn)


def _odd_out_router_kernel(y_ref, x_ref, w_ref, ln_ref, wr_ref, xo_ref, h_ref, idx_ref, gate_ref, hn_prev):
    @pl.when(pl.program_id(0) == 0)
    def _():
        hn_prev[...] = jnp.zeros_like(hn_prev)

    prev = hn_prev[...]
    cols = [jnp.sum(prev * wr_ref[e:e + 1, :], axis=-1, keepdims=True) for e in range(N_EXPERTS)]
    pad = jnp.full((prev.shape[0], LANES - N_EXPERTS), -jnp.inf, F32)
    logits = jnp.transpose(jnp.concatenate(cols + [pad], axis=1))[0:N_EXPERTS, :]
    row = lax.broadcasted_iota(jnp.int32, logits.shape, 0).astype(F32)
    none = float(N_EXPERTS)
    m1 = jnp.max(logits, axis=0, keepdims=True)
    i1 = jnp.min(jnp.where(logits == m1, row, none), axis=0, keepdims=True)
    rest = jnp.where(row == i1, -jnp.inf, logits)
    m2 = jnp.max(rest, axis=0, keepdims=True)
    i2 = jnp.min(jnp.where(rest == m2, row, none), axis=0, keepdims=True)
    e2 = jnp.exp(m2 - m1)
    idx_ref[0:1, :] = i1.astype(jnp.int32)
    idx_ref[1:2, :] = i2.astype(jnp.int32)
    gate_ref[0:1, :] = 1.0 / (1.0 + e2)
    gate_ref[1:2, :] = e2 / (1.0 + e2)

    x = x_ref[...] + jnp.dot(y_ref[...], w_ref[...], preferred_element_type=F32)
    xo_ref[...] = x
    hn = _rms(x, ln_ref[...])
    h_ref[...] = hn.astype(BF16)
    hn_prev[...] = hn


def _odd_out_router(y, x, w_out, ln, wr_t, tm=512):
    n, d = x.shape
    n_tiles = n // tm
    full = lambda shp: pl.BlockSpec(shp, lambda i: (0,) * len(shp))
    cur = lambda i: (jnp.minimum(i, n_tiles - 1), 0)
    prev = lambda i: (0, jnp.maximum(i - 1, 0))
    return pl.pallas_call(
        _odd_out_router_kernel,
        grid=(n_tiles + 1,),
        in_specs=[
            pl.BlockSpec((tm, RET_V_WIDTH), cur),
            pl.BlockSpec((tm, d), cur),
            full((RET_V_WIDTH, d)), full((1, d)), full((N_EXPERTS, d)),
        ],
        out_specs=[
            pl.BlockSpec((tm, d), cur),
            pl.BlockSpec((tm, d), cur),
            pl.BlockSpec((2, tm), prev),
            pl.BlockSpec((2, tm), prev),
        ],
        out_shape=[
            jax.ShapeDtypeStruct((n, d), F32),
            jax.ShapeDtypeStruct((n, d), BF16),
            jax.ShapeDtypeStruct((2, n), jnp.int32),
            jax.ShapeDtypeStruct((2, n), F32),
        ],
        scratch_shapes=[pltpu.VMEM((tm, d), F32)],
        compiler_params=_params(("arbitrary",)),
        name="odd_out_proj_router",
    )(y, x, w_out, ln, wr_t)


MOE_TM = 1024
MOE_TF = 1792
MOE_SUB = 512


def _moe_kernel(tile_expert_ref, n_used_ref, xs_ref, gate_ref, w1_ref, w3_ref, w2_ref, o_ref, acc_ref):
    t = pl.program_id(0)
    f = pl.program_id(1)
    last = pl.num_programs(1) - 1
    used = t < n_used_ref[0]

    @pl.when(used)
    def _():
        for r0 in range(0, MOE_TM, MOE_SUB):
            rows = slice(r0, r0 + MOE_SUB)
            xs = xs_ref[rows, :]
            g = jnp.dot(xs, w1_ref[...], preferred_element_type=F32)
            u = jnp.dot(xs, w3_ref[...], preferred_element_type=F32)
            act = (_silu(g) * u).astype(BF16)
            part = jnp.dot(act, w2_ref[...], preferred_element_type=F32)

            @pl.when(f == 0)
            def _():
                acc_ref[rows, :] = part

            @pl.when(jnp.logical_and(f > 0, f < last))
            def _():
                acc_ref[rows, :] += part

            @pl.when(f == last)
            def _():
                o_ref[rows, :] = ((acc_ref[rows, :] + part) * gate_ref[rows, 0:1]).astype(BF16)

    @pl.when(jnp.logical_and(jnp.logical_not(used), f == last))
    def _():
        o_ref[...] = jnp.zeros_like(o_ref)


def _moe_experts(xs, gate_rows, tile_expert, n_used, w1, w3, w2):
    p, d = xs.shape
    n_tiles = p // MOE_TM
    n_f = D_FF_EXPERT // MOE_TF
    assert n_f >= 2 and n_f * MOE_TF == D_FF_EXPERT

    def row_tile(t, f, te, nu):
        return (jnp.minimum(t, nu[0] - 1), 0)

    def f_idx(t, f, nu):
        return jnp.where(t < nu[0], f, n_f - 1)

    grid_spec = pltpu.PrefetchScalarGridSpec(
        num_scalar_prefetch=2,
        grid=(n_tiles, n_f),
        in_specs=[
            pl.BlockSpec((MOE_TM, d), row_tile),
            pl.BlockSpec((MOE_TM, LANES), row_tile),
            pl.BlockSpec((None, d, MOE_TF), lambda t, f, te, nu: (te[t], 0, f_idx(t, f, nu))),
            pl.BlockSpec((None, d, MOE_TF), lambda t, f, te, nu: (te[t], 0, f_idx(t, f, nu))),
            pl.BlockSpec((None, MOE_TF, d), lambda t, f, te, nu: (te[t], f_idx(t, f, nu), 0)),
        ],
        out_specs=pl.BlockSpec((MOE_TM, d), lambda t, f, te, nu: (t, 0)),
        scratch_shapes=[pltpu.VMEM((MOE_TM, d), F32)],
    )
    return pl.pallas_call(
        _moe_kernel,
        grid_spec=grid_spec,
        out_shape=jax.ShapeDtypeStruct((p, d), BF16),
        compiler_params=_params(("arbitrary", "arbitrary")),
        name="moe_experts",
    )(tile_expert, n_used, xs, gate_rows, w1, w3, w2)


def _combine_kernel(x_ref, y_ref, o_ref):
    o_ref[...] = x_ref[...] + y_ref[0].astype(F32) + y_ref[1].astype(F32)


def _combine(x, y2, tm=1024):
    n, d = x.shape
    return pl.pallas_call(
        _combine_kernel,
        grid=(n // tm,),
        in_specs=[pl.BlockSpec((tm, d), lambda i: (i, 0)), pl.BlockSpec((2, tm, d), lambda i: (0, i, 0))],
        out_specs=pl.BlockSpec((tm, d), lambda i: (i, 0)),
        out_shape=jax.ShapeDtypeStruct((n, d), F32),
        compiler_params=_params(("parallel",)),
        name="moe_combine",
    )(x, y2)


def _route(idx, gates, n):
    e_flat = idx.reshape(-1)
    n_pairs = e_flat.shape[0]
    pair_id = jnp.arange(n_pairs, dtype=jnp.int32)
    order = (jnp.sort(e_flat * n_pairs + pair_id) % n_pairs).astype(jnp.int32)
    rank = jnp.argsort(order).astype(jnp.int32)
    counts = jnp.sum(e_flat[None, :] == jnp.arange(N_EXPERTS, dtype=jnp.int32)[:, None], axis=1).astype(jnp.int32)
    starts = jnp.cumsum(counts) - counts
    tiles_per_expert = (counts + MOE_TM - 1) // MOE_TM
    tile_ends = jnp.cumsum(tiles_per_expert)
    padded_starts = (tile_ends - tiles_per_expert) * MOE_TM
    dest = rank + (padded_starts - starts)[e_flat]
    n_rows = n_pairs + N_EXPERTS * MOE_TM
    n_tiles = n_rows // MOE_TM
    tile_expert = jnp.minimum(
        jnp.sum(jnp.arange(n_tiles, dtype=jnp.int32)[:, None] >= tile_ends[None, :], axis=1), N_EXPERTS - 1
    ).astype(jnp.int32)
    row_expert = jnp.repeat(tile_expert, MOE_TM)
    row_in_expert = jnp.arange(n_rows, dtype=jnp.int32) - padded_starts[row_expert]
    row_valid = row_in_expert < counts[row_expert]
    row_pair = order[jnp.clip(starts[row_expert] + row_in_expert, 0, n_pairs - 1)]
    src_token = jnp.where(row_valid, row_pair % n, 0).astype(jnp.int32)
    row_gate = jnp.where(row_valid, gates.reshape(-1)[row_pair], 0.0)
    n_used = tile_ends[-1:].astype(jnp.int32)
    return src_token, row_gate, dest, tile_expert, n_used


def _channel_dft():
    c = jnp.arange(FNET_GROUP_DIM, dtype=jnp.int32)
    ang = ((c[:, None] * c[None, :]) % FNET_GROUP_DIM).astype(F32) * (2.0 * math.pi / FNET_GROUP_DIM)
    eye = jnp.eye(FNET_GROUPS, dtype=F32)
    scale = FNET_GROUP_DIM ** -0.5
    return jnp.concatenate([jnp.kron(eye, jnp.cos(ang)) * scale, -jnp.kron(eye, jnp.sin(ang)) * scale], axis=1).astype(BF16)


def _sequence_dft(s):
    hi = s // 64
    k = jnp.arange(s, dtype=jnp.int32)
    j_hi = jnp.arange(hi, dtype=jnp.int32)
    j_lo = jnp.arange(64, dtype=jnp.int32)
    w = 2.0 * math.pi / s
    ang_a = (((j_hi[:, None] * k[None, :]) % hi) * 64).astype(F32) * w
    ang_b = ((j_lo[:, None] * k[None, :]) % s).astype(F32) * w
    scale = s ** -0.5
    ca, sa = jnp.cos(ang_a) * scale, jnp.sin(ang_a) * scale
    cb, sb = jnp.cos(ang_b), jnp.sin(ang_b)
    ca2 = jnp.concatenate([ca, ca], axis=1)[:, None, :]
    sa2 = jnp.concatenate([sa, sa], axis=1)[:, None, :]
    p = jnp.concatenate([cb, sb], axis=1)[None, :, :]
    q = jnp.concatenate([sb, -cb], axis=1)[None, :, :]
    return (ca2 * p - sa2 * q).reshape(s, 2 * s).astype(BF16)


def _trunk_to_router(x, p):
    b, s, d = x.shape
    n = b * s
    y, qkv, qkv_base = _even_in(x, p["ln_mix_e"], p["w_in_e"], p["chan_dft"], p["head_sum"], p["qn"], p["kn"])
    fmix = _matmul(_sequence_dft(s), y.reshape(2 * s, b * FNET_WIDTH))
    att = _attention(qkv, qkv_base.reshape(b, s, 3 * ATT_WIDTH), p["slopes"])
    x = _even_out(x, fmix, att, p["w_out_e"])
    x = _swiglu(x.reshape(n, d), p["ln_ffn_e"], p["w_gate_e"], p["w_up_e"], p["w_down_e"])
    z = _odd_in(x, p["ln_mix_o"], p["w_in_o"]).reshape(b, s, ODD_IN)
    sb = _ret_bwd_states(z, p["lgb"])
    yr = _ret_main(z, sb, p["lgf"], p["lgb"], p["ret_gn"])
    x, h, idx, gates = _odd_out_router(yr.reshape(n, RET_V_WIDTH), x, p["w_out_o"], p["ln_ffn_o"], p["w_router_t"])
    return x, h, _route(idx, gates, n)


def _dispatch(h, route):
    src_token, row_gate = route[0], route[1]
    xs = h.at[src_token].get(mode="promise_in_bounds")
    return xs, jnp.broadcast_to(row_gate[:, None], (row_gate.shape[0], LANES))


def _experts(xs, gate_rows, route, p):
    return _moe_experts(xs, gate_rows, route[3], route[4], p["moe_w1"], p["moe_w3"], p["moe_w2"])


def _collect(ye, route, n):
    return ye.at[route[2]].get(mode="promise_in_bounds").reshape(2, n, D_MODEL)


def _both_trunks(x_a, x_b, p):
    outs = []
    for x in (x_a, x_b):
        n = x.shape[0] * x.shape[1]
        xr, h, route = _trunk_to_router(x, p)
        xs, gate_rows = _dispatch(h, route)
        ye = _experts(xs, gate_rows, route, p)
        outs.append(_combine(xr, _collect(ye, route, n)).reshape(x.shape))
    return tuple(outs)


def kernel(x_prompt, x_sample, ln_mix_e, w_in_e, w_out_e, qn_e, kn_e, ln_ffn_e, w_gate_e, w_up_e, w_down_e, ln_mix_o, w_in_o, w_out_o, logdecay_fwd, logdecay_bwd, ret_gn, ln_ffn_o, w_router, moe_w1, moe_w3, moe_w2):
    row = lambda w: w.reshape(1, -1).astype(F32)
    k_scale = jnp.concatenate([
        jnp.ones((RET_QK_WIDTH,), F32), jnp.full((RET_QK_WIDTH,), RET_DK ** -0.5, F32),
        jnp.ones((2 * RET_V_WIDTH,), F32)])
    head_id = jnp.arange(LANES, dtype=jnp.int32) // ATT_HEAD_DIM
    p = {
        "ln_mix_e": row(ln_mix_e[0]),
        "w_in_e": w_in_e[0].astype(BF16),
        "w_out_e": w_out_e[0].astype(BF16),
        "qn": row(jnp.tile(qn_e[0], 2) * (ATT_HEAD_DIM ** -0.5)),
        "kn": row(jnp.tile(kn_e[0], 2)),
        "ln_ffn_e": row(ln_ffn_e[0]),
        "w_gate_e": w_gate_e[0].astype(BF16),
        "w_up_e": w_up_e[0].astype(BF16),
        "w_down_e": w_down_e[0].astype(BF16),
        "ln_mix_o": row(ln_mix_o[0]),
        "w_in_o": (w_in_o[0] * k_scale[None, :]).astype(BF16),
        "w_out_o": w_out_o[0].astype(BF16),
        "lgf": logdecay_fwd[0].astype(F32),
        "lgb": logdecay_bwd[0].astype(F32),
        "ret_gn": row(ret_gn[0]),
        "ln_ffn_o": row(ln_ffn_o[0]),
        "w_router_t": w_router[0].T.astype(F32),
        "moe_w1": moe_w1[0].astype(BF16),
        "moe_w3": moe_w3[0].astype(BF16),
        "moe_w2": moe_w2[0].astype(BF16),
        "chan_dft": _channel_dft(),
        "head_sum": (head_id[:, None] == head_id[None, :]).astype(BF16),
        "slopes": jnp.exp2(-8.0 * (jnp.arange(ATT_HEADS, dtype=F32) + 1.0) / ATT_HEADS),
    }
    return _both_trunks(x_prompt, x_sample, p)
```

```python
import functools
import math

import jax
import jax.numpy as jnp
from jax import lax
from jax.experimental import pallas as pl
from jax.experimental.pallas import tpu as pltpu

D_MODEL = 1024
FNET_GROUPS = 4
FNET_GROUP_DIM = 64
FNET_WIDTH = 256
ATT_HEAD_DIM = 64
ATT_HEADS = 12
ATT_WIDTH = 768
ATT_HALF_WINDOW = 64
DILATIONS = (1, 4, 16)
EVEN_IN = FNET_WIDTH + 3 * ATT_WIDTH
RET_HEADS = 4
RET_DK = 256
RET_DV = 512
RET_QK_WIDTH = 1024
RET_V_WIDTH = 2048
ODD_IN = 6144
D_FF = 2816
N_EXPERTS = 8
D_FF_EXPERT = 3584
EPS = 1e-6
NEG = -1e30

LANES = 128
VMEM_LIMIT_BYTES = 56 * 2**20

F32 = jnp.float32
BF16 = jnp.bfloat16


def _params(semantics):
    return pltpu.CompilerParams(dimension_semantics=semantics, vmem_limit_bytes=VMEM_LIMIT_BYTES)


def _rms(x, w):
    return x * lax.rsqrt(jnp.mean(x * x, axis=-1, keepdims=True) + EPS) * w


def _silu(x):
    return x / (1.0 + jnp.exp(-x))


QKV_CLASSES = 16


def _even_in_kernel(x_ref, ln_ref, w_ref, cd_ref, hs_ref, qn_ref, kn_ref, y_ref, qkv_ref, base_ref, zs_ref):
    h = _rms(x_ref[0], ln_ref[...]).astype(BF16)
    u = jnp.dot(h, w_ref[:, 0:FNET_WIDTH], preferred_element_type=F32).astype(BF16)
    yy = jnp.dot(u, cd_ref[...], preferred_element_type=F32)
    y_ref[0] = yy[:, :FNET_WIDTH].astype(BF16)
    y_ref[1] = yy[:, FNET_WIDTH:].astype(BF16)
    for part, n_ref in ((0, qn_ref), (1, kn_ref)):
        c0 = FNET_WIDTH + part * ATT_WIDTH
        z = jnp.dot(h, w_ref[:, c0:c0 + ATT_WIDTH], preferred_element_type=F32)
        for j in range(ATT_WIDTH // LANES):
            zj = z[:, j * LANES:(j + 1) * LANES]
            ss = jnp.dot((zj * zj).astype(BF16), hs_ref[...], preferred_element_type=F32)
            zn = zj * lax.rsqrt(ss * (1.0 / ATT_HEAD_DIM) + EPS) * n_ref[...]
            cols = slice(part * ATT_WIDTH + j * LANES, part * ATT_WIDTH + (j + 1) * LANES)
            qkv_ref[0, :, cols] = zn.astype(BF16)
            zs_ref[part * (ATT_WIDTH // LANES) + j] = zn
    c0 = FNET_WIDTH + 2 * ATT_WIDTH
    v = jnp.dot(h, w_ref[:, c0:c0 + ATT_WIDTH], preferred_element_type=F32)
    qkv_ref[0, :, 2 * ATT_WIDTH:3 * ATT_WIDTH] = v.astype(BF16)
    for j in range(ATT_WIDTH // LANES):
        zs_ref[2 * (ATT_WIDTH // LANES) + j] = v[:, j * LANES:(j + 1) * LANES]
    rows_per_class = zs_ref.shape[1] // QKV_CLASSES
    for rho in range(QKV_CLASSES):
        for c in range(zs_ref.shape[0]):
            base_ref[0, rho, :, c * LANES:(c + 1) * LANES] = (
                zs_ref[c, pl.ds(rho, rows_per_class, stride=QKV_CLASSES), :].astype(BF16))


def _even_in(x, ln, w_in, cd, hs, qn, kn, tm=512):
    b, s, d = x.shape
    full = lambda shp: pl.BlockSpec(shp, lambda bi, i: (0,) * len(shp))
    return pl.pallas_call(
        _even_in_kernel,
        grid=(b, s // tm),
        in_specs=[
            pl.BlockSpec((1, tm, d), lambda bi, i: (bi, i, 0)),
            full((1, d)), full((d, EVEN_IN)), full((FNET_WIDTH, 2 * FNET_WIDTH)), full((LANES, LANES)),
            full((1, LANES)), full((1, LANES)),
        ],
        out_specs=[
            pl.BlockSpec((2, tm, FNET_WIDTH), lambda bi, i: (0, i, bi)),
            pl.BlockSpec((1, tm, 3 * ATT_WIDTH), lambda bi, i: (bi, i, 0)),
            pl.BlockSpec((1, QKV_CLASSES, tm // QKV_CLASSES, 3 * ATT_WIDTH), lambda bi, i: (bi, 0, i, 0)),
        ],
        out_shape=[
            jax.ShapeDtypeStruct((2, s, b * FNET_WIDTH), BF16),
            jax.ShapeDtypeStruct((b, s, 3 * ATT_WIDTH), BF16),
            jax.ShapeDtypeStruct((b, QKV_CLASSES, s // QKV_CLASSES, 3 * ATT_WIDTH), BF16),
        ],
        scratch_shapes=[pltpu.VMEM((3 * ATT_WIDTH // LANES, tm, LANES), F32)],
        compiler_params=_params(("parallel", "parallel")),
        name="even_in_proj",
    )(x, ln, w_in, cd, hs, qn, kn)


def _matmul_kernel(a_ref, b_ref, o_ref, acc_ref):
    @pl.when(pl.program_id(2) == 0)
    def _():
        acc_ref[...] = jnp.zeros_like(acc_ref)

    acc_ref[...] += jnp.dot(a_ref[...], b_ref[...], preferred_element_type=F32)

    @pl.when(pl.program_id(2) == pl.num_programs(2) - 1)
    def _():
        o_ref[...] = acc_ref[...].astype(o_ref.dtype)


def _matmul(a, b, tm=1024, tn=2048, tk=1024):
    m, k = a.shape
    _, n = b.shape
    tm, tn, tk = min(tm, m), min(tn, n), min(tk, k)
    return pl.pallas_call(
        _matmul_kernel,
        grid=(m // tm, n // tn, k // tk),
        in_specs=[pl.BlockSpec((tm, tk), lambda i, j, kk: (i, kk)),
                  pl.BlockSpec((tk, tn), lambda i, j, kk: (kk, j))],
        out_specs=pl.BlockSpec((tm, tn), lambda i, j, kk: (i, j)),
        out_shape=jax.ShapeDtypeStruct((m, n), BF16),
        scratch_shapes=[pltpu.VMEM((tm, tn), F32)],
        compiler_params=_params(("parallel", "parallel", "arbitrary")),
        name="seq_dft_matmul",
    )(a, b)


ATT_QBLK = 128
ATT_CONV_ROWS = 512
ATT_GROUP = 16
ATT_MAX_DILATION = 16
ATT_DOUBLE_BUFFER_BYTES = 16 * 2**20


def _attn_kernel(slopes_ref, k_ref, v_ref, qb_ref, kb_ref, vb_ref, o_ref, xf, qs, kd, vd, bias, sc_buf, p_buf, m_buf,
                 m_s, l_s, a_s, *, seq):
    pair = pl.program_id(1)
    lane = lax.broadcasted_iota(jnp.int32, (1, LANES), 1)
    first_head = lane < ATT_HEAD_DIM
    slopes = (slopes_ref[2 * pair], slopes_ref[2 * pair + 1])
    base_len = seq // ATT_MAX_DILATION

    def conv_rows(c):
        return pl.ds(pl.multiple_of(c * ATT_CONV_ROWS, ATT_CONV_ROWS), ATT_CONV_ROWS)

    def store_split(dst_ref, rows, x, fill):
        other = jnp.full_like(x, fill)
        dst_ref[0, rows, :] = jnp.where(first_head, x, other)
        dst_ref[1, rows, :] = jnp.where(first_head, other, x)

    def class_major_rows(t, r):
        blocks_per_class = seq // r // ATT_QBLK
        rho = t // blocks_per_class
        l0 = (t % blocks_per_class) * ATT_QBLK
        return pl.ds(rho + r * l0, ATT_QBLK, stride=r)

    def widen_q(c, carry):
        qs[conv_rows(c), :] = qb_ref[0, conv_rows(c), :].astype(F32)
        return carry

    lax.fori_loop(0, seq // ATT_CONV_ROWS, widen_q, 0)

    for pattern, r in enumerate(DILATIONS):
        sub_len = seq // r
        n_keys = min(2 * ATT_QBLK, sub_len)
        blocks_per_class = sub_len // ATT_QBLK
        n_runs = ATT_MAX_DILATION // r
        run_len = ATT_QBLK // n_runs

        n_kruns = 1 if r == 1 else n_runs
        krun_len = n_keys // n_kruns
        if r == 1:
            def copy(c, carry):
                store_split(kd, conv_rows(c), k_ref[0, conv_rows(c), :], 0.0)
                store_split(vd, conv_rows(c), v_ref[0, conv_rows(c), :], 1.0)
                return carry

            lax.fori_loop(0, seq // ATT_CONV_ROWS, copy, 0)
        elif pattern == 1:
            def copy_base(c, carry):
                store_split(kd, conv_rows(c), kb_ref[0, conv_rows(c), :], 0.0)
                store_split(vd, conv_rows(c), vb_ref[0, conv_rows(c), :], 1.0)
                return carry

            lax.fori_loop(0, seq // ATT_CONV_ROWS, copy_base, 0)

        n_idx = lax.broadcasted_iota(jnp.int32, (ATT_QBLK, n_keys), 0)
        q_rel = n_runs * (n_idx % run_len) + n_idx // run_len
        c_idx = lax.broadcasted_iota(jnp.int32, (ATT_QBLK, n_keys), 1)
        k_rel = n_kruns * (c_idx % krun_len) + c_idx // krun_len
        offsets = (0, ATT_HALF_WINDOW, 2 * ATT_HALF_WINDOW) if blocks_per_class > 1 else (0,)
        for variant, off in enumerate(offsets):
            dist = jnp.abs(k_rel - off - q_rel)
            for h in range(2):
                scaled = (-slopes[h] * float(r)) * dist.astype(F32)
                bias[h, variant, :, 0:n_keys] = jnp.where(dist <= ATT_HALF_WINDOW, scaled, NEG)

        def geometry(t, r=r, blocks_per_class=blocks_per_class, n_runs=n_runs, run_len=run_len,
                     n_kruns=n_kruns, krun_len=krun_len, n_keys=n_keys):
            rho = t // blocks_per_class
            lb = t % blocks_per_class
            if blocks_per_class > 1:
                variant = jnp.where(lb == 0, 0, jnp.where(lb == blocks_per_class - 1, 2, 1))
            else:
                variant = 0
            runs = [pl.ds(pl.multiple_of((rho + r * j) * base_len + lb * run_len, 8), run_len)
                    for j in range(n_runs)]
            if r == 1:
                k_runs = [pl.ds(pl.multiple_of(t * ATT_QBLK - variant * ATT_HALF_WINDOW, ATT_HALF_WINDOW), n_keys)]
            else:
                start = lb * run_len - variant * (ATT_HALF_WINDOW // n_kruns)
                k_runs = [pl.ds(pl.multiple_of((rho + r * j) * base_len + start, 16), krun_len)
                          for j in range(n_kruns)]
            return variant, k_runs, runs

        def load_runs(ref, runs):
            return jnp.concatenate([ref[rows, :] for rows in runs], axis=0) if len(runs) > 1 else ref[runs[0], :]

        def store_runs(ref, runs, x, run_len=run_len):
            for j, rows in enumerate(runs):
                ref[rows, :] = x[j * run_len:(j + 1) * run_len, :]

        def group(g, carry, n_keys=n_keys, pattern=pattern, geometry=geometry, load_runs=load_runs,
                  store_runs=store_runs):
            for i in range(ATT_GROUP):
                variant, k_runs, runs = geometry(g * ATT_GROUP + i)
                q = load_runs(qs, runs).astype(BF16)
                for h in range(2):
                    sc = lax.dot_general(q, load_runs(kd.at[h], k_runs), (((1,), (1,)), ((), ())),
                                         preferred_element_type=F32)
                    sc_buf[i, h, :, 0:n_keys] = sc + bias[h, variant, :, 0:n_keys]
            for i in range(ATT_GROUP):
                ms = []
                for h in range(2):
                    sc = sc_buf[i, h, :, 0:n_keys]
                    m = jnp.max(sc, axis=-1, keepdims=True)
                    p_buf[i, h, :, 0:n_keys] = jnp.exp(sc - m).astype(BF16)
                    ms.append(m)
                m_buf[i] = jnp.where(first_head, ms[0], ms[1])
            for i in range(ATT_GROUP):
                _, k_runs, runs = geometry(g * ATT_GROUP + i)
                pv0 = jnp.dot(p_buf[i, 0, :, 0:n_keys], load_runs(vd.at[0], k_runs), preferred_element_type=F32)
                pv1 = jnp.dot(p_buf[i, 1, :, 0:n_keys], load_runs(vd.at[1], k_runs), preferred_element_type=F32)
                a_new = jnp.where(first_head, pv0, pv1)
                l_new = pltpu.roll(jnp.where(first_head, pv1, pv0), ATT_HEAD_DIM, 1)
                m_new = m_buf[i]
                if pattern == 0:
                    store_runs(m_s, runs, m_new)
                    store_runs(l_s, runs, l_new)
                    store_runs(a_s, runs, a_new)
                else:
                    m_old = load_runs(m_s, runs)
                    m_tot = jnp.maximum(m_old, m_new)
                    w_old = jnp.exp(m_old - m_tot)
                    w_new = jnp.exp(m_new - m_tot)
                    store_runs(m_s, runs, m_tot)
                    store_runs(l_s, runs, w_old * load_runs(l_s, runs) + w_new * l_new)
                    store_runs(a_s, runs, w_old * load_runs(a_s, runs) + w_new * a_new)
            return carry

        lax.fori_loop(0, seq // (ATT_QBLK * ATT_GROUP), group, 0)

    def scatter(t, carry):
        rows = pl.ds(pl.multiple_of(t * ATT_QBLK, ATT_QBLK), ATT_QBLK)
        xf[class_major_rows(t, ATT_MAX_DILATION), :] = a_s[rows, :] / l_s[rows, :]
        return carry

    lax.fori_loop(0, seq // ATT_QBLK, scatter, 0)

    def finish(c, carry):
        o_ref[0, conv_rows(c), :] = xf[conv_rows(c), :].astype(BF16)
        return carry

    lax.fori_loop(0, seq // ATT_CONV_ROWS, finish, 0)


def _attention(qkv, qkv_base, slopes):
    b, s, _ = qkv.shape
    n_pairs = ATT_WIDTH // LANES
    mode = {} if s * LANES * 2 * 5 * 2 <= ATT_DOUBLE_BUFFER_BYTES else {"pipeline_mode": pl.Buffered(1)}
    blk = lambda off: pl.BlockSpec((1, s, LANES), lambda bi, j, off=off: (bi, 0, off + j), **mode)
    return pl.pallas_call(
        functools.partial(_attn_kernel, seq=s),
        grid=(b, n_pairs),
        in_specs=[pl.BlockSpec(memory_space=pltpu.SMEM), blk(n_pairs), blk(2 * n_pairs),
                  blk(0), blk(n_pairs), blk(2 * n_pairs)],
        out_specs=pl.BlockSpec((1, s, LANES), lambda bi, j: (bi, 0, j)),
        out_shape=jax.ShapeDtypeStruct((b, s, ATT_WIDTH), BF16),
        scratch_shapes=[
            pltpu.VMEM((s, LANES), F32),
            pltpu.VMEM((s, LANES), F32),
            pltpu.VMEM((2, s, LANES), BF16),
            pltpu.VMEM((2, s, LANES), BF16),
            pltpu.VMEM((2, 3, ATT_QBLK, 2 * ATT_QBLK), F32),
            pltpu.VMEM((ATT_GROUP, 2, ATT_QBLK, 2 * ATT_QBLK), F32),
            pltpu.VMEM((ATT_GROUP, 2, ATT_QBLK, 2 * ATT_QBLK), BF16),
            pltpu.VMEM((ATT_GROUP, ATT_QBLK, LANES), F32),
            pltpu.VMEM((s, LANES), F32),
            pltpu.VMEM((s, LANES), F32),
            pltpu.VMEM((s, LANES), F32),
        ],
        compiler_params=_params(("parallel", "parallel")),
        name="dilated_attention",
    )(slopes, qkv, qkv, qkv_base, qkv_base, qkv_base)


def _even_out_kernel(x_ref, f_ref, a_ref, w_ref, o_ref):
    acc = x_ref[0]
    acc = acc + jnp.dot(f_ref[...], w_ref[0:FNET_WIDTH, :], preferred_element_type=F32)
    acc = acc + jnp.dot(a_ref[0], w_ref[FNET_WIDTH:, :], preferred_element_type=F32)
    o_ref[0] = acc


def _even_out(x, fmix, att, w_out, tm=512):
    b, s, d = x.shape
    return pl.pallas_call(
        _even_out_kernel,
        grid=(b, s // tm),
        in_specs=[
            pl.BlockSpec((1, tm, d), lambda bi, i: (bi, i, 0)),
            pl.BlockSpec((tm, FNET_WIDTH), lambda bi, i: (i, bi)),
            pl.BlockSpec((1, tm, ATT_WIDTH), lambda bi, i: (bi, i, 0)),
            pl.BlockSpec((d, d), lambda bi, i: (0, 0)),
        ],
        out_specs=pl.BlockSpec((1, tm, d), lambda bi, i: (bi, i, 0)),
        out_shape=jax.ShapeDtypeStruct((b, s, d), F32),
        compiler_params=_params(("parallel", "parallel")),
        name="even_out_proj",
    )(x, fmix, att, w_out)


def _resident(shape):
    return pl.BlockSpec(shape, lambda *_: (0,) * len(shape), pipeline_mode=pl.Buffered(1))


def _swiglu_kernel(x_ref, ln_ref, wg_ref, wu_ref, wd_ref, o_ref):
    x = x_ref[...]
    h = _rms(x, ln_ref[...]).astype(BF16)
    g = jnp.dot(h, wg_ref[...], preferred_element_type=F32)
    u = jnp.dot(h, wu_ref[...], preferred_element_type=F32)
    act = (_silu(g) * u).astype(BF16)
    o_ref[...] = x + jnp.dot(act, wd_ref[...], preferred_element_type=F32)


def _swiglu(x, ln, wg, wu, wd, tm=512):
    n, d = x.shape
    dff = wg.shape[1]
    return pl.pallas_call(
        _swiglu_kernel,
        grid=(n // tm,),
        in_specs=[
            pl.BlockSpec((tm, d), lambda i: (i, 0)),
            _resident((1, d)), _resident((d, dff)), _resident((d, dff)), _resident((dff, d)),
        ],
        out_specs=pl.BlockSpec((tm, d), lambda i: (i, 0)),
        out_shape=jax.ShapeDtypeStruct((n, d), F32),
        compiler_params=_params(("parallel",)),
        name="swiglu_ffn",
    )(x, ln, wg, wu, wd)


ODD_IN_COLS = 2048


def _odd_in_kernel(x_ref, ln_ref, w_ref, o_ref):
    h = _rms(x_ref[...], ln_ref[...]).astype(BF16)
    for c0 in range(0, ODD_IN, ODD_IN_COLS):
        cols = slice(c0, c0 + ODD_IN_COLS)
        o_ref[:, cols] = jnp.dot(h, w_ref[:, cols], preferred_element_type=F32).astype(BF16)


def _odd_in(x, ln, w_in, tm=512):
    n, d = x.shape
    return pl.pallas_call(
        _odd_in_kernel,
        grid=(n // tm,),
        in_specs=[pl.BlockSpec((tm, d), lambda i: (i, 0)), _resident((1, d)), _resident((d, ODD_IN))],
        out_specs=pl.BlockSpec((tm, ODD_IN), lambda i: (i, 0)),
        out_shape=jax.ShapeDtypeStruct((n, ODD_IN), BF16),
        compiler_params=_params(("parallel",)),
        name="odd_in_proj",
    )(x, ln, w_in)


RET_CHUNK = 256
RET_TILE = 1024


def _chunk_positions():
    return lax.broadcasted_iota(jnp.int32, (RET_CHUNK, 1), 0).astype(F32)


def _kt_v(k_scaled, v):
    return lax.dot_general(k_scaled, v, (((0,), (0,)), ((), ())), preferred_element_type=F32)


def _ret_bwd_state_kernel(lgb_ref, k_ref, v_ref, sb_ref, state):
    @pl.when(pl.program_id(1) == 0)
    def _():
        state[...] = jnp.zeros_like(state)

    pos = _chunk_positions()
    for head in range(RET_HEADS):
        lg = lgb_ref[head]
        key_decay = jnp.exp(lg * pos)
        chunk_decay = jnp.exp(jnp.full((1, RET_DV), lg * RET_CHUNK, F32))
        k_cols = slice(head * RET_DK, (head + 1) * RET_DK)
        v_cols = slice(head * RET_DV, (head + 1) * RET_DV)
        for c in reversed(range(RET_TILE // RET_CHUNK)):
            rows = slice(c * RET_CHUNK, (c + 1) * RET_CHUNK)
            sb_ref[0, head, c] = state[head].astype(BF16)
            kd = (k_ref[0, rows, k_cols].astype(F32) * key_decay).astype(BF16)
            state[head] = state[head] * chunk_decay + _kt_v(kd, v_ref[0, rows, v_cols])


def _ret_bwd_states(z, lgb):
    b, s, _ = z.shape
    n = s // RET_CHUNK
    n_tiles = s // RET_TILE
    per_tile = RET_TILE // RET_CHUNK
    return pl.pallas_call(
        _ret_bwd_state_kernel,
        grid=(b, n_tiles),
        in_specs=[
            pl.BlockSpec(memory_space=pltpu.SMEM),
            pl.BlockSpec((1, RET_TILE, RET_QK_WIDTH), lambda bi, t: (bi, n_tiles - 1 - t, 1)),
            pl.BlockSpec((1, RET_TILE, RET_V_WIDTH), lambda bi, t: (bi, n_tiles - 1 - t, 1)),
        ],
        out_specs=pl.BlockSpec((1, RET_HEADS, per_tile, RET_DK, RET_DV), lambda bi, t: (bi, 0, n_tiles - 1 - t, 0, 0)),
        out_shape=jax.ShapeDtypeStruct((b, RET_HEADS, n, RET_DK, RET_DV), BF16),
        scratch_shapes=[pltpu.VMEM((RET_HEADS, RET_DK, RET_DV), F32)],
        compiler_params=_params(("parallel", "arbitrary")),
        name="retention_bwd_states",
    )(lgb, z, z)


def _ret_main_kernel(lgf_ref, lgb_ref, q_ref, k_ref, v_ref, g_ref, sb_ref, gn_ref, o_ref, state):
    head = pl.program_id(1)

    @pl.when(pl.program_id(2) == 0)
    def _():
        state[...] = jnp.zeros_like(state)

    lgf = lgf_ref[head]
    lgb = lgb_ref[head]
    c = RET_CHUNK
    pos = _chunk_positions()
    rel = (lax.broadcasted_iota(jnp.int32, (c, c), 0) - lax.broadcasted_iota(jnp.int32, (c, c), 1)).astype(F32)
    decay = jnp.exp(jnp.where(rel >= 0, lgf * rel, -lgb * rel))
    q_decay_f = jnp.exp(lgf * (pos + 1.0))
    q_decay_b = jnp.exp(lgb * (c - pos))
    k_decay_f = jnp.exp(lgf * (c - 1.0 - pos))
    chunk_decay = jnp.exp(jnp.full((1, RET_DV), lgf * c, F32))
    gn = gn_ref[...]

    for ci in range(RET_TILE // RET_CHUNK):
        rows = slice(ci * c, (ci + 1) * c)
        q = q_ref[0, rows, :]
        k = k_ref[0, rows, :]
        v = v_ref[0, rows, :]
        qf32 = q.astype(F32)
        sc = lax.dot_general(q, k, (((1,), (1,)), ((), ())), preferred_element_type=F32)
        y = jnp.dot((sc * decay).astype(BF16), v, preferred_element_type=F32)
        y = y + jnp.dot((qf32 * q_decay_f).astype(BF16), state[...].astype(BF16), preferred_element_type=F32)
        y = y + jnp.dot((qf32 * q_decay_b).astype(BF16), sb_ref[0, 0, ci], preferred_element_type=F32)
        kd = (k.astype(F32) * k_decay_f).astype(BF16)
        state[...] = state[...] * chunk_decay + _kt_v(kd, v)
        yn = _rms(y, gn)
        o_ref[0, rows, :] = (_silu(g_ref[0, rows, :].astype(F32)) * yn).astype(BF16)


def _ret_main(z, sb, lgf, lgb, gn):
    b, s, _ = z.shape
    n_tiles = s // RET_TILE
    per_tile = RET_TILE // RET_CHUNK
    k_off = RET_QK_WIDTH // RET_DK
    v_off = 2 * RET_QK_WIDTH // RET_DV
    g_off = (2 * RET_QK_WIDTH + RET_V_WIDTH) // RET_DV
    smem = pl.BlockSpec(memory_space=pltpu.SMEM)
    return pl.pallas_call(
        _ret_main_kernel,
        grid=(b, RET_HEADS, n_tiles),
        in_specs=[
            smem, smem,
            pl.BlockSpec((1, RET_TILE, RET_DK), lambda bi, h, t: (bi, t, h)),
            pl.BlockSpec((1, RET_TILE, RET_DK), lambda bi, h, t: (bi, t, k_off + h)),
            pl.BlockSpec((1, RET_TILE, RET_DV), lambda bi, h, t: (bi, t, v_off + h)),
            pl.BlockSpec((1, RET_TILE, RET_DV), lambda bi, h, t: (bi, t, g_off + h)),
            pl.BlockSpec((1, 1, per_tile, RET_DK, RET_DV), lambda bi, h, t: (bi, h, t, 0, 0)),
            pl.BlockSpec((1, RET_DV), lambda bi, h, t: (0, h)),
        ],
        out_specs=pl.BlockSpec((1, RET_TILE, RET_DV), lambda bi, h, t: (bi, t, h)),
        out_shape=jax.ShapeDtypeStruct((b, s, RET_V_WIDTH), BF16),
        scratch_shapes=[pltpu.VMEM((RET_DK, RET_DV), F32)],
        compiler_params=_params(("parallel", "parallel", "arbitrary")),
        name="retention_main",
    )(lgf, lgb, z, z, z, z, sb, gn)


def _odd_out_router_kernel(y_ref, x_ref, w_ref, ln_ref, wr_ref, xo_ref, h_ref, idx_ref, gate_ref, hn_prev):
    @pl.when(pl.program_id(0) == 0)
    def _():
        hn_prev[...] = jnp.zeros_like(hn_prev)

    prev = hn_prev[...]
    cols = [jnp.sum(prev * wr_ref[e:e + 1, :], axis=-1, keepdims=True) for e in range(N_EXPERTS)]
    pad = jnp.full((prev.shape[0], LANES - N_EXPERTS), -jnp.inf, F32)
    logits = jnp.transpose(jnp.concatenate(cols + [pad], axis=1))[0:N_EXPERTS, :]
    row = lax.broadcasted_iota(jnp.int32, logits.shape, 0).astype(F32)
    none = float(N_EXPERTS)
    m1 = jnp.max(logits, axis=0, keepdims=True)
    i1 = jnp.min(jnp.where(logits == m1, row, none), axis=0, keepdims=True)
    rest = jnp.where(row == i1, -jnp.inf, logits)
    m2 = jnp.max(rest, axis=0, keepdims=True)
    i2 = jnp.min(jnp.where(rest == m2, row, none), axis=0, keepdims=True)
    e2 = jnp.exp(m2 - m1)
    idx_ref[0:1, :] = i1.astype(jnp.int32)
    idx_ref[1:2, :] = i2.astype(jnp.int32)
    gate_ref[0:1, :] = 1.0 / (1.0 + e2)
    gate_ref[1:2, :] = e2 / (1.0 + e2)

    x = x_ref[...] + jnp.dot(y_ref[...], w_ref[...], preferred_element_type=F32)
    xo_ref[...] = x
    hn = _rms(x, ln_ref[...])
    h_ref[...] = hn.astype(BF16)
    hn_prev[...] = hn


def _odd_out_router(y, x, w_out, ln, wr_t, tm=512):
    n, d = x.shape
    n_tiles = n // tm
    full = lambda shp: pl.BlockSpec(shp, lambda i: (0,) * len(shp))
    cur = lambda i: (jnp.minimum(i, n_tiles - 1), 0)
    prev = lambda i: (0, jnp.maximum(i - 1, 0))
    return pl.pallas_call(
        _odd_out_router_kernel,
        grid=(n_tiles + 1,),
        in_specs=[
            pl.BlockSpec((tm, RET_V_WIDTH), cur),
            pl.BlockSpec((tm, d), cur),
            full((RET_V_WIDTH, d)), full((1, d)), full((N_EXPERTS, d)),
        ],
        out_specs=[
            pl.BlockSpec((tm, d), cur),
            pl.BlockSpec((tm, d), cur),
            pl.BlockSpec((2, tm), prev),
            pl.BlockSpec((2, tm), prev),
        ],
        out_shape=[
            jax.ShapeDtypeStruct((n, d), F32),
            jax.ShapeDtypeStruct((n, d), BF16),
            jax.ShapeDtypeStruct((2, n), jnp.int32),
            jax.ShapeDtypeStruct((2, n), F32),
        ],
        scratch_shapes=[pltpu.VMEM((tm, d), F32)],
        compiler_params=_params(("arbitrary",)),
        name="odd_out_proj_router",
    )(y, x, w_out, ln, wr_t)


MOE_TM = 1024
MOE_TF = 1792
MOE_SUB = 512


def _moe_kernel(tile_expert_ref, n_used_ref, xs_ref, gate_ref, w1_ref, w3_ref, w2_ref, o_ref, acc_ref):
    t = pl.program_id(0)
    f = pl.program_id(1)
    last = pl.num_programs(1) - 1
    used = t < n_used_ref[0]

    @pl.when(used)
    def _():
        for r0 in range(0, MOE_TM, MOE_SUB):
            rows = slice(r0, r0 + MOE_SUB)
            xs = xs_ref[rows, :]
            g = jnp.dot(xs, w1_ref[...], preferred_element_type=F32)
            u = jnp.dot(xs, w3_ref[...], preferred_element_type=F32)
            act = (_silu(g) * u).astype(BF16)
            part = jnp.dot(act, w2_ref[...], preferred_element_type=F32)

            @pl.when(f == 0)
            def _():
                acc_ref[rows, :] = part

            @pl.when(jnp.logical_and(f > 0, f < last))
            def _():
                acc_ref[rows, :] += part

            @pl.when(f == last)
            def _():
                o_ref[rows, :] = ((acc_ref[rows, :] + part) * gate_ref[rows, 0:1]).astype(BF16)

    @pl.when(jnp.logical_and(jnp.logical_not(used), f == last))
    def _():
        o_ref[...] = jnp.zeros_like(o_ref)


def _moe_experts(xs, gate_rows, tile_expert, n_used, w1, w3, w2):
    p, d = xs.shape
    n_tiles = p // MOE_TM
    n_f = D_FF_EXPERT // MOE_TF
    assert n_f >= 2 and n_f * MOE_TF == D_FF_EXPERT

    def row_tile(t, f, te, nu):
        return (jnp.minimum(t, nu[0] - 1), 0)

    def f_idx(t, f, nu):
        return jnp.where(t < nu[0], f, n_f - 1)

    grid_spec = pltpu.PrefetchScalarGridSpec(
        num_scalar_prefetch=2,
        grid=(n_tiles, n_f),
        in_specs=[
            pl.BlockSpec((MOE_TM, d), row_tile),
            pl.BlockSpec((MOE_TM, LANES), row_tile),
            pl.BlockSpec((None, d, MOE_TF), lambda t, f, te, nu: (te[t], 0, f_idx(t, f, nu))),
            pl.BlockSpec((None, d, MOE_TF), lambda t, f, te, nu: (te[t], 0, f_idx(t, f, nu))),
            pl.BlockSpec((None, MOE_TF, d), lambda t, f, te, nu: (te[t], f_idx(t, f, nu), 0)),
        ],
        out_specs=pl.BlockSpec((MOE_TM, d), lambda t, f, te, nu: (t, 0)),
        scratch_shapes=[pltpu.VMEM((MOE_TM, d), F32)],
    )
    return pl.pallas_call(
        _moe_kernel,
        grid_spec=grid_spec,
        out_shape=jax.ShapeDtypeStruct((p, d), BF16),
        compiler_params=_params(("arbitrary", "arbitrary")),
        name="moe_experts",
    )(tile_expert, n_used, xs, gate_rows, w1, w3, w2)


def _combine_kernel(x_ref, y_ref, o_ref):
    o_ref[...] = x_ref[...] + y_ref[0].astype(F32) + y_ref[1].astype(F32)


def _combine(x, y2, tm=1024):
    n, d = x.shape
    return pl.pallas_call(
        _combine_kernel,
        grid=(n // tm,),
        in_specs=[pl.BlockSpec((tm, d), lambda i: (i, 0)), pl.BlockSpec((2, tm, d), lambda i: (0, i, 0))],
        out_specs=pl.BlockSpec((tm, d), lambda i: (i, 0)),
        out_shape=jax.ShapeDtypeStruct((n, d), F32),
        compiler_params=_params(("parallel",)),
        name="moe_combine",
    )(x, y2)


def _route(idx, gates, n):
    e_flat = idx.reshape(-1)
    n_pairs = e_flat.shape[0]
    pair_id = jnp.arange(n_pairs, dtype=jnp.int32)
    order = (jnp.sort(e_flat * n_pairs + pair_id) % n_pairs).astype(jnp.int32)
    rank = jnp.argsort(order).astype(jnp.int32)
    counts = jnp.sum(e_flat[None, :] == jnp.arange(N_EXPERTS, dtype=jnp.int32)[:, None], axis=1).astype(jnp.int32)
    starts = jnp.cumsum(counts) - counts
    tiles_per_expert = (counts + MOE_TM - 1) // MOE_TM
    tile_ends = jnp.cumsum(tiles_per_expert)
    padded_starts = (tile_ends - tiles_per_expert) * MOE_TM
    dest = rank + (padded_starts - starts)[e_flat]
    n_rows = n_pairs + N_EXPERTS * MOE_TM
    n_tiles = n_rows // MOE_TM
    tile_expert = jnp.minimum(
        jnp.sum(jnp.arange(n_tiles, dtype=jnp.int32)[:, None] >= tile_ends[None, :], axis=1), N_EXPERTS - 1
    ).astype(jnp.int32)
    row_expert = jnp.repeat(tile_expert, MOE_TM)
    row_in_expert = jnp.arange(n_rows, dtype=jnp.int32) - padded_starts[row_expert]
    row_valid = row_in_expert < counts[row_expert]
    row_pair = order[jnp.clip(starts[row_expert] + row_in_expert, 0, n_pairs - 1)]
    src_token = jnp.where(row_valid, row_pair % n, 0).astype(jnp.int32)
    row_gate = jnp.where(row_valid, gates.reshape(-1)[row_pair], 0.0)
    n_used = tile_ends[-1:].astype(jnp.int32)
    return src_token, row_gate, dest, tile_expert, n_used


def _channel_dft():
    c = jnp.arange(FNET_GROUP_DIM, dtype=jnp.int32)
    ang = ((c[:, None] * c[None, :]) % FNET_GROUP_DIM).astype(F32) * (2.0 * math.pi / FNET_GROUP_DIM)
    eye = jnp.eye(FNET_GROUPS, dtype=F32)
    scale = FNET_GROUP_DIM ** -0.5
    return jnp.concatenate([jnp.kron(eye, jnp.cos(ang)) * scale, -jnp.kron(eye, jnp.sin(ang)) * scale], axis=1).astype(BF16)


def _sequence_dft(s):
    hi = s // 64
    k = jnp.arange(s, dtype=jnp.int32)
    j_hi = jnp.arange(hi, dtype=jnp.int32)
    j_lo = jnp.arange(64, dtype=jnp.int32)
    w = 2.0 * math.pi / s
    ang_a = (((j_hi[:, None] * k[None, :]) % hi) * 64).astype(F32) * w
    ang_b = ((j_lo[:, None] * k[None, :]) % s).astype(F32) * w
    scale = s ** -0.5
    ca, sa = jnp.cos(ang_a) * scale, jnp.sin(ang_a) * scale
    cb, sb = jnp.cos(ang_b), jnp.sin(ang_b)
    ca2 = jnp.concatenate([ca, ca], axis=1)[:, None, :]
    sa2 = jnp.concatenate([sa, sa], axis=1)[:, None, :]
    p = jnp.concatenate([cb, sb], axis=1)[None, :, :]
    q = jnp.concatenate([sb, -cb], axis=1)[None, :, :]
    return (ca2 * p - sa2 * q).reshape(s, 2 * s).astype(BF16)


def _trunk_to_router(x, p):
    b, s, d = x.shape
    n = b * s
    y, qkv, qkv_base = _even_in(x, p["ln_mix_e"], p["w_in_e"], p["chan_dft"], p["head_sum"], p["qn"], p["kn"])
    fmix = _matmul(_sequence_dft(s), y.reshape(2 * s, b * FNET_WIDTH))
    att = _attention(qkv, qkv_base.reshape(b, s, 3 * ATT_WIDTH), p["slopes"])
    x = _even_out(x, fmix, att, p["w_out_e"])
    x = _swiglu(x.reshape(n, d), p["ln_ffn_e"], p["w_gate_e"], p["w_up_e"], p["w_down_e"])
    z = _odd_in(x, p["ln_mix_o"], p["w_in_o"]).reshape(b, s, ODD_IN)
    sb = _ret_bwd_states(z, p["lgb"])
    yr = _ret_main(z, sb, p["lgf"], p["lgb"], p["ret_gn"])
    x, h, idx, gates = _odd_out_router(yr.reshape(n, RET_V_WIDTH), x, p["w_out_o"], p["ln_ffn_o"], p["w_router_t"])
    return x, h, _route(idx, gates, n)


def _dispatch(h, route):
    src_token, row_gate = route[0], route[1]
    xs = h.at[src_token].get(mode="promise_in_bounds")
    return xs, jnp.broadcast_to(row_gate[:, None], (row_gate.shape[0], LANES))


def _experts(xs, gate_rows, route, p):
    return _moe_experts(xs, gate_rows, route[3], route[4], p["moe_w1"], p["moe_w3"], p["moe_w2"])


def _collect(ye, route, n):
    return ye.at[route[2]].get(mode="promise_in_bounds").reshape(2, n, D_MODEL)


def _both_trunks(x_a, x_b, p):
    outs = []
    for x in (x_a, x_b):
        n = x.shape[0] * x.shape[1]
        xr, h, route = _trunk_to_router(x, p)
        xs, gate_rows = _dispatch(h, route)
        ye = _experts(xs, gate_rows, route, p)
        outs.append(_combine(xr, _collect(ye, route, n)).reshape(x.shape))
    return tuple(outs)


def kernel(x_prompt, x_sample, ln_mix_e, w_in_e, w_out_e, qn_e, kn_e, ln_ffn_e, w_gate_e, w_up_e, w_down_e, ln_mix_o, w_in_o, w_out_o, logdecay_fwd, logdecay_bwd, ret_gn, ln_ffn_o, w_router, moe_w1, moe_w3, moe_w2):
    row = lambda w: w.reshape(1, -1).astype(F32)
    k_scale = jnp.concatenate([
        jnp.ones((RET_QK_WIDTH,), F32), jnp.full((RET_QK_WIDTH,), RET_DK ** -0.5, F32),
        jnp.ones((2 * RET_V_WIDTH,), F32)])
    head_id = jnp.arange(LANES, dtype=jnp.int32) // ATT_HEAD_DIM
    p = {
        "ln_mix_e": row(ln_mix_e[0]),
        "w_in_e": w_in_e[0].astype(BF16),
        "w_out_e": w_out_e[0].astype(BF16),
        "qn": row(jnp.tile(qn_e[0], 2) * (ATT_HEAD_DIM ** -0.5)),
        "kn": row(jnp.tile(kn_e[0], 2)),
        "ln_ffn_e": row(ln_ffn_e[0]),
        "w_gate_e": w_gate_e[0].astype(BF16),
        "w_up_e": w_up_e[0].astype(BF16),
        "w_down_e": w_down_e[0].astype(BF16),
        "ln_mix_o": row(ln_mix_o[0]),
        "w_in_o": (w_in_o[0] * k_scale[None, :]).astype(BF16),
        "w_out_o": w_out_o[0].astype(BF16),
        "lgf": logdecay_fwd[0].astype(F32),
        "lgb": logdecay_bwd[0].astype(F32),
        "ret_gn": row(ret_gn[0]),
        "ln_ffn_o": row(ln_ffn_o[0]),
        "w_router_t": w_router[0].T.astype(F32),
        "moe_w1": moe_w1[0].astype(BF16),
        "moe_w3": moe_w3[0].astype(BF16),
        "moe_w2": moe_w2[0].astype(BF16),
        "chan_dft": _channel_dft(),
        "head_sum": (head_id[:, None] == head_id[None, :]).astype(BF16),
        "slopes": jnp.exp2(-8.0 * (jnp.arange(ATT_HEADS, dtype=F32) + 1.0) / ATT_HEADS),
    }
    return _both_trunks(x_prompt, x_sample, p)
```

```python
import functools
import math

import jax
import jax.numpy as jnp
from jax import lax
from jax.experimental import pallas as pl
from jax.experimental.pallas import tpu as pltpu

D_MODEL = 1024
FNET_GROUPS = 4
FNET_GROUP_DIM = 64
FNET_WIDTH = 256
ATT_HEAD_DIM = 64
ATT_HEADS = 12
ATT_WIDTH = 768
ATT_HALF_WINDOW = 64
DILATIONS = (1, 4, 16)
EVEN_IN = FNET_WIDTH + 3 * ATT_WIDTH
RET_HEADS = 4
RET_DK = 256
RET_DV = 512
RET_QK_WIDTH = 1024
RET_V_WIDTH = 2048
ODD_IN = 6144
D_FF = 2816
N_EXPERTS = 8
D_FF_EXPERT = 3584
EPS = 1e-6
NEG = -1e30

LANES = 128
VMEM_LIMIT_BYTES = 56 * 2**20

F32 = jnp.float32
BF16 = jnp.bfloat16


def _params(semantics):
    return pltpu.CompilerParams(dimension_semantics=semantics, vmem_limit_bytes=VMEM_LIMIT_BYTES)


def _rms(x, w):
    return x * lax.rsqrt(jnp.mean(x * x, axis=-1, keepdims=True) + EPS) * w


def _silu(x):
    return x / (1.0 + jnp.exp(-x))


QKV_CLASSES = 16


def _even_in_kernel(x_ref, ln_ref, w_ref, cd_ref, hs_ref, qn_ref, kn_ref, y_ref, qkv_ref, base_ref, zs_ref):
    h = _rms(x_ref[0], ln_ref[...]).astype(BF16)
    u = jnp.dot(h, w_ref[:, 0:FNET_WIDTH], preferred_element_type=F32).astype(BF16)
    yy = jnp.dot(u, cd_ref[...], preferred_element_type=F32)
    y_ref[0] = yy[:, :FNET_WIDTH].astype(BF16)
    y_ref[1] = yy[:, FNET_WIDTH:].astype(BF16)
    for part, n_ref in ((0, qn_ref), (1, kn_ref)):
        c0 = FNET_WIDTH + part * ATT_WIDTH
        z = jnp.dot(h, w_ref[:, c0:c0 + ATT_WIDTH], preferred_element_type=F32)
        for j in range(ATT_WIDTH // LANES):
            zj = z[:, j * LANES:(j + 1) * LANES]
            ss = jnp.dot((zj * zj).astype(BF16), hs_ref[...], preferred_element_type=F32)
            zn = zj * lax.rsqrt(ss * (1.0 / ATT_HEAD_DIM) + EPS) * n_ref[...]
            cols = slice(part * ATT_WIDTH + j * LANES, part * ATT_WIDTH + (j + 1) * LANES)
            qkv_ref[0, :, cols] = zn.astype(BF16)
            zs_ref[part * (ATT_WIDTH // LANES) + j] = zn
    c0 = FNET_WIDTH + 2 * ATT_WIDTH
    v = jnp.dot(h, w_ref[:, c0:c0 + ATT_WIDTH], preferred_element_type=F32)
    qkv_ref[0, :, 2 * ATT_WIDTH:3 * ATT_WIDTH] = v.astype(BF16)
    for j in range(ATT_WIDTH // LANES):
        zs_ref[2 * (ATT_WIDTH // LANES) + j] = v[:, j * LANES:(j + 1) * LANES]
    rows_per_class = zs_ref.shape[1] // QKV_CLASSES
    for rho in range(QKV_CLASSES):
        for c in range(zs_ref.shape[0]):
            base_ref[0, rho, :, c * LANES:(c + 1) * LANES] = (
                zs_ref[c, pl.ds(rho, rows_per_class, stride=QKV_CLASSES), :].astype(BF16))


def _even_in(x, ln, w_in, cd, hs, qn, kn, tm=512):
    b, s, d = x.shape
    full = lambda shp: pl.BlockSpec(shp, lambda bi, i: (0,) * len(shp))
    return pl.pallas_call(
        _even_in_kernel,
        grid=(b, s // tm),
        in_specs=[
            pl.BlockSpec((1, tm, d), lambda bi, i: (bi, i, 0)),
            full((1, d)), full((d, EVEN_IN)), full((FNET_WIDTH, 2 * FNET_WIDTH)), full((LANES, LANES)),
            full((1, LANES)), full((1, LANES)),
        ],
        out_specs=[
            pl.BlockSpec((2, tm, FNET_WIDTH), lambda bi, i: (0, i, bi)),
            pl.BlockSpec((1, tm, 3 * ATT_WIDTH), lambda bi, i: (bi, i, 0)),
            pl.BlockSpec((1, QKV_CLASSES, tm // QKV_CLASSES, 3 * ATT_WIDTH), lambda bi, i: (bi, 0, i, 0)),
        ],
        out_shape=[
            jax.ShapeDtypeStruct((2, s, b * FNET_WIDTH), BF16),
            jax.ShapeDtypeStruct((b, s, 3 * ATT_WIDTH), BF16),
            jax.ShapeDtypeStruct((b, QKV_CLASSES, s // QKV_CLASSES, 3 * ATT_WIDTH), BF16),
        ],
        scratch_shapes=[pltpu.VMEM((3 * ATT_WIDTH // LANES, tm, LANES), F32)],
        compiler_params=_params(("parallel", "parallel")),
        name="even_in_proj",
    )(x, ln, w_in, cd, hs, qn, kn)


def _matmul_kernel(a_ref, b_ref, o_ref, acc_ref):
    @pl.when(pl.program_id(2) == 0)
    def _():
        acc_ref[...] = jnp.zeros_like(acc_ref)

    acc_ref[...] += jnp.dot(a_ref[...], b_ref[...], preferred_element_type=F32)

    @pl.when(pl.program_id(2) == pl.num_programs(2) - 1)
    def _():
        o_ref[...] = acc_ref[...].astype(o_ref.dtype)


def _matmul(a, b, tm=1024, tn=2048, tk=1024):
    m, k = a.shape
    _, n = b.shape
    tm, tn, tk = min(tm, m), min(tn, n), min(tk, k)
    return pl.pallas_call(
        _matmul_kernel,
        grid=(m // tm, n // tn, k // tk),
        in_specs=[pl.BlockSpec((tm, tk), lambda i, j, kk: (i, kk)),
                  pl.BlockSpec((tk, tn), lambda i, j, kk: (kk, j))],
        out_specs=pl.BlockSpec((tm, tn), lambda i, j, kk: (i, j)),
        out_shape=jax.ShapeDtypeStruct((m, n), BF16),
        scratch_shapes=[pltpu.VMEM((tm, tn), F32)],
        compiler_params=_params(("parallel", "parallel", "arbitrary")),
        name="seq_dft_matmul",
    )(a, b)


ATT_QBLK = 128
ATT_CONV_ROWS = 512
ATT_GROUP = 16
ATT_MAX_DILATION = 16
ATT_DOUBLE_BUFFER_BYTES = 16 * 2**20


def _attn_kernel(slopes_ref, k_ref, v_ref, qb_ref, kb_ref, vb_ref, o_ref, xf, qs, kd, vd, bias, sc_buf, p_buf, m_buf,
                 m_s, l_s, a_s, *, seq):
    pair = pl.program_id(1)
    lane = lax.broadcasted_iota(jnp.int32, (1, LANES), 1)
    first_head = lane < ATT_HEAD_DIM
    slopes = (slopes_ref[2 * pair], slopes_ref[2 * pair + 1])
    base_len = seq // ATT_MAX_DILATION

    def conv_rows(c):
        return pl.ds(pl.multiple_of(c * ATT_CONV_ROWS, ATT_CONV_ROWS), ATT_CONV_ROWS)

    def store_split(dst_ref, rows, x, fill):
        other = jnp.full_like(x, fill)
        dst_ref[0, rows, :] = jnp.where(first_head, x, other)
        dst_ref[1, rows, :] = jnp.where(first_head, other, x)

    def class_major_rows(t, r):
        blocks_per_class = seq // r // ATT_QBLK
        rho = t // blocks_per_class
        l0 = (t % blocks_per_class) * ATT_QBLK
        return pl.ds(rho + r * l0, ATT_QBLK, stride=r)

    def widen_q(c, carry):
        qs[conv_rows(c), :] = qb_ref[0, conv_rows(c), :].astype(F32)
        return carry

    lax.fori_loop(0, seq // ATT_CONV_ROWS, widen_q, 0)

    for pattern, r in enumerate(DILATIONS):
        sub_len = seq // r
        n_keys = min(2 * ATT_QBLK, sub_len)
        blocks_per_class = sub_len // ATT_QBLK
        n_runs = ATT_MAX_DILATION // r
        run_len = ATT_QBLK // n_runs

        n_kruns = 1 if r == 1 else n_runs
        krun_len = n_keys // n_kruns
        if r == 1:
            def copy(c, carry):
                store_split(kd, conv_rows(c), k_ref[0, conv_rows(c), :], 0.0)
                store_split(vd, conv_rows(c), v_ref[0, conv_rows(c), :], 1.0)
                return carry

            lax.fori_loop(0, seq // ATT_CONV_ROWS, copy, 0)
        elif pattern == 1:
            def copy_base(c, carry):
                store_split(kd, conv_rows(c), kb_ref[0, conv_rows(c), :], 0.0)
                store_split(vd, conv_rows(c), vb_ref[0, conv_rows(c), :], 1.0)
                return carry

            lax.fori_loop(0, seq // ATT_CONV_ROWS, copy_base, 0)

        n_idx = lax.broadcasted_iota(jnp.int32, (ATT_QBLK, n_keys), 0)
        q_rel = n_runs * (n_idx % run_len) + n_idx // run_len
        c_idx = lax.broadcasted_iota(jnp.int32, (ATT_QBLK, n_keys), 1)
        k_rel = n_kruns * (c_idx % krun_len) + c_idx // krun_len
        offsets = (0, ATT_HALF_WINDOW, 2 * ATT_HALF_WINDOW) if blocks_per_class > 1 else (0,)
        for variant, off in enumerate(offsets):
            dist = jnp.abs(k_rel - off - q_rel)
            for h in range(2):
                scaled = (-slopes[h] * float(r)) * dist.astype(F32)
                bias[h, variant, :, 0:n_keys] = jnp.where(dist <= ATT_HALF_WINDOW, scaled, NEG)

        def geometry(t, r=r, blocks_per_class=blocks_per_class, n_runs=n_runs, run_len=run_len,
                     n_kruns=n_kruns, krun_len=krun_len, n_keys=n_keys):
            rho = t // blocks_per_class
            lb = t % blocks_per_class
            if blocks_per_class > 1:
                variant = jnp.where(lb == 0, 0, jnp.where(lb == blocks_per_class - 1, 2, 1))
            else:
                variant = 0
            runs = [pl.ds(pl.multiple_of((rho + r * j) * base_len + lb * run_len, 8), run_len)
                    for j in range(n_runs)]
            if r == 1:
                k_runs = [pl.ds(pl.multiple_of(t * ATT_QBLK - variant * ATT_HALF_WINDOW, ATT_HALF_WINDOW), n_keys)]
            else:
                start = lb * run_len - variant * (ATT_HALF_WINDOW // n_kruns)
                k_runs = [pl.ds(pl.multiple_of((rho + r * j) * base_len + start, 16), krun_len)
                          for j in range(n_kruns)]
            return variant, k_runs, runs

        def load_runs(ref, runs):
            return jnp.concatenate([ref[rows, :] for rows in runs], axis=0) if len(runs) > 1 else ref[runs[0], :]

        def store_runs(ref, runs, x, run_len=run_len):
            for j, rows in enumerate(runs):
                ref[rows, :] = x[j * run_len:(j + 1) * run_len, :]

        def group(g, carry, n_keys=n_keys, pattern=pattern, geometry=geometry, load_runs=load_runs,
                  store_runs=store_runs):
            for i in range(ATT_GROUP):
                variant, k_runs, runs = geometry(g * ATT_GROUP + i)
                q = load_runs(qs, runs).astype(BF16)
                for h in range(2):
                    sc = lax.dot_general(q, load_runs(kd.at[h], k_runs), (((1,), (1,)), ((), ())),
                                         preferred_element_type=F32)
                    sc_buf[i, h, :, 0:n_keys] = sc + bias[h, variant, :, 0:n_keys]
            for i in range(ATT_GROUP):
                ms = []
                for h in range(2):
                    sc = sc_buf[i, h, :, 0:n_keys]
                    m = jnp.max(sc, axis=-1, keepdims=True)
                    p_buf[i, h, :, 0:n_keys] = jnp.exp(sc - m).astype(BF16)
                    ms.append(m)
                m_buf[i] = jnp.where(first_head, ms[0], ms[1])
            for i in range(ATT_GROUP):
                _, k_runs, runs = geometry(g * ATT_GROUP + i)
                pv0 = jnp.dot(p_buf[i, 0, :, 0:n_keys], load_runs(vd.at[0], k_runs), preferred_element_type=F32)
                pv1 = jnp.dot(p_buf[i, 1, :, 0:n_keys], load_runs(vd.at[1], k_runs), preferred_element_type=F32)
                a_new = jnp.where(first_head, pv0, pv1)
                l_new = pltpu.roll(jnp.where(first_head, pv1, pv0), ATT_HEAD_DIM, 1)
                m_new = m_buf[i]
                if pattern == 0:
                    store_runs(m_s, runs, m_new)
                    store_runs(l_s, runs, l_new)
                    store_runs(a_s, runs, a_new)
                else:
                    m_old = load_runs(m_s, runs)
                    m_tot = jnp.maximum(m_old, m_new)
                    w_old = jnp.exp(m_old - m_tot)
                    w_new = jnp.exp(m_new - m_tot)
                    store_runs(m_s, runs, m_tot)
                    store_runs(l_s, runs, w_old * load_runs(l_s, runs) + w_new * l_new)
                    store_runs(a_s, runs, w_old * load_runs(a_s, runs) + w_new * a_new)
            return carry

        lax.fori_loop(0, seq // (ATT_QBLK * ATT_GROUP), group, 0)

    def scatter(t, carry):
        rows = pl.ds(pl.multiple_of(t * ATT_QBLK, ATT_QBLK), ATT_QBLK)
        xf[class_major_rows(t, ATT_MAX_DILATION), :] = a_s[rows, :] / l_s[rows, :]
        return carry

    lax.fori_loop(0, seq // ATT_QBLK, scatter, 0)

    def finish(c, carry):
        o_ref[0, conv_rows(c), :] = xf[conv_rows(c), :].astype(BF16)
        return carry

    lax.fori_loop(0, seq // ATT_CONV_ROWS, finish, 0)


def _attention(qkv, qkv_base, slopes):
    b, s, _ = qkv.shape
    n_pairs = ATT_WIDTH // LANES
    mode = {} if s * LANES * 2 * 5 * 2 <= ATT_DOUBLE_BUFFER_BYTES else {"pipeline_mode": pl.Buffered(1)}
    blk = lambda off: pl.BlockSpec((1, s, LANES), lambda bi, j, off=off: (bi, 0, off + j), **mode)
    return pl.pallas_call(
        functools.partial(_attn_kernel, seq=s),
        grid=(b, n_pairs),
        in_specs=[pl.BlockSpec(memory_space=pltpu.SMEM), blk(n_pairs), blk(2 * n_pairs),
                  blk(0), blk(n_pairs), blk(2 * n_pairs)],
        out_specs=pl.BlockSpec((1, s, LANES), lambda bi, j: (bi, 0, j)),
        out_shape=jax.ShapeDtypeStruct((b, s, ATT_WIDTH), BF16),
        scratch_shapes=[
            pltpu.VMEM((s, LANES), F32),
            pltpu.VMEM((s, LANES), F32),
            pltpu.VMEM((2, s, LANES), BF16),
            pltpu.VMEM((2, s, LANES), BF16),
            pltpu.VMEM((2, 3, ATT_QBLK, 2 * ATT_QBLK), F32),
            pltpu.VMEM((ATT_GROUP, 2, ATT_QBLK, 2 * ATT_QBLK), F32),
            pltpu.VMEM((ATT_GROUP, 2, ATT_QBLK, 2 * ATT_QBLK), BF16),
            pltpu.VMEM((ATT_GROUP, ATT_QBLK, LANES), F32),
            pltpu.VMEM((s, LANES), F32),
            pltpu.VMEM((s, LANES), F32),
            pltpu.VMEM((s, LANES), F32),
        ],
        compiler_params=_params(("parallel", "parallel")),
        name="dilated_attention",
    )(slopes, qkv, qkv, qkv_base, qkv_base, qkv_base)


def _even_out_kernel(x_ref, f_ref, a_ref, w_ref, o_ref):
    acc = x_ref[0]
    acc = acc + jnp.dot(f_ref[...], w_ref[0:FNET_WIDTH, :], preferred_element_type=F32)
    acc = acc + jnp.dot(a_ref[0], w_ref[FNET_WIDTH:, :], preferred_element_type=F32)
    o_ref[0] = acc


def _even_out(x, fmix, att, w_out, tm=512):
    b, s, d = x.shape
    return pl.pallas_call(
        _even_out_kernel,
        grid=(b, s // tm),
        in_specs=[
            pl.BlockSpec((1, tm, d), lambda bi, i: (bi, i, 0)),
            pl.BlockSpec((tm, FNET_WIDTH), lambda bi, i: (i, bi)),
            pl.BlockSpec((1, tm, ATT_WIDTH), lambda bi, i: (bi, i, 0)),
            pl.BlockSpec((d, d), lambda bi, i: (0, 0)),
        ],
        out_specs=pl.BlockSpec((1, tm, d), lambda bi, i: (bi, i, 0)),
        out_shape=jax.ShapeDtypeStruct((b, s, d), F32),
        compiler_params=_params(("parallel", "parallel")),
        name="even_out_proj",
    )(x, fmix, att, w_out)


def _resident(shape):
    return pl.BlockSpec(shape, lambda *_: (0,) * len(shape), pipeline_mode=pl.Buffered(1))


def _swiglu_kernel(x_ref, ln_ref, wg_ref, wu_ref, wd_ref, o_ref):
    x = x_ref[...]
    h = _rms(x, ln_ref[...]).astype(BF16)
    g = jnp.dot(h, wg_ref[...], preferred_element_type=F32)
    u = jnp.dot(h, wu_ref[...], preferred_element_type=F32)
    act = (_silu(g) * u).astype(BF16)
    o_ref[...] = x + jnp.dot(act, wd_ref[...], preferred_element_type=F32)


def _swiglu(x, ln, wg, wu, wd, tm=512):
    n, d = x.shape
    dff = wg.shape[1]
    return pl.pallas_call(
        _swiglu_kernel,
        grid=(n // tm,),
        in_specs=[
            pl.BlockSpec((tm, d), lambda i: (i, 0)),
            _resident((1, d)), _resident((d, dff)), _resident((d, dff)), _resident((dff, d)),
        ],
        out_specs=pl.BlockSpec((tm, d), lambda i: (i, 0)),
        out_shape=jax.ShapeDtypeStruct((n, d), F32),
        compiler_params=_params(("parallel",)),
        name="swiglu_ffn",
    )(x, ln, wg, wu, wd)


ODD_IN_COLS = 2048


def _odd_in_kernel(x_ref, ln_ref, w_ref, o_ref):
    h = _rms(x_ref[...], ln_ref[...]).astype(BF16)
    for c0 in range(0, ODD_IN, ODD_IN_COLS):
        cols = slice(c0, c0 + ODD_IN_COLS)
        o_ref[:, cols] = jnp.dot(h, w_ref[:, cols], preferred_element_type=F32).astype(BF16)


def _odd_in(x, ln, w_in, tm=512):
    n, d = x.shape
    return pl.pallas_call(
        _odd_in_kernel,
        grid=(n // tm,),
        in_specs=[pl.BlockSpec((tm, d), lambda i: (i, 0)), _resident((1, d)), _resident((d, ODD_IN))],
        out_specs=pl.BlockSpec((tm, ODD_IN), lambda i: (i, 0)),
        out_shape=jax.ShapeDtypeStruct((n, ODD_IN), BF16),
        compiler_params=_params(("parallel",)),
        name="odd_in_proj",
    )(x, ln, w_in)


RET_CHUNK = 256
RET_TILE = 1024


def _chunk_positions():
    return lax.broadcasted_iota(jnp.int32, (RET_CHUNK, 1), 0).astype(F32)


def _kt_v(k_scaled, v):
    return lax.dot_general(k_scaled, v, (((0,), (0,)), ((), ())), preferred_element_type=F32)


def _ret_bwd_state_kernel(lgb_ref, k_ref, v_ref, sb_ref, state):
    @pl.when(pl.program_id(1) == 0)
    def _():
        state[...] = jnp.zeros_like(state)

    pos = _chunk_positions()
    for head in range(RET_HEADS):
        lg = lgb_ref[head]
        key_decay = jnp.exp(lg * pos)
        chunk_decay = jnp.exp(jnp.full((1, RET_DV), lg * RET_CHUNK, F32))
        k_cols = slice(head * RET_DK, (head + 1) * RET_DK)
        v_cols = slice(head * RET_DV, (head + 1) * RET_DV)
        for c in reversed(range(RET_TILE // RET_CHUNK)):
            rows = slice(c * RET_CHUNK, (c + 1) * RET_CHUNK)
            sb_ref[0, head, c] = state[head].astype(BF16)
            kd = (k_ref[0, rows, k_cols].astype(F32) * key_decay).astype(BF16)
            state[head] = state[head] * chunk_decay + _kt_v(kd, v_ref[0, rows, v_cols])


def _ret_bwd_states(z, lgb):
    b, s, _ = z.shape
    n = s // RET_CHUNK
    n_tiles = s // RET_TILE
    per_tile = RET_TILE // RET_CHUNK
    return pl.pallas_call(
        _ret_bwd_state_kernel,
        grid=(b, n_tiles),
        in_specs=[
            pl.BlockSpec(memory_space=pltpu.SMEM),
            pl.BlockSpec((1, RET_TILE, RET_QK_WIDTH), lambda bi, t: (bi, n_tiles - 1 - t, 1)),
            pl.BlockSpec((1, RET_TILE, RET_V_WIDTH), lambda bi, t: (bi, n_tiles - 1 - t, 1)),
        ],
        out_specs=pl.BlockSpec((1, RET_HEADS, per_tile, RET_DK, RET_DV), lambda bi, t: (bi, 0, n_tiles - 1 - t, 0, 0)),
        out_shape=jax.ShapeDtypeStruct((b, RET_HEADS, n, RET_DK, RET_DV), BF16),
        scratch_shapes=[pltpu.VMEM((RET_HEADS, RET_DK, RET_DV), F32)],
        compiler_params=_params(("parallel", "arbitrary")),
        name="retention_bwd_states",
    )(lgb, z, z)


RET_MAIN_TILE = 512


def _ret_main_kernel(lgf_ref, lgb_ref, q_ref, k_ref, v_ref, g_ref, sb_ref, gn_ref, o_ref, state):
    @pl.when(pl.program_id(1) == 0)
    def _():
        state[...] = jnp.zeros_like(state)

    c = RET_CHUNK
    pos = _chunk_positions()
    rel = (lax.broadcasted_iota(jnp.int32, (c, c), 0) - lax.broadcasted_iota(jnp.int32, (c, c), 1)).astype(F32)
    for head in range(RET_HEADS):
        lgf = lgf_ref[head]
        lgb = lgb_ref[head]
        decay = jnp.exp(jnp.where(rel >= 0, lgf * rel, -lgb * rel))
        q_decay_f = jnp.exp(lgf * (pos + 1.0))
        q_decay_b = jnp.exp(lgb * (c - pos))
        k_decay_f = jnp.exp(lgf * (c - 1.0 - pos))
        chunk_decay = jnp.exp(jnp.full((1, RET_DV), lgf * c, F32))
        qk_cols = slice(head * RET_DK, (head + 1) * RET_DK)
        v_cols = slice(head * RET_DV, (head + 1) * RET_DV)
        gn = gn_ref[:, v_cols]
        for ci in range(RET_MAIN_TILE // RET_CHUNK):
            rows = slice(ci * c, (ci + 1) * c)
            q = q_ref[0, rows, qk_cols]
            k = k_ref[0, rows, qk_cols]
            v = v_ref[0, rows, v_cols]
            qf32 = q.astype(F32)
            sc = lax.dot_general(q, k, (((1,), (1,)), ((), ())), preferred_element_type=F32)
            y = jnp.dot((sc * decay).astype(BF16), v, preferred_element_type=F32)
            y = y + jnp.dot((qf32 * q_decay_f).astype(BF16), state[head].astype(BF16), preferred_element_type=F32)
            y = y + jnp.dot((qf32 * q_decay_b).astype(BF16), sb_ref[0, head, ci], preferred_element_type=F32)
            kd = (k.astype(F32) * k_decay_f).astype(BF16)
            state[head] = state[head] * chunk_decay + _kt_v(kd, v)
            yn = _rms(y, gn)
            o_ref[0, rows, v_cols] = (_silu(g_ref[0, rows, v_cols].astype(F32)) * yn).astype(BF16)


def _ret_main(z, sb, lgf, lgb, gn):
    b, s, _ = z.shape
    n_tiles = s // RET_MAIN_TILE
    per_tile = RET_MAIN_TILE // RET_CHUNK
    smem = pl.BlockSpec(memory_space=pltpu.SMEM)
    return pl.pallas_call(
        _ret_main_kernel,
        grid=(b, n_tiles),
        in_specs=[
            smem, smem,
            pl.BlockSpec((1, RET_MAIN_TILE, RET_QK_WIDTH), lambda bi, t: (bi, t, 0)),
            pl.BlockSpec((1, RET_MAIN_TILE, RET_QK_WIDTH), lambda bi, t: (bi, t, 1)),
            pl.BlockSpec((1, RET_MAIN_TILE, RET_V_WIDTH), lambda bi, t: (bi, t, 1)),
            pl.BlockSpec((1, RET_MAIN_TILE, RET_V_WIDTH), lambda bi, t: (bi, t, 2)),
            pl.BlockSpec((1, RET_HEADS, per_tile, RET_DK, RET_DV), lambda bi, t: (bi, 0, t, 0, 0)),
            pl.BlockSpec((1, RET_V_WIDTH), lambda bi, t: (0, 0)),
        ],
        out_specs=pl.BlockSpec((1, RET_MAIN_TILE, RET_V_WIDTH), lambda bi, t: (bi, t, 0)),
        out_shape=jax.ShapeDtypeStruct((b, s, RET_V_WIDTH), BF16),
        scratch_shapes=[pltpu.VMEM((RET_HEADS, RET_DK, RET_DV), F32)],
        compiler_params=_params(("parallel", "arbitrary")),
        name="retention_main",
    )(lgf, lgb, z, z, z, z, sb, gn)


def _odd_out_router_kernel(y_ref, x_ref, w_ref, ln_ref, wr_ref, xo_ref, h_ref, idx_ref, gate_ref, hn_prev):
    @pl.when(pl.program_id(0) == 0)
    def _():
        hn_prev[...] = jnp.zeros_like(hn_prev)

    prev = hn_prev[...]
    cols = [jnp.sum(prev * wr_ref[e:e + 1, :], axis=-1, keepdims=True) for e in range(N_EXPERTS)]
    pad = jnp.full((prev.shape[0], LANES - N_EXPERTS), -jnp.inf, F32)
    logits = jnp.transpose(jnp.concatenate(cols + [pad], axis=1))[0:N_EXPERTS, :]
    row = lax.broadcasted_iota(jnp.int32, logits.shape, 0).astype(F32)
    none = float(N_EXPERTS)
    m1 = jnp.max(logits, axis=0, keepdims=True)
    i1 = jnp.min(jnp.where(logits == m1, row, none), axis=0, keepdims=True)
    rest = jnp.where(row == i1, -jnp.inf, logits)
    m2 = jnp.max(rest, axis=0, keepdims=True)
    i2 = jnp.min(jnp.where(rest == m2, row, none), axis=0, keepdims=True)
    e2 = jnp.exp(m2 - m1)
    idx_ref[0:1, :] = i1.astype(jnp.int32)
    idx_ref[1:2, :] = i2.astype(jnp.int32)
    gate_ref[0:1, :] = 1.0 / (1.0 + e2)
    gate_ref[1:2, :] = e2 / (1.0 + e2)

    x = x_ref[...] + jnp.dot(y_ref[...], w_ref[...], preferred_element_type=F32)
    xo_ref[...] = x
    hn = _rms(x, ln_ref[...])
    h_ref[...] = hn.astype(BF16)
    hn_prev[...] = hn


def _odd_out_router(y, x, w_out, ln, wr_t, tm=512):
    n, d = x.shape
    n_tiles = n // tm
    full = lambda shp: pl.BlockSpec(shp, lambda i: (0,) * len(shp))
    cur = lambda i: (jnp.minimum(i, n_tiles - 1), 0)
    prev = lambda i: (0, jnp.maximum(i - 1, 0))
    return pl.pallas_call(
        _odd_out_router_kernel,
        grid=(n_tiles + 1,),
        in_specs=[
            pl.BlockSpec((tm, RET_V_WIDTH), cur),
            pl.BlockSpec((tm, d), cur),
            full((RET_V_WIDTH, d)), full((1, d)), full((N_EXPERTS, d)),
        ],
        out_specs=[
            pl.BlockSpec((tm, d), cur),
            pl.BlockSpec((tm, d), cur),
            pl.BlockSpec((2, tm), prev),
            pl.BlockSpec((2, tm), prev),
        ],
        out_shape=[
            jax.ShapeDtypeStruct((n, d), F32),
            jax.ShapeDtypeStruct((n, d), BF16),
            jax.ShapeDtypeStruct((2, n), jnp.int32),
            jax.ShapeDtypeStruct((2, n), F32),
        ],
        scratch_shapes=[pltpu.VMEM((tm, d), F32)],
        compiler_params=_params(("arbitrary",)),
        name="odd_out_proj_router",
    )(y, x, w_out, ln, wr_t)


MOE_TM = 1024
MOE_TF = 1792
MOE_SUB = 512


def _moe_kernel(tile_expert_ref, n_used_ref, xs_ref, gate_ref, w1_ref, w3_ref, w2_ref, o_ref, acc_ref):
    t = pl.program_id(0)
    f = pl.program_id(1)
    last = pl.num_programs(1) - 1
    used = t < n_used_ref[0]

    @pl.when(used)
    def _():
        for r0 in range(0, MOE_TM, MOE_SUB):
            rows = slice(r0, r0 + MOE_SUB)
            xs = xs_ref[rows, :]
            g = jnp.dot(xs, w1_ref[...], preferred_element_type=F32)
            u = jnp.dot(xs, w3_ref[...], preferred_element_type=F32)
            act = (_silu(g) * u).astype(BF16)
            part = jnp.dot(act, w2_ref[...], preferred_element_type=F32)

            @pl.when(f == 0)
            def _():
                acc_ref[rows, :] = part

            @pl.when(jnp.logical_and(f > 0, f < last))
            def _():
                acc_ref[rows, :] += part

            @pl.when(f == last)
            def _():
                o_ref[rows, :] = ((acc_ref[rows, :] + part) * gate_ref[rows, 0:1]).astype(BF16)

    @pl.when(jnp.logical_and(jnp.logical_not(used), f == last))
    def _():
        o_ref[...] = jnp.zeros_like(o_ref)


def _moe_experts(xs, gate_rows, tile_expert, n_used, w1, w3, w2):
    p, d = xs.shape
    n_tiles = p // MOE_TM
    n_f = D_FF_EXPERT // MOE_TF
    assert n_f >= 2 and n_f * MOE_TF == D_FF_EXPERT

    def row_tile(t, f, te, nu):
        return (jnp.minimum(t, nu[0] - 1), 0)

    def f_idx(t, f, nu):
        return jnp.where(t < nu[0], f, n_f - 1)

    grid_spec = pltpu.PrefetchScalarGridSpec(
        num_scalar_prefetch=2,
        grid=(n_tiles, n_f),
        in_specs=[
            pl.BlockSpec((MOE_TM, d), row_tile),
            pl.BlockSpec((MOE_TM, LANES), row_tile),
            pl.BlockSpec((None, d, MOE_TF), lambda t, f, te, nu: (te[t], 0, f_idx(t, f, nu))),
            pl.BlockSpec((None, d, MOE_TF), lambda t, f, te, nu: (te[t], 0, f_idx(t, f, nu))),
            pl.BlockSpec((None, MOE_TF, d), lambda t, f, te, nu: (te[t], f_idx(t, f, nu), 0)),
        ],
        out_specs=pl.BlockSpec((MOE_TM, d), lambda t, f, te, nu: (t, 0)),
        scratch_shapes=[pltpu.VMEM((MOE_TM, d), F32)],
    )
    return pl.pallas_call(
        _moe_kernel,
        grid_spec=grid_spec,
        out_shape=jax.ShapeDtypeStruct((p, d), BF16),
        compiler_params=_params(("arbitrary", "arbitrary")),
        name="moe_experts",
    )(tile_expert, n_used, xs, gate_rows, w1, w3, w2)


def _combine_kernel(x_ref, y_ref, o_ref):
    o_ref[...] = x_ref[...] + y_ref[0].astype(F32) + y_ref[1].astype(F32)


def _combine(x, y2, tm=1024):
    n, d = x.shape
    return pl.pallas_call(
        _combine_kernel,
        grid=(n // tm,),
        in_specs=[pl.BlockSpec((tm, d), lambda i: (i, 0)), pl.BlockSpec((2, tm, d), lambda i: (0, i, 0))],
        out_specs=pl.BlockSpec((tm, d), lambda i: (i, 0)),
        out_shape=jax.ShapeDtypeStruct((n, d), F32),
        compiler_params=_params(("parallel",)),
        name="moe_combine",
    )(x, y2)


def _route(idx, gates, n):
    e_flat = idx.reshape(-1)
    n_pairs = e_flat.shape[0]
    pair_id = jnp.arange(n_pairs, dtype=jnp.int32)
    order = (jnp.sort(e_flat * n_pairs + pair_id) % n_pairs).astype(jnp.int32)
    rank = jnp.argsort(order).astype(jnp.int32)
    counts = jnp.sum(e_flat[None, :] == jnp.arange(N_EXPERTS, dtype=jnp.int32)[:, None], axis=1).astype(jnp.int32)
    starts = jnp.cumsum(counts) - counts
    tiles_per_expert = (counts + MOE_TM - 1) // MOE_TM
    tile_ends = jnp.cumsum(tiles_per_expert)
    padded_starts = (tile_ends - tiles_per_expert) * MOE_TM
    dest = rank + (padded_starts - starts)[e_flat]
    n_rows = n_pairs + N_EXPERTS * MOE_TM
    n_tiles = n_rows // MOE_TM
    tile_expert = jnp.minimum(
        jnp.sum(jnp.arange(n_tiles, dtype=jnp.int32)[:, None] >= tile_ends[None, :], axis=1), N_EXPERTS - 1
    ).astype(jnp.int32)
    row_expert = jnp.repeat(tile_expert, MOE_TM)
    row_in_expert = jnp.arange(n_rows, dtype=jnp.int32) - padded_starts[row_expert]
    row_valid = row_in_expert < counts[row_expert]
    row_pair = order[jnp.clip(starts[row_expert] + row_in_expert, 0, n_pairs - 1)]
    src_token = jnp.where(row_valid, row_pair % n, 0).astype(jnp.int32)
    row_gate = jnp.where(row_valid, gates.reshape(-1)[row_pair], 0.0)
    n_used = tile_ends[-1:].astype(jnp.int32)
    return src_token, row_gate, dest, tile_expert, n_used


def _channel_dft():
    c = jnp.arange(FNET_GROUP_DIM, dtype=jnp.int32)
    ang = ((c[:, None] * c[None, :]) % FNET_GROUP_DIM).astype(F32) * (2.0 * math.pi / FNET_GROUP_DIM)
    eye = jnp.eye(FNET_GROUPS, dtype=F32)
    scale = FNET_GROUP_DIM ** -0.5
    return jnp.concatenate([jnp.kron(eye, jnp.cos(ang)) * scale, -jnp.kron(eye, jnp.sin(ang)) * scale], axis=1).astype(BF16)


def _sequence_dft(s):
    hi = s // 64
    k = jnp.arange(s, dtype=jnp.int32)
    j_hi = jnp.arange(hi, dtype=jnp.int32)
    j_lo = jnp.arange(64, dtype=jnp.int32)
    w = 2.0 * math.pi / s
    ang_a = (((j_hi[:, None] * k[None, :]) % hi) * 64).astype(F32) * w
    ang_b = ((j_lo[:, None] * k[None, :]) % s).astype(F32) * w
    scale = s ** -0.5
    ca, sa = jnp.cos(ang_a) * scale, jnp.sin(ang_a) * scale
    cb, sb = jnp.cos(ang_b), jnp.sin(ang_b)
    ca2 = jnp.concatenate([ca, ca], axis=1)[:, None, :]
    sa2 = jnp.concatenate([sa, sa], axis=1)[:, None, :]
    p = jnp.concatenate([cb, sb], axis=1)[None, :, :]
    q = jnp.concatenate([sb, -cb], axis=1)[None, :, :]
    return (ca2 * p - sa2 * q).reshape(s, 2 * s).astype(BF16)


def _trunk_to_router(x, p):
    b, s, d = x.shape
    n = b * s
    y, qkv, qkv_base = _even_in(x, p["ln_mix_e"], p["w_in_e"], p["chan_dft"], p["head_sum"], p["qn"], p["kn"])
    fmix = _matmul(_sequence_dft(s), y.reshape(2 * s, b * FNET_WIDTH))
    att = _attention(qkv, qkv_base.reshape(b, s, 3 * ATT_WIDTH), p["slopes"])
    x = _even_out(x, fmix, att, p["w_out_e"])
    x = _swiglu(x.reshape(n, d), p["ln_ffn_e"], p["w_gate_e"], p["w_up_e"], p["w_down_e"])
    z = _odd_in(x, p["ln_mix_o"], p["w_in_o"]).reshape(b, s, ODD_IN)
    sb = _ret_bwd_states(z, p["lgb"])
    yr = _ret_main(z, sb, p["lgf"], p["lgb"], p["ret_gn"])
    x, h, idx, gates = _odd_out_router(yr.reshape(n, RET_V_WIDTH), x, p["w_out_o"], p["ln_ffn_o"], p["w_router_t"])
    return x, h, _route(idx, gates, n)


def _dispatch(h, route):
    src_token, row_gate = route[0], route[1]
    xs = h.at[src_token].get(mode="promise_in_bounds")
    return xs, jnp.broadcast_to(row_gate[:, None], (row_gate.shape[0], LANES))


def _experts(xs, gate_rows, route, p):
    return _moe_experts(xs, gate_rows, route[3], route[4], p["moe_w1"], p["moe_w3"], p["moe_w2"])


def _collect(ye, route, n):
    return ye.at[route[2]].get(mode="promise_in_bounds").reshape(2, n, D_MODEL)


def _both_trunks(x_a, x_b, p):
    outs = []
    for x in (x_a, x_b):
        n = x.shape[0] * x.shape[1]
        xr, h, route = _trunk_to_router(x, p)
        xs, gate_rows = _dispatch(h, route)
        ye = _experts(xs, gate_rows, route, p)
        outs.append(_combine(xr, _collect(ye, route, n)).reshape(x.shape))
    return tuple(outs)


def kernel(x_prompt, x_sample, ln_mix_e, w_in_e, w_out_e, qn_e, kn_e, ln_ffn_e, w_gate_e, w_up_e, w_down_e, ln_mix_o, w_in_o, w_out_o, logdecay_fwd, logdecay_bwd, ret_gn, ln_ffn_o, w_router, moe_w1, moe_w3, moe_w2):
    row = lambda w: w.reshape(1, -1).astype(F32)
    k_scale = jnp.concatenate([
        jnp.ones((RET_QK_WIDTH,), F32), jnp.full((RET_QK_WIDTH,), RET_DK ** -0.5, F32),
        jnp.ones((2 * RET_V_WIDTH,), F32)])
    head_id = jnp.arange(LANES, dtype=jnp.int32) // ATT_HEAD_DIM
    p = {
        "ln_mix_e": row(ln_mix_e[0]),
        "w_in_e": w_in_e[0].astype(BF16),
        "w_out_e": w_out_e[0].astype(BF16),
        "qn": row(jnp.tile(qn_e[0], 2) * (ATT_HEAD_DIM ** -0.5)),
        "kn": row(jnp.tile(kn_e[0], 2)),
        "ln_ffn_e": row(ln_ffn_e[0]),
        "w_gate_e": w_gate_e[0].astype(BF16),
        "w_up_e": w_up_e[0].astype(BF16),
        "w_down_e": w_down_e[0].astype(BF16),
        "ln_mix_o": row(ln_mix_o[0]),
        "w_in_o": (w_in_o[0] * k_scale[None, :]).astype(BF16),
        "w_out_o": w_out_o[0].astype(BF16),
        "lgf": logdecay_fwd[0].astype(F32),
        "lgb": logdecay_bwd[0].astype(F32),
        "ret_gn": row(ret_gn[0]),
        "ln_ffn_o": row(ln_ffn_o[0]),
        "w_router_t": w_router[0].T.astype(F32),
        "moe_w1": moe_w1[0].astype(BF16),
        "moe_w3": moe_w3[0].astype(BF16),
        "moe_w2": moe_w2[0].astype(BF16),
        "chan_dft": _channel_dft(),
        "head_sum": (head_id[:, None] == head_id[None, :]).astype(BF16),
        "slopes": jnp.exp2(-8.0 * (jnp.arange(ATT_HEADS, dtype=F32) + 1.0) / ATT_HEADS),
    }
    return _both_trunks(x_prompt, x_sample, p)
```
